```python
import jax
import jax.numpy as jnp
from jax import lax
import numpy as np


D_MODEL = 1024
BATCH = 2
SEQ = 8192
DEPTH = 4

HEAD_DIM = 64
RET_HEADS = 4
DSA_HEADS = 4
DIL_HEADS = 8
RET_W = RET_HEADS * HEAD_DIM
DSA_W = DSA_HEADS * HEAD_DIM
DIL_W = DIL_HEADS * HEAD_DIM
D_MIX = RET_W + DSA_W + DIL_W
RET_CHUNK = 128
RET_ROPE_BASE = 10000.0
KV_LATENT = 128
IDX_HEADS = 8
IDX_DIM = 64
TOPK_MAX = 256
QUERY_BLOCK = 128
DIL_PATTERNS = ((128, 1), (512, 4), (2048, 16))
D_FF = 4 * D_MODEL
NORM_EPS = 1e-6
RET_COLS = 4 * RET_W
DSA_SIZES = (DSA_W, KV_LATENT, IDX_HEADS * IDX_DIM, IDX_DIM, IDX_HEADS)
DSA_COLS = sum(DSA_SIZES)
DIL_COLS = 3 * DIL_W
N_IN = RET_COLS + DSA_COLS + DIL_COLS

kernel_name = 'hymba_style_retention_dsa_dilated_hybrid'


def _offsets(sizes):
    return [int(o) for o in np.cumsum(sizes)[:-1]]


def rms_norm(x, w):
    xf = x.astype(jnp.float32)
    y = xf * lax.rsqrt(jnp.mean(xf * xf, axis=-1, keepdims=True) + NORM_EPS)
    return (y * w.astype(jnp.float32)).astype(x.dtype)


def rotary(x, pos):
    half = x.shape[-1] // 2
    inv = RET_ROPE_BASE ** (-jnp.arange(half, dtype=jnp.float32) / half)
    ang = pos.astype(jnp.float32)[:, None] * inv[None, :]
    cos = jnp.cos(ang)[None, :, None, :]
    sin = jnp.sin(ang)[None, :, None, :]
    xf = x.astype(jnp.float32)
    x1, x2 = xf[..., :half], xf[..., half:]
    return jnp.concatenate([x1 * cos - x2 * sin, x2 * cos + x1 * sin], axis=-1)


def retention_chunkwise(q, k, v):
    B, S, H, d = q.shape
    C = RET_CHUNK
    N = S // C
    log_g = jnp.log(1.0 - 2.0 ** (-5.0 - jnp.arange(H, dtype=jnp.float32)))

    def chunks(t):
        return t.reshape(B, N, C, H, d).transpose(0, 3, 1, 2, 4)

    qc, kc, vc = chunks(q), chunks(k * (d ** -0.5)), chunks(v)
    i = jnp.arange(C, dtype=jnp.float32)
    diff = i[:, None] - i[None, :]
    decay = jnp.exp(jnp.maximum(diff, 0.0)[None] * log_g[:, None, None]) * (diff >= 0)[None]
    inner = jnp.einsum('bhncd,bhnmd->bhncm', qc, kc) * decay[None, :, None]
    inner = jnp.einsum('bhncm,bhnme->bhnce', inner, vc)
    zeta = jnp.exp((C - 1.0 - i)[None, :] * log_g[:, None])
    xi = jnp.exp((i + 1.0)[None, :] * log_g[:, None])
    kv = jnp.einsum('bhnmd,bhnme->nbhde', kc * zeta[None, :, None, :, None], vc)
    g_chunk = jnp.exp(C * log_g)[None, :, None, None]

    def step(state, kv_n):
        return g_chunk * state + kv_n, state

    _, state_prev = lax.scan(step, jnp.zeros((B, H, d, d), jnp.float32), kv)
    cross = jnp.einsum('bhncd,nbhde->bhnce', qc, state_prev) * xi[None, :, None, :, None]
    o = inner + cross
    return o.transpose(0, 2, 3, 1, 4).reshape(B, S, H, d)


def retention_mixer(cols, norm_w, pos):
    B, S, _ = cols.shape
    q, k, v, g = jnp.split(cols, 4, axis=-1)
    shp = (B, S, RET_HEADS, HEAD_DIM)
    o = retention_chunkwise(rotary(q.reshape(shp), pos), rotary(k.reshape(shp), pos),
                            v.reshape(shp).astype(jnp.float32))
    mu = jnp.mean(o, axis=-1, keepdims=True)
    var = jnp.mean(jnp.square(o - mu), axis=-1, keepdims=True)
    o = (o - mu) * lax.rsqrt(var + NORM_EPS) * norm_w.astype(jnp.float32)
    o = o.reshape(B, S, RET_W)
    return (jax.nn.silu(g.astype(jnp.float32)) * o).astype(cols.dtype)


def dsa_mixer(cols, kv_norm_w, w_uk, w_uv):
    B, S, _ = cols.shape
    q, c_kv, q_idx, k_idx, w_idx = jnp.split(cols, _offsets(DSA_SIZES), axis=-1)
    q = q.reshape(B, S, DSA_HEADS, HEAD_DIM)
    c_kv = rms_norm(c_kv, kv_norm_w)
    k = (c_kv @ w_uk).reshape(B, S, DSA_HEADS, HEAD_DIM)
    v = (c_kv @ w_uv).reshape(B, S, DSA_HEADS, HEAD_DIM)
    q_idx = q_idx.reshape(B, S, IDX_HEADS, IDX_DIM)
    w_idx = w_idx * (IDX_HEADS ** -0.5 * IDX_DIM ** -0.5)
    n_sel = min(TOPK_MAX, S // 4)
    nblk = S // QUERY_BLOCK
    key_pos = jnp.arange(S)

    def to_blocks(t):
        return jnp.moveaxis(t.reshape((B, nblk, QUERY_BLOCK) + t.shape[2:]), 1, 0)

    def block(args):
        qb, qib, wb, start = args
        qpos = start + jnp.arange(QUERY_BLOCK)
        logits = jnp.einsum('bqhd,bsd->bqhs', qib, k_idx)
        score = jnp.einsum('bqhs,bqh->bqs', jax.nn.relu(logits), wb).astype(jnp.float32)
        causal = key_pos[None, :] <= qpos[:, None]
        score = jnp.where(causal[None], score, -jnp.inf)
        _, idx = lax.top_k(score, n_sel)
        k_sel = jax.vmap(lambda kb, ib: kb[ib])(k, idx)
        v_sel = jax.vmap(lambda vb, ib: vb[ib])(v, idx)
        s = jnp.einsum('bqhd,bqnhd->bhqn', qb, k_sel).astype(jnp.float32) * (HEAD_DIM ** -0.5)
        valid = (idx <= qpos[None, :, None])[:, None]
        p = jax.nn.softmax(jnp.where(valid, s, -jnp.inf), axis=-1).astype(v.dtype)
        return jnp.einsum('bhqn,bqnhd->bqhd', p, v_sel)

    starts = jnp.arange(nblk) * QUERY_BLOCK
    o = lax.map(block, (to_blocks(q), to_blocks(q_idx), to_blocks(w_idx), starts))
    return jnp.moveaxis(o, 0, 1).reshape(B, S, DSA_W).astype(cols.dtype)


def dilated_branch(q, k, v, window, dilation):
    B, S, H, d = q.shape
    steps = window // dilation
    n = S // dilation
    lb = min(steps, n)
    nb = -(-n // lb)
    pad = nb * lb - n

    def to_sub(t):
        t = t.reshape(B, n, dilation, H, d).transpose(0, 2, 3, 1, 4)
        t = jnp.pad(t, ((0, 0), (0, 0), (0, 0), (0, pad), (0, 0)))
        return t.reshape(B, dilation, H, nb, lb, d)

    def with_prev(t):
        prev = jnp.pad(t, ((0, 0), (0, 0), (0, 0), (1, 0), (0, 0), (0, 0)))[:, :, :, :-1]
        return jnp.concatenate([prev, t], axis=4)

    qs = to_sub(q)
    kk = with_prev(to_sub(k))
    vv = with_prev(to_sub(v)).astype(jnp.float32)
    qi = jnp.arange(lb)[:, None]
    kj = jnp.arange(2 * lb)[None, :]
    dist = lb + qi - kj
    key_idx = jnp.arange(nb)[:, None, None] * lb - lb + kj[None]
    mask = (dist >= 0) & (dist <= steps) & (key_idx >= 0)
    s = jnp.einsum('brhnqd,brhnkd->brhnqk', qs, kk).astype(jnp.float32) * (d ** -0.5)
    s = jnp.where(mask, s, -jnp.inf)
    m = jnp.max(s, axis=-1, keepdims=True)
    p = jnp.exp(s - m)
    den = jnp.sum(p, axis=-1, keepdims=True)
    o = jnp.einsum('brhnqk,brhnkd->brhnqd', p, vv) / den
    lse = (m + jnp.log(den))[..., 0]
    o = o.reshape(B, dilation, H, nb * lb, d)[:, :, :, :n].transpose(0, 3, 1, 2, 4).reshape(B, S, H, d)
    lse = lse.reshape(B, dilation, H, nb * lb)[:, :, :, :n].transpose(0, 3, 1, 2).reshape(B, S, H)
    return o, lse


def dilated_mixer(cols):
    B, S, _ = cols.shape
    q, k, v = jnp.split(cols, 3, axis=-1)
    shp = (B, S, DIL_HEADS, HEAD_DIM)
    q, k, v = q.reshape(shp), k.reshape(shp), v.reshape(shp)
    outs = []
    lses = []
    for window, dilation in DIL_PATTERNS:
        o, lse = dilated_branch(q, k, v, window, dilation)
        outs.append(o)
        lses.append(lse)
    alpha = jax.nn.softmax(jnp.stack(lses, axis=0), axis=0)
    o = jnp.sum(alpha[..., None] * jnp.stack(outs, axis=0), axis=0)
    return o.reshape(B, S, DIL_W).astype(cols.dtype)


def hybrid_layer(x, attn_norm_w, w_in, ret_norm_w, kv_norm_w, w_uk, w_uv, w_out, mlp_norm_w, w_up, w_down, pos):
    h = rms_norm(x, attn_norm_w)
    proj = h @ w_in
    ret_cols, dsa_cols, dil_cols = jnp.split(proj, _offsets((RET_COLS, DSA_COLS, DIL_COLS)), axis=-1)
    mixed = jnp.concatenate([retention_mixer(ret_cols, ret_norm_w, pos),
                             dsa_mixer(dsa_cols, kv_norm_w, w_uk, w_uv),
                             dilated_mixer(dil_cols)], axis=-1)
    x = x + mixed @ w_out
    h = rms_norm(x, mlp_norm_w)
    return x + jnp.square(jax.nn.relu(h @ w_up)) @ w_down


def setup_inputs(seed: int = 0) -> dict:
    key = jax.random.key(seed)
    ks = jax.random.split(key, 12)
    f32 = jnp.float32

    def nrm(k, shape, scale):
        return jax.random.normal(k, shape, f32) * scale

    return {
        'x': nrm(ks[0], (BATCH, SEQ, D_MODEL), 1.0),
        'attn_norm_w': 1.0 + nrm(ks[1], (DEPTH, D_MODEL), 0.02),
        'w_in': nrm(ks[2], (DEPTH, D_MODEL, N_IN), D_MODEL ** -0.5),
        'ret_norm_w': 1.0 + nrm(ks[3], (DEPTH, RET_HEADS, HEAD_DIM), 0.02),
        'dsa_kv_norm_w': 1.0 + nrm(ks[4], (DEPTH, KV_LATENT), 0.02),
        'dsa_w_uk': nrm(ks[5], (DEPTH, KV_LATENT, DSA_W), KV_LATENT ** -0.5),
        'dsa_w_uv': nrm(ks[6], (DEPTH, KV_LATENT, DSA_W), KV_LATENT ** -0.5),
        'w_out': nrm(ks[7], (DEPTH, D_MIX, D_MODEL), D_MIX ** -0.5),
        'mlp_norm_w': 1.0 + nrm(ks[8], (DEPTH, D_MODEL), 0.02),
        'w_up': nrm(ks[9], (DEPTH, D_MODEL, D_FF), D_MODEL ** -0.5),
        'w_down': nrm(ks[10], (DEPTH, D_FF, D_MODEL), D_FF ** -0.5),
        'final_norm_w': 1.0 + nrm(ks[11], (D_MODEL,), 0.02),
    }


def reference(x, attn_norm_w, w_in, ret_norm_w, dsa_kv_norm_w, dsa_w_uk, dsa_w_uv, w_out, mlp_norm_w, w_up, w_down, final_norm_w):
    pos = jnp.arange(x.shape[1])
    for layer in range(DEPTH):
        x = hybrid_layer(x, attn_norm_w[layer], w_in[layer], ret_norm_w[layer], dsa_kv_norm_w[layer],
                         dsa_w_uk[layer], dsa_w_uv[layer], w_out[layer], mlp_norm_w[layer],
                         w_up[layer], w_down[layer], pos)
    return rms_norm(x, final_norm_w)
```

```python
import functools
import math

import numpy as np
import jax
import jax.numpy as jnp
from jax import lax
from jax.experimental import pallas as pl
from jax.experimental.pallas import tpu as pltpu

F32 = jnp.float32
BF16 = jnp.bfloat16
I32 = jnp.int32

HEAD_DIM = 64
RET_HEADS = 4
DSA_HEADS = 4
DIL_HEADS = 8
RET_W = RET_HEADS * HEAD_DIM
DSA_W = DSA_HEADS * HEAD_DIM
DIL_W = DIL_HEADS * HEAD_DIM
RET_CHUNK = 128
RET_ROPE_BASE = 10000.0
KV_LATENT = 128
IDX_HEADS = 8
IDX_DIM = 64
TOPK_MAX = 256
DIL_PATTERNS = ((128, 1), (512, 4), (2048, 16))
DIL_BLOCK = 128
NORM_EPS = 1e-6

V7X_VMEM_LIMIT_BYTES = 56 * 1024 * 1024

ROW_TILE = 512
DSA_QB = 128
DSA_KC = 256
FF_CHUNK = 1024

WIDX_ROWS = 16

INT_MIN = -(2 ** 31)
KEY_NEG_INF = int(np.int32(np.uint32(0xFF800000) ^ np.uint32(0x7FFFFFFF)))
MASKED_SCORE = -1e30


def _params(*semantics):
    return pltpu.CompilerParams(dimension_semantics=semantics, vmem_limit_bytes=V7X_VMEM_LIMIT_BYTES)


def _resident(shape, index_map):
    return pl.BlockSpec(shape, index_map, pipeline_mode=pl.Buffered(1))


def _dot(a, b):
    return jnp.dot(a, b, preferred_element_type=F32)


def _dot_nt(a, b):
    return lax.dot_general(a, b, (((1,), (1,)), ((), ())), preferred_element_type=F32)


def _rms(x, w):
    return x * lax.rsqrt(jnp.mean(x * x, axis=-1, keepdims=True) + NORM_EPS) * w


def _inproj_kernel(x_ref, nw_ref, w_ref, wt_ref, ret_ref, dq_ref, dk_ref, dv_ref, ckv_ref, kidx_ref,
                   qidxT_ref, dsaqT_ref, widxT_ref):
    h = _rms(x_ref[...], nw_ref[...]).astype(BF16)
    o = 0
    for ref in (ret_ref, dq_ref, dk_ref, dv_ref, ckv_ref, kidx_ref):
        n = ref.shape[-1]
        ref[...] = _dot(h, w_ref[:, o:o + n]).astype(ref.dtype)
        o += n
    o = 0
    for ref in (qidxT_ref, dsaqT_ref, widxT_ref):
        n = ref.shape[-2]
        ref[0] = _dot_nt(wt_ref[o:o + n, :], h).astype(ref.dtype)
        o += n


def _inproj(x2, nw, w_main, w_t, B, S):
    T, D = x2.shape
    tm = ROW_TILE
    nt = S // tm
    widx_rows = w_t.shape[0] - IDX_HEADS * IDX_DIM - DSA_W
    row = lambda i: (i, 0)
    tr = lambda i: (i // nt, 0, i % nt)
    out_shape = (
        jax.ShapeDtypeStruct((T, 4 * RET_W), F32),
        jax.ShapeDtypeStruct((T, DIL_W), BF16),
        jax.ShapeDtypeStruct((T, DIL_W), BF16),
        jax.ShapeDtypeStruct((T, DIL_W), BF16),
        jax.ShapeDtypeStruct((T, KV_LATENT), F32),
        jax.ShapeDtypeStruct((T, IDX_DIM), BF16),
        jax.ShapeDtypeStruct((B, IDX_HEADS * IDX_DIM, S), BF16),
        jax.ShapeDtypeStruct((B, DSA_W, S), BF16),
        jax.ShapeDtypeStruct((B, widx_rows, S), F32),
    )
    out_specs = (
        pl.BlockSpec((tm, 4 * RET_W), row),
        pl.BlockSpec((tm, DIL_W), row),
        pl.BlockSpec((tm, DIL_W), row),
        pl.BlockSpec((tm, DIL_W), row),
        pl.BlockSpec((tm, KV_LATENT), row),
        pl.BlockSpec((tm, IDX_DIM), row),
        pl.BlockSpec((1, IDX_HEADS * IDX_DIM, tm), tr),
        pl.BlockSpec((1, DSA_W, tm), tr),
        pl.BlockSpec((1, widx_rows, tm), tr),
    )
    return pl.pallas_call(
        _inproj_kernel,
        grid=(T // tm,),
        in_specs=[
            pl.BlockSpec((tm, D), row),
            _resident((1, D), lambda i: (0, 0)),
            _resident(w_main.shape, lambda i: (0, 0)),
            _resident(w_t.shape, lambda i: (0, 0)),
        ],
        out_specs=out_specs,
        out_shape=out_shape,
        compiler_params=_params("parallel"),
        name="inproj",
    )(x2, nw, w_main, w_t)


def _retention_tables(S):
    half = HEAD_DIM // 2
    inv = RET_ROPE_BASE ** (-jnp.arange(half, dtype=F32) / half)
    ang = jnp.arange(S, dtype=F32)[:, None] * inv[None, :]
    cos, sin = jnp.cos(ang), jnp.sin(ang)
    cos_h = jnp.concatenate([cos, cos], axis=-1)
    sin_h = jnp.concatenate([-sin, sin], axis=-1)
    cosf = jnp.tile(cos_h, (1, RET_HEADS))
    sinf = jnp.tile(sin_h, (1, RET_HEADS))
    C = RET_CHUNK
    log_g = np.log(1.0 - 2.0 ** (-5.0 - np.arange(RET_HEADS, dtype=np.float64)))
    i = np.arange(C, dtype=np.float64)
    diff = i[:, None] - i[None, :]
    decay = np.exp(np.maximum(diff, 0.0)[None] * log_g[:, None, None]) * (diff >= 0)[None]
    zeta = np.exp((C - 1.0 - i)[None, :] * log_g[:, None])
    xi = np.exp((i + 1.0)[None, :] * log_g[:, None])
    g_chunk = np.exp(C * log_g)
    rep = lambda t: np.repeat(t.T, HEAD_DIM, axis=1)
    g_rows = np.repeat(g_chunk, HEAD_DIM)[:, None] * np.ones((1, RET_W))
    return (cosf, sinf, jnp.asarray(decay, F32), jnp.asarray(rep(zeta), F32), jnp.asarray(rep(xi), F32),
            jnp.asarray(g_rows, F32))


def _retention_kernel(cols_ref, cos_ref, sin_ref, decay_ref, zeta_ref, xi_ref, grow_ref, nw_ref, o_ref, state_ref):
    W = RET_W

    @pl.when(pl.program_id(1) == 0)
    def _():
        state_ref[...] = jnp.zeros_like(state_ref)

    lane = lax.broadcasted_iota(I32, (1, W), 1)
    first_half = (lane % HEAD_DIM) < (HEAD_DIM // 2)
    cosf, sinf = cos_ref[...], sin_ref[...]

    def rot(t):
        partner = jnp.where(first_half, pltpu.roll(t, W - HEAD_DIM // 2, 1), pltpu.roll(t, HEAD_DIM // 2, 1))
        return t * cosf + partner * sinf

    q = rot(cols_ref[:, 0:W])
    k = rot(cols_ref[:, W:2 * W]) * (HEAD_DIM ** -0.5)
    v = cols_ref[:, 2 * W:3 * W].astype(BF16)
    g = cols_ref[:, 3 * W:4 * W]

    head_of_lane = lane // HEAD_DIM
    kb = k.astype(BF16)
    inner = jnp.zeros((RET_CHUNK, W), F32)
    for h in range(RET_HEADS):
        sel = head_of_lane == h
        qh = jnp.where(sel, q, 0.0).astype(BF16)
        a = (_dot_nt(qh, kb) * decay_ref[h]).astype(BF16)
        inner = inner + jnp.where(sel, _dot(a, v), 0.0)

    state = state_ref[...]
    cross = _dot(q.astype(BF16), state.astype(BF16)) * xi_ref[...]
    o = inner + cross

    kzT = (k * zeta_ref[...]).T.astype(BF16)
    kv = _dot(kzT, v)
    r_head = lax.broadcasted_iota(I32, (W, W), 0) // HEAD_DIM
    c_head = lax.broadcasted_iota(I32, (W, W), 1) // HEAD_DIM
    state_ref[...] = grow_ref[...] * state + jnp.where(r_head == c_head, kv, 0.0)

    mu = jnp.zeros_like(o)
    for h in range(RET_HEADS):
        sel = head_of_lane == h
        mu = mu + jnp.where(sel, jnp.sum(jnp.where(sel, o, 0.0), axis=-1, keepdims=True), 0.0)
    mu = mu * (1.0 / HEAD_DIM)
    d = o - mu
    var = jnp.zeros_like(o)
    for h in range(RET_HEADS):
        sel = head_of_lane == h
        var = var + jnp.where(sel, jnp.sum(jnp.where(sel, d * d, 0.0), axis=-1, keepdims=True), 0.0)
    var = var * (1.0 / HEAD_DIM)
    y = d * lax.rsqrt(var + NORM_EPS) * nw_ref[...]
    o_ref[...] = (jax.nn.silu(g) * y).astype(o_ref.dtype)


def _retention(ret_cols, tables, norm_w, B, S):
    T = ret_cols.shape[0]
    C, W = RET_CHUNK, RET_W
    n = S // C
    cosf, sinf, decay, zeta, xi, g_rows = tables
    tok = lambda b, j: (b * n + j, 0)
    pos = lambda b, j: (j, 0)
    const2 = lambda b, j: (0, 0)
    return pl.pallas_call(
        _retention_kernel,
        grid=(B, n),
        in_specs=[
            pl.BlockSpec((C, 4 * W), tok),
            pl.BlockSpec((C, W), pos),
            pl.BlockSpec((C, W), pos),
            _resident(decay.shape, lambda b, j: (0, 0, 0)),
            _resident((C, W), const2),
            _resident((C, W), const2),
            _resident((W, W), const2),
            _resident((1, W), const2),
        ],
        out_specs=pl.BlockSpec((C, W), tok),
        out_shape=jax.ShapeDtypeStruct((T, W), BF16),
        scratch_shapes=[pltpu.VMEM((W, W), F32)],
        compiler_params=_params("parallel", "arbitrary"),
        name="retention",
    )(ret_cols, cosf, sinf, decay, zeta, xi, g_rows, norm_w)


def _dsa_kv_kernel(ckv_ref, nw_ref, wuk_ref, wuvT_ref, k_ref, vT_ref):
    c = _rms(ckv_ref[...], nw_ref[...]).astype(BF16)
    k_ref[...] = _dot(c, wuk_ref[...]).astype(k_ref.dtype)
    vT_ref[0] = _dot_nt(wuvT_ref[...], c).astype(vT_ref.dtype)


def _dsa_kv(ckv, nw, w_uk, w_uvT):
    T = ckv.shape[0]
    tm = DSA_KC
    return pl.pallas_call(
        _dsa_kv_kernel,
        grid=(T // tm,),
        in_specs=[
            pl.BlockSpec((tm, KV_LATENT), lambda i: (i, 0)),
            _resident((1, KV_LATENT), lambda i: (0, 0)),
            _resident(w_uk.shape, lambda i: (0, 0)),
            _resident(w_uvT.shape, lambda i: (0, 0)),
        ],
        out_specs=(pl.BlockSpec((tm, DSA_W), lambda i: (i, 0)),
                   pl.BlockSpec((1, DSA_W, tm), lambda i: (i, 0, 0))),
        out_shape=(jax.ShapeDtypeStruct((T, DSA_W), BF16),
                   jax.ShapeDtypeStruct((T // tm, DSA_W, tm), BF16)),
        compiler_params=_params("parallel"),
        name="dsa_kv",
    )(ckv, nw, w_uk, w_uvT)


def _sortable(x):
    b = pltpu.bitcast(x, I32)
    return b ^ ((b >> 31) & 0x7FFFFFFF)


def _dsa_kernel(qidxT_ref, dsaqT_ref, widxT_ref, kidx_ref, k_ref, vT_ref, o_ref, key_ref, cut_ref, *, n_sel, S):
    QB, KC = DSA_QB, DSA_KC
    i = pl.program_id(1)
    n_chunks = (i * QB + QB + KC - 1) // KC
    qpos = i * QB + lax.broadcasted_iota(I32, (1, QB), 1)
    krow = lax.broadcasted_iota(I32, (KC, 1), 0)
    w_scale = IDX_HEADS ** -0.5 * IDX_DIM ** -0.5
    widxT = widxT_ref[0] * w_scale
    qidxT = qidxT_ref[0]

    def score_chunk(c, carry):
        off = pl.multiple_of(c * KC, KC)
        kc = kidx_ref[pl.ds(off, KC), :]
        acc = jnp.zeros((KC, QB), F32)
        for h in range(IDX_HEADS):
            logit = _dot(kc, qidxT[h * IDX_DIM:(h + 1) * IDX_DIM, :])
            acc = acc + jnp.maximum(logit, 0.0) * widxT[h:h + 1, :]
        acc = jnp.where(acc == 0.0, 0.0, acc)
        score = jnp.where(off + krow <= qpos, acc, -jnp.inf)
        key_ref[pl.ds(off, KC), :] = _sortable(score)
        return carry

    lax.fori_loop(0, n_chunks, score_chunk, 0)

    def count(pred):
        def body(c, cnt):
            off = pl.multiple_of(c * KC, KC)
            hit = jnp.where(pred(key_ref[pl.ds(off, KC), :], off + krow), 1.0, 0.0)
            return cnt + jnp.sum(hit.reshape(KC // 8, 8, QB), axis=0)
        cnt8 = lax.fori_loop(0, n_chunks, body, jnp.zeros((8, QB), F32))
        return jnp.sum(cnt8, axis=0, keepdims=True)

    def bit_step(t, ans):
        cand = ans | jnp.left_shift(jnp.int32(1), 31 - t)
        cand_s = cand ^ INT_MIN
        cnt = count(lambda key, _: key >= cand_s)
        return jnp.where(cnt >= n_sel, cand, ans)

    thr = lax.fori_loop(0, 32, bit_step, jnp.zeros((1, QB), I32)) ^ INT_MIN

    n_gt = count(lambda key, _: key > thr)
    n_ge = count(lambda key, _: key >= thr)
    tie_take = n_sel - n_gt
    needs_cut = (n_ge > n_sel) & (thr > KEY_NEG_INF)
    cut_ref[...] = jnp.full((1, QB), S, I32)

    @pl.when(jnp.max(jnp.where(needs_cut, 1.0, 0.0)) > 0.0)
    def _():
        def cut_step(t, cpos):
            cand = cpos | jnp.left_shift(jnp.int32(1), (S - 1).bit_length() - 1 - t)
            before = count(lambda key, kp: (key == thr) & (kp < cand))
            return jnp.where(before < tie_take, cand, cpos)
        cpos = lax.fori_loop(0, (S - 1).bit_length(), cut_step, jnp.zeros((1, QB), I32))
        cut_ref[...] = jnp.where(needs_cut, cpos, S)

    cut = cut_ref[...]

    row_head = lax.broadcasted_iota(I32, (DSA_W, 1), 0) // HEAD_DIM
    dsaqT = dsaqT_ref[0]
    q_heads = [jnp.where(row_head == h, dsaqT, jnp.zeros_like(dsaqT)) for h in range(DSA_HEADS)]

    def attend_chunk(c, carry):
        ms, ls, accs = carry
        off = pl.multiple_of(c * KC, KC)
        key = key_ref[pl.ds(off, KC), :]
        sel = ((key > thr) | ((key == thr) & (off + krow <= cut))) & (key > KEY_NEG_INF)
        kch = k_ref[pl.ds(off, KC), :]
        vch = vT_ref[c]
        new_m, new_l, new_acc = [], [], []
        for h in range(DSA_HEADS):
            s = _dot(kch, q_heads[h]) * (HEAD_DIM ** -0.5)
            s = jnp.where(sel, s, MASKED_SCORE)
            m_new = jnp.maximum(ms[h], jnp.max(s, axis=0, keepdims=True))
            alpha = jnp.exp(ms[h] - m_new)
            p = jnp.where(sel, jnp.exp(s - m_new), 0.0)
            new_m.append(m_new)
            new_l.append(alpha * ls[h] + jnp.sum(p, axis=0, keepdims=True))
            pv = _dot(vch[h * HEAD_DIM:(h + 1) * HEAD_DIM, :], p.astype(BF16))
            new_acc.append(alpha * accs[h] + pv)
        return tuple(new_m), tuple(new_l), tuple(new_acc)

    init = (tuple(jnp.full((1, QB), MASKED_SCORE, F32) for _ in range(DSA_HEADS)),
            tuple(jnp.zeros((1, QB), F32) for _ in range(DSA_HEADS)),
            tuple(jnp.zeros((HEAD_DIM, QB), F32) for _ in range(DSA_HEADS)))
    _, ls, accs = lax.fori_loop(0, n_chunks, attend_chunk, init)
    oT = jnp.concatenate([accs[h] / ls[h] for h in range(DSA_HEADS)], axis=0)
    o_ref[...] = oT.T.astype(o_ref.dtype)


def _dsa(qidxT, dsaqT, widxT, kidx, k, vT, B, S):
    QB, KC = DSA_QB, DSA_KC
    nq = S // QB
    nkc = S // KC
    n_sel = min(TOPK_MAX, S // 4)
    T = B * S
    kern = functools.partial(_dsa_kernel, n_sel=n_sel, S=S)
    return pl.pallas_call(
        kern,
        grid=(B, nq),
        in_specs=[
            pl.BlockSpec((1, IDX_HEADS * IDX_DIM, QB), lambda b, i: (b, 0, i)),
            pl.BlockSpec((1, DSA_W, QB), lambda b, i: (b, 0, i)),
            pl.BlockSpec((1, widxT.shape[1], QB), lambda b, i: (b, 0, i)),
            pl.BlockSpec((S, IDX_DIM), lambda b, i: (b, 0)),
            pl.BlockSpec((S, DSA_W), lambda b, i: (b, 0)),
            pl.BlockSpec((nkc, DSA_W, KC), lambda b, i: (b, 0, 0)),
        ],
        out_specs=pl.BlockSpec((QB, DSA_W), lambda b, i: (b * nq + i, 0)),
        out_shape=jax.ShapeDtypeStruct((T, DSA_W), BF16),
        scratch_shapes=[pltpu.VMEM((S, QB), I32), pltpu.VMEM((1, QB), I32)],
        compiler_params=_params("parallel", "arbitrary"),
        name="dsa",
    )(qidxT, dsaqT, widxT, kidx, k, vT)


def _dilated_kernel(q_ref, kp_ref, kc_ref, vp_ref, vc_ref, o_ref, lse_ref, *, steps):
    L = DIL_BLOCK
    j = pl.program_id(2)
    qi = lax.broadcasted_iota(I32, (L, 2 * L), 0)
    kj = lax.broadcasted_iota(I32, (L, 2 * L), 1)
    dist = L + qi - kj
    mask = (dist >= 0) & (dist <= steps) & (j * L - L + kj >= 0)
    lane = lax.broadcasted_iota(I32, (1, 2 * HEAD_DIM), 1)
    low = lane < HEAD_DIM
    for pair in range(DIL_HEADS // 2):
        cs = slice(pair * 2 * HEAD_DIM, (pair + 1) * 2 * HEAD_DIM)
        q = q_ref[0, :, cs]
        kk = jnp.concatenate([kp_ref[0, :, cs], kc_ref[0, :, cs]], axis=0)
        vv = jnp.concatenate([vp_ref[0, :, cs], vc_ref[0, :, cs]], axis=0)
        outs, lses = [], []
        for sub in range(2):
            qh = jnp.where(low if sub == 0 else ~low, q, jnp.zeros_like(q))
            s = _dot_nt(qh, kk) * (HEAD_DIM ** -0.5)
            s = jnp.where(mask, s, -jnp.inf)
            m = jnp.max(s, axis=-1, keepdims=True)
            p = jnp.exp(s - m)
            den = jnp.sum(p, axis=-1, keepdims=True)
            outs.append(_dot(p.astype(BF16), vv) / den)
            lses.append(m + jnp.log(den))
        o_ref[0, :, cs] = jnp.where(low, outs[0], outs[1])
        lse_ref[0, :, cs] = jnp.where(low, lses[0], lses[1])


def _dilated(q, k, v, B, S, window, dilation):
    steps = window // dilation
    n = S // dilation
    L = DIL_BLOCK
    assert steps == L and n % L == 0, "dilated branch supports window/dilation == 128 and S % (128*dilation) == 0"
    W = DIL_W
    view = lambda t: t.reshape(B, n, dilation * W)
    own = lambda b, r, j: (b, j, r)
    prev = lambda b, r, j: (b, jnp.maximum(j - 1, 0), r)
    blk = (1, L, W)
    o, lse = pl.pallas_call(
        functools.partial(_dilated_kernel, steps=steps),
        grid=(B, dilation, n // L),
        in_specs=[pl.BlockSpec(blk, own), pl.BlockSpec(blk, prev), pl.BlockSpec(blk, own),
                  pl.BlockSpec(blk, prev), pl.BlockSpec(blk, own)],
        out_specs=(pl.BlockSpec(blk, own), pl.BlockSpec(blk, own)),
        out_shape=(jax.ShapeDtypeStruct((B, n, dilation * W), F32),
                   jax.ShapeDtypeStruct((B, n, dilation * W), F32)),
        compiler_params=_params("parallel", "parallel", "parallel"),
        name=f"dilated_d{dilation}",
    )(view(q), view(k), view(k), view(v), view(v))
    return o.reshape(B * S, W), lse.reshape(B * S, W)


def _dil_merge_kernel(o1, o2, o3, l1, l2, l3, out_ref):
    a, b, c = l1[...], l2[...], l3[...]
    m = jnp.maximum(jnp.maximum(a, b), c)
    ea, eb, ec = jnp.exp(a - m), jnp.exp(b - m), jnp.exp(c - m)
    tot = ea + eb + ec
    out_ref[...] = ((ea * o1[...] + eb * o2[...] + ec * o3[...]) / tot).astype(out_ref.dtype)


def _dil_merge(os, lses):
    T, W = os[0].shape
    tm = ROW_TILE
    spec = pl.BlockSpec((tm, W), lambda i: (i, 0))
    return pl.pallas_call(
        _dil_merge_kernel,
        grid=(T // tm,),
        in_specs=[spec] * 6,
        out_specs=spec,
        out_shape=jax.ShapeDtypeStruct((T, W), BF16),
        compiler_params=_params("parallel"),
        name="dil_merge",
    )(*os, *lses)


def _mlp_kernel(x_ref, ret_ref, dsa_ref, dil_ref, wout_ref, nw_ref, wup_ref, wdown_ref, o_ref):
    mixed = (_dot(ret_ref[...], wout_ref[0:RET_W, :])
             + _dot(dsa_ref[...], wout_ref[RET_W:RET_W + DSA_W, :])
             + _dot(dil_ref[...], wout_ref[RET_W + DSA_W:, :]))
    x = x_ref[...] + mixed
    h = _rms(x, nw_ref[...]).astype(BF16)
    ff = None
    for c in range(wup_ref.shape[1] // FF_CHUNK):
        cs = slice(c * FF_CHUNK, (c + 1) * FF_CHUNK)
        u = jnp.maximum(_dot(h, wup_ref[:, cs]), 0.0)
        part = _dot((u * u).astype(BF16), wdown_ref[cs, :])
        ff = part if ff is None else ff + part
    o_ref[...] = x + ff


def _mlp(x2, ret_o, dsa_o, dil_o, w_out, nw, w_up, w_down):
    T, D = x2.shape
    tm = ROW_TILE
    row = lambda i: (i, 0)
    const = lambda i: (0, 0)
    return pl.pallas_call(
        _mlp_kernel,
        grid=(T // tm,),
        in_specs=[
            pl.BlockSpec((tm, D), row),
            pl.BlockSpec((tm, RET_W), row),
            pl.BlockSpec((tm, DSA_W), row),
            pl.BlockSpec((tm, DIL_W), row),
            _resident(w_out.shape, const),
            _resident((1, D), const),
            _resident(w_up.shape, const),
            _resident(w_down.shape, const),
        ],
        out_specs=pl.BlockSpec((tm, D), row),
        out_shape=jax.ShapeDtypeStruct((T, D), F32),
        compiler_params=_params("parallel"),
        name="outproj_mlp",
    )(x2, ret_o, dsa_o, dil_o, w_out, nw, w_up, w_down)


def _final_norm_kernel(x_ref, w_ref, o_ref):
    o_ref[...] = _rms(x_ref[...], w_ref[...])


def _final_norm(x2, w):
    T, D = x2.shape
    tm = ROW_TILE
    return pl.pallas_call(
        _final_norm_kernel,
        grid=(T // tm,),
        in_specs=[pl.BlockSpec((tm, D), lambda i: (i, 0)), _resident((1, D), lambda i: (0, 0))],
        out_specs=pl.BlockSpec((tm, D), lambda i: (i, 0)),
        out_shape=jax.ShapeDtypeStruct((T, D), F32),
        compiler_params=_params("parallel"),
        name="final_norm",
    )(x2, w)


def _split_w_in(w_in):
    ret_cols = 4 * RET_W
    o = ret_cols
    dsa_q = w_in[:, o:o + DSA_W]; o += DSA_W
    c_kv = w_in[:, o:o + KV_LATENT]; o += KV_LATENT
    q_idx = w_in[:, o:o + IDX_HEADS * IDX_DIM]; o += IDX_HEADS * IDX_DIM
    k_idx = w_in[:, o:o + IDX_DIM]; o += IDX_DIM
    w_idx = w_in[:, o:o + IDX_HEADS]; o += IDX_HEADS
    dil = w_in[:, o:]
    w_main = jnp.concatenate([w_in[:, :ret_cols], dil, c_kv, k_idx], axis=1).astype(BF16)
    pad = jnp.zeros((w_in.shape[0], WIDX_ROWS - IDX_HEADS), w_in.dtype)
    w_t = jnp.concatenate([q_idx, dsa_q, w_idx, pad], axis=1).T.astype(BF16)
    return w_main, w_t


def kernel(x, attn_norm_w, w_in, ret_norm_w, dsa_kv_norm_w, dsa_w_uk, dsa_w_uv, w_out, mlp_norm_w, w_up, w_down, final_norm_w):
    B, S, D = x.shape
    depth = w_in.shape[0]
    assert S % ROW_TILE == 0 and S % DSA_KC == 0 and S % RET_CHUNK == 0
    tables = _retention_tables(S)
    x2 = x.reshape(B * S, D)
    for layer in range(depth):
        w_main, w_t = _split_w_in(w_in[layer])
        ret_cols, dq, dk, dv, ckv, kidx, qidxT, dsaqT, widxT = _inproj(
            x2, attn_norm_w[layer][None, :], w_main, w_t, B, S)
        ret_o = _retention(ret_cols, tables, ret_norm_w[layer].reshape(1, RET_W), B, S)
        k, vT = _dsa_kv(ckv, dsa_kv_norm_w[layer][None, :], dsa_w_uk[layer].astype(BF16),
                        dsa_w_uv[layer].T.astype(BF16))
        dsa_o = _dsa(qidxT, dsaqT, widxT, kidx, k, vT, B, S)
        branches = [_dilated(dq, dk, dv, B, S, window, dilation) for window, dilation in DIL_PATTERNS]
        dil_o = _dil_merge([o for o, _ in branches], [l for _, l in branches])
        x2 = _mlp(x2, ret_o, dsa_o, dil_o, w_out[layer].astype(BF16), mlp_norm_w[layer][None, :],
                  w_up[layer].astype(BF16), w_down[layer].astype(BF16))
    return _final_norm(x2, final_norm_w[None, :]).reshape(B, S, D)
```

```python
import functools
import math

import numpy as np
import jax
import jax.numpy as jnp
from jax import lax
from jax.experimental import pallas as pl
from jax.experimental.pallas import tpu as pltpu

F32 = jnp.float32
BF16 = jnp.bfloat16
I32 = jnp.int32

HEAD_DIM = 64
RET_HEADS = 4
DSA_HEADS = 4
DIL_HEADS = 8
RET_W = RET_HEADS * HEAD_DIM
DSA_W = DSA_HEADS * HEAD_DIM
DIL_W = DIL_HEADS * HEAD_DIM
RET_CHUNK = 128
RET_ROPE_BASE = 10000.0
KV_LATENT = 128
IDX_HEADS = 8
IDX_DIM = 64
TOPK_MAX = 256
DIL_PATTERNS = ((128, 1), (512, 4), (2048, 16))
DIL_BLOCK = 128
NORM_EPS = 1e-6

V7X_VMEM_LIMIT_BYTES = 56 * 1024 * 1024

ROW_TILE = 512
DSA_QB = 256
DSA_KC = 256
DSA_SUB = 64
DSA_SCORE_ROWS = 128
DSA_IDX_GROUP = 4
FF_CHUNK = 1024

WIDX_ROWS = 16

INT_MIN = -(2 ** 31)
KEY_NEG_INF = int(np.int32(np.uint32(0xFF800000) ^ np.uint32(0x7FFFFFFF)))
MASKED_SCORE = -1e30


def _params(*semantics):
    return pltpu.CompilerParams(dimension_semantics=semantics, vmem_limit_bytes=V7X_VMEM_LIMIT_BYTES)


def _resident(shape, index_map):
    return pl.BlockSpec(shape, index_map, pipeline_mode=pl.Buffered(1))


def _dot(a, b):
    return jnp.dot(a, b, preferred_element_type=F32)


def _dot_nt(a, b):
    return lax.dot_general(a, b, (((1,), (1,)), ((), ())), preferred_element_type=F32)


def _rms(x, w):
    return x * lax.rsqrt(jnp.mean(x * x, axis=-1, keepdims=True) + NORM_EPS) * w


def _inproj_kernel(x_ref, nw_ref, w_ref, wt_ref, ret_ref, dq_ref, dk_ref, dv_ref, ckv_ref, kidx_ref,
                   qidxT_ref, dsaqT_ref, widxT_ref):
    h = _rms(x_ref[...], nw_ref[...]).astype(BF16)
    o = 0
    for ref in (ret_ref, dq_ref, dk_ref, dv_ref, ckv_ref, kidx_ref):
        n = ref.shape[-1]
        ref[...] = _dot(h, w_ref[:, o:o + n]).astype(ref.dtype)
        o += n
    o = 0
    for ref in (qidxT_ref, dsaqT_ref, widxT_ref):
        n = ref.shape[-2]
        ref[0] = _dot_nt(wt_ref[o:o + n, :], h).astype(ref.dtype)
        o += n


def _inproj(x2, nw, w_main, w_t, B, S):
    T, D = x2.shape
    tm = ROW_TILE
    nt = S // tm
    widx_rows = w_t.shape[0] - IDX_HEADS * IDX_DIM - DSA_W
    row = lambda i: (i, 0)
    tr = lambda i: (i // nt, 0, i % nt)
    out_shape = (
        jax.ShapeDtypeStruct((T, 4 * RET_W), F32),
        jax.ShapeDtypeStruct((T, DIL_W), BF16),
        jax.ShapeDtypeStruct((T, DIL_W), BF16),
        jax.ShapeDtypeStruct((T, DIL_W), BF16),
        jax.ShapeDtypeStruct((T, KV_LATENT), F32),
        jax.ShapeDtypeStruct((T, IDX_DIM), BF16),
        jax.ShapeDtypeStruct((B, IDX_HEADS * IDX_DIM, S), BF16),
        jax.ShapeDtypeStruct((B, DSA_W, S), BF16),
        jax.ShapeDtypeStruct((B, widx_rows, S), F32),
    )
    out_specs = (
        pl.BlockSpec((tm, 4 * RET_W), row),
        pl.BlockSpec((tm, DIL_W), row),
        pl.BlockSpec((tm, DIL_W), row),
        pl.BlockSpec((tm, DIL_W), row),
        pl.BlockSpec((tm, KV_LATENT), row),
        pl.BlockSpec((tm, IDX_DIM), row),
        pl.BlockSpec((1, IDX_HEADS * IDX_DIM, tm), tr),
        pl.BlockSpec((1, DSA_W, tm), tr),
        pl.BlockSpec((1, widx_rows, tm), tr),
    )
    return pl.pallas_call(
        _inproj_kernel,
        grid=(T // tm,),
        in_specs=[
            pl.BlockSpec((tm, D), row),
            _resident((1, D), lambda i: (0, 0)),
            _resident(w_main.shape, lambda i: (0, 0)),
            _resident(w_t.shape, lambda i: (0, 0)),
        ],
        out_specs=out_specs,
        out_shape=out_shape,
        compiler_params=_params("parallel"),
        name="inproj",
    )(x2, nw, w_main, w_t)


def _retention_tables(S):
    half = HEAD_DIM // 2
    inv = RET_ROPE_BASE ** (-jnp.arange(half, dtype=F32) / half)
    ang = jnp.arange(S, dtype=F32)[:, None] * inv[None, :]
    cos, sin = jnp.cos(ang), jnp.sin(ang)
    cos_h = jnp.concatenate([cos, cos], axis=-1)
    sin_h = jnp.concatenate([-sin, sin], axis=-1)
    cosf = jnp.tile(cos_h, (1, RET_HEADS))
    sinf = jnp.tile(sin_h, (1, RET_HEADS))
    C = RET_CHUNK
    log_g = np.log(1.0 - 2.0 ** (-5.0 - np.arange(RET_HEADS, dtype=np.float64)))
    i = np.arange(C, dtype=np.float64)
    diff = i[:, None] - i[None, :]
    decay = np.exp(np.maximum(diff, 0.0)[None] * log_g[:, None, None]) * (diff >= 0)[None]
    zeta = np.exp((C - 1.0 - i)[None, :] * log_g[:, None])
    xi = np.exp((i + 1.0)[None, :] * log_g[:, None])
    g_chunk = np.exp(C * log_g)
    rep = lambda t: np.repeat(t.T, HEAD_DIM, axis=1)
    g_rows = np.repeat(g_chunk, HEAD_DIM)[:, None] * np.ones((1, RET_W))
    return (cosf, sinf, jnp.asarray(decay, F32), jnp.asarray(rep(zeta), F32), jnp.asarray(rep(xi), F32),
            jnp.asarray(g_rows, F32))


def _retention_kernel(cols_ref, cos_ref, sin_ref, decay_ref, zeta_ref, xi_ref, grow_ref, nw_ref, o_ref, state_ref):
    W = RET_W

    @pl.when(pl.program_id(1) == 0)
    def _():
        state_ref[...] = jnp.zeros_like(state_ref)

    lane = lax.broadcasted_iota(I32, (1, W), 1)
    first_half = (lane % HEAD_DIM) < (HEAD_DIM // 2)
    cosf, sinf = cos_ref[...], sin_ref[...]

    def rot(t):
        partner = jnp.where(first_half, pltpu.roll(t, W - HEAD_DIM // 2, 1), pltpu.roll(t, HEAD_DIM // 2, 1))
        return t * cosf + partner * sinf

    q = rot(cols_ref[:, 0:W])
    k = rot(cols_ref[:, W:2 * W]) * (HEAD_DIM ** -0.5)
    v = cols_ref[:, 2 * W:3 * W].astype(BF16)
    g = cols_ref[:, 3 * W:4 * W]

    head_of_lane = lane // HEAD_DIM
    kb = k.astype(BF16)
    inner = jnp.zeros((RET_CHUNK, W), F32)
    for h in range(RET_HEADS):
        sel = head_of_lane == h
        qh = jnp.where(sel, q, 0.0).astype(BF16)
        a = (_dot_nt(qh, kb) * decay_ref[h]).astype(BF16)
        inner = inner + jnp.where(sel, _dot(a, v), 0.0)

    state = state_ref[...]
    cross = _dot(q.astype(BF16), state.astype(BF16)) * xi_ref[...]
    o = inner + cross

    kzT = (k * zeta_ref[...]).T.astype(BF16)
    kv = _dot(kzT, v)
    r_head = lax.broadcasted_iota(I32, (W, W), 0) // HEAD_DIM
    c_head = lax.broadcasted_iota(I32, (W, W), 1) // HEAD_DIM
    state_ref[...] = grow_ref[...] * state + jnp.where(r_head == c_head, kv, 0.0)

    mu = jnp.zeros_like(o)
    for h in range(RET_HEADS):
        sel = head_of_lane == h
        mu = mu + jnp.where(sel, jnp.sum(jnp.where(sel, o, 0.0), axis=-1, keepdims=True), 0.0)
    mu = mu * (1.0 / HEAD_DIM)
    d = o - mu
    var = jnp.zeros_like(o)
    for h in range(RET_HEADS):
        sel = head_of_lane == h
        var = var + jnp.where(sel, jnp.sum(jnp.where(sel, d * d, 0.0), axis=-1, keepdims=True), 0.0)
    var = var * (1.0 / HEAD_DIM)
    y = d * lax.rsqrt(var + NORM_EPS) * nw_ref[...]
    o_ref[...] = (jax.nn.silu(g) * y).astype(o_ref.dtype)


def _retention(ret_cols, tables, norm_w, B, S):
    T = ret_cols.shape[0]
    C, W = RET_CHUNK, RET_W
    n = S // C
    cosf, sinf, decay, zeta, xi, g_rows = tables
    tok = lambda b, j: (b * n + j, 0)
    pos = lambda b, j: (j, 0)
    const2 = lambda b, j: (0, 0)
    return pl.pallas_call(
        _retention_kernel,
        grid=(B, n),
        in_specs=[
            pl.BlockSpec((C, 4 * W), tok),
            pl.BlockSpec((C, W), pos),
            pl.BlockSpec((C, W), pos),
            _resident(decay.shape, lambda b, j: (0, 0, 0)),
            _resident((C, W), const2),
            _resident((C, W), const2),
            _resident((W, W), const2),
            _resident((1, W), const2),
        ],
        out_specs=pl.BlockSpec((C, W), tok),
        out_shape=jax.ShapeDtypeStruct((T, W), BF16),
        scratch_shapes=[pltpu.VMEM((W, W), F32)],
        compiler_params=_params("parallel", "arbitrary"),
        name="retention",
    )(ret_cols, cosf, sinf, decay, zeta, xi, g_rows, norm_w)


def _dsa_kv_kernel(ckv_ref, nw_ref, wuk_ref, wuvT_ref, k_ref, vT_ref):
    c = _rms(ckv_ref[...], nw_ref[...]).astype(BF16)
    k_ref[...] = _dot(c, wuk_ref[...]).astype(k_ref.dtype)
    vT_ref[0] = _dot_nt(wuvT_ref[...], c).astype(vT_ref.dtype)


def _dsa_kv(ckv, nw, w_uk, w_uvT):
    T = ckv.shape[0]
    tm = DSA_KC
    return pl.pallas_call(
        _dsa_kv_kernel,
        grid=(T // tm,),
        in_specs=[
            pl.BlockSpec((tm, KV_LATENT), lambda i: (i, 0)),
            _resident((1, KV_LATENT), lambda i: (0, 0)),
            _resident(w_uk.shape, lambda i: (0, 0)),
            _resident(w_uvT.shape, lambda i: (0, 0)),
        ],
        out_specs=(pl.BlockSpec((tm, DSA_W), lambda i: (i, 0)),
                   pl.BlockSpec((1, DSA_W, tm), lambda i: (i, 0, 0))),
        out_shape=(jax.ShapeDtypeStruct((T, DSA_W), BF16),
                   jax.ShapeDtypeStruct((T // tm, DSA_W, tm), BF16)),
        compiler_params=_params("parallel"),
        name="dsa_kv",
    )(ckv, nw, w_uk, w_uvT)


def _sortable(x):
    b = pltpu.bitcast(x, I32)
    return b ^ ((b >> 31) & 0x7FFFFFFF)


def _dsa_kernel(qidxT_ref, dsaqT_ref, widxT_ref, kidx_ref, k_ref, vT_ref, o_ref,
                key_ref, cut_ref, qh_ref, s_ref, bias_ref, p_ref, acc_ref, *, n_sel, S):
    QB, KC, SUB = DSA_QB, DSA_KC, DSA_SUB
    n_sub = KC // SUB
    i = pl.program_id(1)
    n_chunks = i + 1
    qpos = i * QB + lax.broadcasted_iota(I32, (1, QB), 1)
    krow = lax.broadcasted_iota(I32, (SUB, 1), 0)
    w_scale = IDX_HEADS ** -0.5 * IDX_DIM ** -0.5
    widxT = widxT_ref[0] * w_scale

    def score_chunk(c, carry):
        off = pl.multiple_of(c * KC, KC)
        for sub in range(KC // DSA_SCORE_ROWS):
            r0 = off + sub * DSA_SCORE_ROWS
            kc = kidx_ref[pl.ds(r0, DSA_SCORE_ROWS), :]
            acc = jnp.zeros((DSA_SCORE_ROWS, QB), F32)
            for g in range(0, IDX_HEADS, DSA_IDX_GROUP):
                logits = [_dot(kc, qidxT_ref[0, h * IDX_DIM:(h + 1) * IDX_DIM, :])
                          for h in range(g, g + DSA_IDX_GROUP)]
                for h, logit in zip(range(g, g + DSA_IDX_GROUP), logits):
                    acc = acc + jnp.maximum(logit, 0.0) * widxT[h:h + 1, :]
            acc = jnp.where(acc == 0.0, 0.0, acc)
            kpos = r0 + lax.broadcasted_iota(I32, (DSA_SCORE_ROWS, 1), 0)
            score = jnp.where(kpos <= qpos, acc, -jnp.inf)
            key_ref[pl.ds(r0, DSA_SCORE_ROWS), :] = _sortable(score)
        return carry

    lax.fori_loop(0, n_chunks, score_chunk, 0)

    def count(pred):
        def body(c, cnt):
            off = pl.multiple_of(c * KC, KC)
            for sub in range(n_sub):
                r0 = off + sub * SUB
                cnt = cnt + jnp.where(pred(key_ref[pl.ds(r0, SUB), :], r0 + krow), 1.0, 0.0)
            return cnt
        cnt = lax.fori_loop(0, n_chunks, body, jnp.zeros((SUB, QB), F32))
        return jnp.sum(cnt, axis=0, keepdims=True)

    def bit_step(t, ans):
        cand = ans | jnp.left_shift(jnp.int32(1), 31 - t)
        cand_s = cand ^ INT_MIN
        cnt = count(lambda key, _: key >= cand_s)
        return jnp.where(cnt >= n_sel, cand, ans)

    thr = lax.fori_loop(0, 32, bit_step, jnp.zeros((1, QB), I32)) ^ INT_MIN

    n_gt = count(lambda key, _: key > thr)
    n_ge = count(lambda key, _: key >= thr)
    tie_take = n_sel - n_gt
    needs_cut = (n_ge > n_sel) & (thr > KEY_NEG_INF)
    cut_ref[...] = jnp.full((1, QB), S, I32)

    @pl.when(jnp.max(jnp.where(needs_cut, 1.0, 0.0)) > 0.0)
    def _():
        def cut_step(t, cpos):
            cand = cpos | jnp.left_shift(jnp.int32(1), (S - 1).bit_length() - 1 - t)
            before = count(lambda key, kp: (key == thr) & (kp < cand))
            return jnp.where(before < tie_take, cand, cpos)
        cpos = lax.fori_loop(0, (S - 1).bit_length(), cut_step, jnp.zeros((1, QB), I32))
        cut_ref[...] = jnp.where(needs_cut, cpos, S)

    cut = cut_ref[...]

    row_head = lax.broadcasted_iota(I32, (DSA_W, 1), 0) // HEAD_DIM
    dsaqT = dsaqT_ref[0] * (HEAD_DIM ** -0.5)
    for h in range(DSA_HEADS):
        qh_ref[h] = jnp.where(row_head == h, dsaqT, 0.0).astype(BF16)
    acc_ref[...] = jnp.zeros_like(acc_ref)

    def attend_chunk(c, carry):
        ms, ls = carry
        off = pl.multiple_of(c * KC, KC)
        kch = k_ref[pl.ds(off, KC), :]
        for h in range(DSA_HEADS):
            s_ref[h] = _dot(kch, qh_ref[h])
        for sub in range(n_sub):
            r0 = off + sub * SUB
            key = key_ref[pl.ds(r0, SUB), :]
            sel = ((key > thr) | ((key == thr) & (r0 + krow <= cut))) & (key > KEY_NEG_INF)
            bias_ref[sub * SUB:(sub + 1) * SUB, :] = jnp.where(sel, 0.0, MASKED_SCORE)
        new_m, new_l, alphas = [], [], []
        for h in range(DSA_HEADS):
            mx = jnp.full((SUB, QB), MASKED_SCORE, F32)
            for sub in range(n_sub):
                rows = slice(sub * SUB, (sub + 1) * SUB)
                t = s_ref[h, rows, :] + bias_ref[rows, :]
                s_ref[h, rows, :] = t
                mx = jnp.maximum(mx, t)
            m_new = jnp.maximum(ms[h], jnp.max(mx, axis=0, keepdims=True))
            alpha = jnp.exp(ms[h] - m_new)
            psum = jnp.zeros((SUB, QB), F32)
            for sub in range(n_sub):
                rows = slice(sub * SUB, (sub + 1) * SUB)
                p = jnp.exp(s_ref[h, rows, :] - m_new)
                psum = psum + p
                p_ref[h, rows, :] = p.astype(BF16)
            new_m.append(m_new)
            new_l.append(alpha * ls[h] + jnp.sum(psum, axis=0, keepdims=True))
            alphas.append(alpha)
        vch = vT_ref[c]
        pvs = [_dot(vch[h * HEAD_DIM:(h + 1) * HEAD_DIM, :], p_ref[h]) for h in range(DSA_HEADS)]
        for h in range(DSA_HEADS):
            acc_ref[h] = alphas[h] * acc_ref[h] + pvs[h]
        return tuple(new_m), tuple(new_l)

    init = (tuple(jnp.full((1, QB), MASKED_SCORE, F32) for _ in range(DSA_HEADS)),
            tuple(jnp.zeros((1, QB), F32) for _ in range(DSA_HEADS)))
    _, ls = lax.fori_loop(0, n_chunks, attend_chunk, init)
    oT = jnp.concatenate([acc_ref[h] / ls[h] for h in range(DSA_HEADS)], axis=0)
    o_ref[...] = oT.T.astype(o_ref.dtype)


def _dsa(qidxT, dsaqT, widxT, kidx, k, vT, B, S):
    QB, KC = DSA_QB, DSA_KC
    assert QB == KC
    nq = S // QB
    nkc = S // KC
    n_sel = min(TOPK_MAX, S // 4)
    T = B * S
    kern = functools.partial(_dsa_kernel, n_sel=n_sel, S=S)
    return pl.pallas_call(
        kern,
        grid=(B, nq),
        in_specs=[
            pl.BlockSpec((1, IDX_HEADS * IDX_DIM, QB), lambda b, i: (b, 0, i)),
            pl.BlockSpec((1, DSA_W, QB), lambda b, i: (b, 0, i)),
            pl.BlockSpec((1, widxT.shape[1], QB), lambda b, i: (b, 0, i)),
            pl.BlockSpec((S, IDX_DIM), lambda b, i: (b, 0)),
            pl.BlockSpec((S, DSA_W), lambda b, i: (b, 0)),
            pl.BlockSpec((nkc, DSA_W, KC), lambda b, i: (b, 0, 0)),
        ],
        out_specs=pl.BlockSpec((QB, DSA_W), lambda b, i: (b * nq + i, 0)),
        out_shape=jax.ShapeDtypeStruct((T, DSA_W), BF16),
        scratch_shapes=[
            pltpu.VMEM((S, QB), I32),
            pltpu.VMEM((1, QB), I32),
            pltpu.VMEM((DSA_HEADS, DSA_W, QB), BF16),
            pltpu.VMEM((DSA_HEADS, KC, QB), F32),
            pltpu.VMEM((KC, QB), F32),
            pltpu.VMEM((DSA_HEADS, KC, QB), BF16),
            pltpu.VMEM((DSA_HEADS, HEAD_DIM, QB), F32),
        ],
        compiler_params=_params("parallel", "arbitrary"),
        name="dsa",
    )(qidxT, dsaqT, widxT, kidx, k, vT)


def _dilated_kernel(q_ref, kp_ref, kc_ref, vp_ref, vc_ref, o_ref, lse_ref, *, steps):
    L = DIL_BLOCK
    j = pl.program_id(2)
    qi = lax.broadcasted_iota(I32, (L, 2 * L), 0)
    kj = lax.broadcasted_iota(I32, (L, 2 * L), 1)
    dist = L + qi - kj
    mask = (dist >= 0) & (dist <= steps) & (j * L - L + kj >= 0)
    lane = lax.broadcasted_iota(I32, (1, 2 * HEAD_DIM), 1)
    low = lane < HEAD_DIM
    for pair in range(DIL_HEADS // 2):
        cs = slice(pair * 2 * HEAD_DIM, (pair + 1) * 2 * HEAD_DIM)
        q = q_ref[0, :, cs]
        kk = jnp.concatenate([kp_ref[0, :, cs], kc_ref[0, :, cs]], axis=0)
        vv = jnp.concatenate([vp_ref[0, :, cs], vc_ref[0, :, cs]], axis=0)
        outs, lses = [], []
        for sub in range(2):
            qh = jnp.where(low if sub == 0 else ~low, q, jnp.zeros_like(q))
            s = _dot_nt(qh, kk) * (HEAD_DIM ** -0.5)
            s = jnp.where(mask, s, -jnp.inf)
            m = jnp.max(s, axis=-1, keepdims=True)
            p = jnp.exp(s - m)
            den = jnp.sum(p, axis=-1, keepdims=True)
            outs.append(_dot(p.astype(BF16), vv) / den)
            lses.append(m + jnp.log(den))
        o_ref[0, :, cs] = jnp.where(low, outs[0], outs[1])
        lse_ref[0, :, cs] = jnp.where(low, lses[0], lses[1])


def _dilated(q, k, v, B, S, window, dilation):
    steps = window // dilation
    n = S // dilation
    L = DIL_BLOCK
    assert steps == L and n % L == 0, "dilated branch supports window/dilation == 128 and S % (128*dilation) == 0"
    W = DIL_W
    view = lambda t: t.reshape(B, n, dilation * W)
    own = lambda b, r, j: (b, j, r)
    prev = lambda b, r, j: (b, jnp.maximum(j - 1, 0), r)
    blk = (1, L, W)
    o, lse = pl.pallas_call(
        functools.partial(_dilated_kernel, steps=steps),
        grid=(B, dilation, n // L),
        in_specs=[pl.BlockSpec(blk, own), pl.BlockSpec(blk, prev), pl.BlockSpec(blk, own),
                  pl.BlockSpec(blk, prev), pl.BlockSpec(blk, own)],
        out_specs=(pl.BlockSpec(blk, own), pl.BlockSpec(blk, own)),
        out_shape=(jax.ShapeDtypeStruct((B, n, dilation * W), F32),
                   jax.ShapeDtypeStruct((B, n, dilation * W), F32)),
        compiler_params=_params("parallel", "parallel", "parallel"),
        name=f"dilated_d{dilation}",
    )(view(q), view(k), view(k), view(v), view(v))
    return o.reshape(B * S, W), lse.reshape(B * S, W)


def _dil_merge_kernel(o1, o2, o3, l1, l2, l3, out_ref):
    a, b, c = l1[...], l2[...], l3[...]
    m = jnp.maximum(jnp.maximum(a, b), c)
    ea, eb, ec = jnp.exp(a - m), jnp.exp(b - m), jnp.exp(c - m)
    tot = ea + eb + ec
    out_ref[...] = ((ea * o1[...] + eb * o2[...] + ec * o3[...]) / tot).astype(out_ref.dtype)


def _dil_merge(os, lses):
    T, W = os[0].shape
    tm = ROW_TILE
    spec = pl.BlockSpec((tm, W), lambda i: (i, 0))
    return pl.pallas_call(
        _dil_merge_kernel,
        grid=(T // tm,),
        in_specs=[spec] * 6,
        out_specs=spec,
        out_shape=jax.ShapeDtypeStruct((T, W), BF16),
        compiler_params=_params("parallel"),
        name="dil_merge",
    )(*os, *lses)


def _mlp_kernel(x_ref, ret_ref, dsa_ref, dil_ref, wout_ref, nw_ref, wup_ref, wdown_ref, o_ref):
    mixed = (_dot(ret_ref[...], wout_ref[0:RET_W, :])
             + _dot(dsa_ref[...], wout_ref[RET_W:RET_W + DSA_W, :])
             + _dot(dil_ref[...], wout_ref[RET_W + DSA_W:, :]))
    x = x_ref[...] + mixed
    h = _rms(x, nw_ref[...]).astype(BF16)
    ff = None
    for c in range(wup_ref.shape[1] // FF_CHUNK):
        cs = slice(c * FF_CHUNK, (c + 1) * FF_CHUNK)
        u = jnp.maximum(_dot(h, wup_ref[:, cs]), 0.0)
        part = _dot((u * u).astype(BF16), wdown_ref[cs, :])
        ff = part if ff is None else ff + part
    o_ref[...] = x + ff


def _mlp(x2, ret_o, dsa_o, dil_o, w_out, nw, w_up, w_down):
    T, D = x2.shape
    tm = ROW_TILE
    row = lambda i: (i, 0)
    const = lambda i: (0, 0)
    return pl.pallas_call(
        _mlp_kernel,
        grid=(T // tm,),
        in_specs=[
            pl.BlockSpec((tm, D), row),
            pl.BlockSpec((tm, RET_W), row),
            pl.BlockSpec((tm, DSA_W), row),
            pl.BlockSpec((tm, DIL_W), row),
            _resident(w_out.shape, const),
            _resident((1, D), const),
            _resident(w_up.shape, const),
            _resident(w_down.shape, const),
        ],
        out_specs=pl.BlockSpec((tm, D), row),
        out_shape=jax.ShapeDtypeStruct((T, D), F32),
        compiler_params=_params("parallel"),
        name="outproj_mlp",
    )(x2, ret_o, dsa_o, dil_o, w_out, nw, w_up, w_down)


def _final_norm_kernel(x_ref, w_ref, o_ref):
    o_ref[...] = _rms(x_ref[...], w_ref[...])


def _final_norm(x2, w):
    T, D = x2.shape
    tm = ROW_TILE
    return pl.pallas_call(
        _final_norm_kernel,
        grid=(T // tm,),
        in_specs=[pl.BlockSpec((tm, D), lambda i: (i, 0)), _resident((1, D), lambda i: (0, 0))],
        out_specs=pl.BlockSpec((tm, D), lambda i: (i, 0)),
        out_shape=jax.ShapeDtypeStruct((T, D), F32),
        compiler_params=_params("parallel"),
        name="final_norm",
    )(x2, w)


def _split_w_in(w_in):
    ret_cols = 4 * RET_W
    o = ret_cols
    dsa_q = w_in[:, o:o + DSA_W]; o += DSA_W
    c_kv = w_in[:, o:o + KV_LATENT]; o += KV_LATENT
    q_idx = w_in[:, o:o + IDX_HEADS * IDX_DIM]; o += IDX_HEADS * IDX_DIM
    k_idx = w_in[:, o:o + IDX_DIM]; o += IDX_DIM
    w_idx = w_in[:, o:o + IDX_HEADS]; o += IDX_HEADS
    dil = w_in[:, o:]
    w_main = jnp.concatenate([w_in[:, :ret_cols], dil, c_kv, k_idx], axis=1).astype(BF16)
    pad = jnp.zeros((w_in.shape[0], WIDX_ROWS - IDX_HEADS), w_in.dtype)
    w_t = jnp.concatenate([q_idx, dsa_q, w_idx, pad], axis=1).T.astype(BF16)
    return w_main, w_t


def kernel(x, attn_norm_w, w_in, ret_norm_w, dsa_kv_norm_w, dsa_w_uk, dsa_w_uv, w_out, mlp_norm_w, w_up, w_down, final_norm_w):
    B, S, D = x.shape
    depth = w_in.shape[0]
    assert S % ROW_TILE == 0 and S % DSA_KC == 0 and S % RET_CHUNK == 0
    tables = _retention_tables(S)
    x2 = x.reshape(B * S, D)
    for layer in range(depth):
        w_main, w_t = _split_w_in(w_in[layer])
        ret_cols, dq, dk, dv, ckv, kidx, qidxT, dsaqT, widxT = _inproj(
            x2, attn_norm_w[layer][None, :], w_main, w_t, B, S)
        ret_o = _retention(ret_cols, tables, ret_norm_w[layer].reshape(1, RET_W), B, S)
        k, vT = _dsa_kv(ckv, dsa_kv_norm_w[layer][None, :], dsa_w_uk[layer].astype(BF16),
                        dsa_w_uv[layer].T.astype(BF16))
        dsa_o = _dsa(qidxT, dsaqT, widxT, kidx, k, vT, B, S)
        branches = [_dilated(dq, dk, dv, B, S, window, dilation) for window, dilation in DIL_PATTERNS]
        dil_o = _dil_merge([o for o, _ in branches], [l for _, l in branches])
        x2 = _mlp(x2, ret_o, dsa_o, dil_o, w_out[layer].astype(BF16), mlp_norm_w[layer][None, :],
                  w_up[layer].astype(BF16), w_down[layer].astype(BF16))
    return _final_norm(x2, final_norm_w[None, :]).reshape(B, S, D)
```

```python
import functools
import math

import numpy as np
import jax
import jax.numpy as jnp
from jax import lax
from jax.experimental import pallas as pl
from jax.experimental.pallas import tpu as pltpu

F32 = jnp.float32
BF16 = jnp.bfloat16
I32 = jnp.int32

HEAD_DIM = 64
RET_HEADS = 4
DSA_HEADS = 4
DIL_HEADS = 8
RET_W = RET_HEADS * HEAD_DIM
DSA_W = DSA_HEADS * HEAD_DIM
DIL_W = DIL_HEADS * HEAD_DIM
RET_CHUNK = 128
RET_ROPE_BASE = 10000.0
KV_LATENT = 128
IDX_HEADS = 8
IDX_DIM = 64
TOPK_MAX = 256
DIL_PATTERNS = ((128, 1), (512, 4), (2048, 16))
DIL_BLOCK = 128
NORM_EPS = 1e-6

V7X_VMEM_LIMIT_BYTES = 56 * 1024 * 1024

ROW_TILE = 512
DSA_QB = 256
DSA_KC = 256
DSA_SUB = 64
DSA_SCORE_ROWS = 128
DSA_IDX_GROUP = 4
FF_CHUNK = 1024

WIDX_ROWS = 16

INT_MIN = -(2 ** 31)
KEY_NEG_INF = int(np.int32(np.uint32(0xFF800000) ^ np.uint32(0x7FFFFFFF)))
MASKED_SCORE = -1e30


def _params(*semantics):
    return pltpu.CompilerParams(dimension_semantics=semantics, vmem_limit_bytes=V7X_VMEM_LIMIT_BYTES)


def _resident(shape, index_map):
    return pl.BlockSpec(shape, index_map, pipeline_mode=pl.Buffered(1))


def _dot(a, b):
    return jnp.dot(a, b, preferred_element_type=F32)


def _dot_nt(a, b):
    return lax.dot_general(a, b, (((1,), (1,)), ((), ())), preferred_element_type=F32)


def _rms(x, w):
    return x * lax.rsqrt(jnp.mean(x * x, axis=-1, keepdims=True) + NORM_EPS) * w


def _inproj_kernel(x_ref, nw_ref, w_ref, wt_ref, ret_ref, dq_ref, dk_ref, dv_ref, ckv_ref, kidx_ref,
                   qidxT_ref, dsaqT_ref, widxT_ref):
    h = _rms(x_ref[...], nw_ref[...]).astype(BF16)
    o = 0
    for ref in (ret_ref, dq_ref, dk_ref, dv_ref, ckv_ref, kidx_ref):
        n = ref.shape[-1]
        ref[...] = _dot(h, w_ref[:, o:o + n]).astype(ref.dtype)
        o += n
    o = 0
    for ref in (qidxT_ref, dsaqT_ref, widxT_ref):
        n = ref.shape[-2]
        ref[0] = _dot_nt(wt_ref[o:o + n, :], h).astype(ref.dtype)
        o += n


def _inproj(x2, nw, w_main, w_t, B, S):
    T, D = x2.shape
    tm = ROW_TILE
    nt = S // tm
    widx_rows = w_t.shape[0] - IDX_HEADS * IDX_DIM - DSA_W
    row = lambda i: (i, 0)
    tr = lambda i: (i // nt, 0, i % nt)
    out_shape = (
        jax.ShapeDtypeStruct((T, 4 * RET_W), F32),
        jax.ShapeDtypeStruct((T, DIL_W), BF16),
        jax.ShapeDtypeStruct((T, DIL_W), BF16),
        jax.ShapeDtypeStruct((T, DIL_W), BF16),
        jax.ShapeDtypeStruct((T, KV_LATENT), F32),
        jax.ShapeDtypeStruct((T, IDX_DIM), BF16),
        jax.ShapeDtypeStruct((B, IDX_HEADS * IDX_DIM, S), BF16),
        jax.ShapeDtypeStruct((B, DSA_W, S), BF16),
        jax.ShapeDtypeStruct((B, widx_rows, S), F32),
    )
    out_specs = (
        pl.BlockSpec((tm, 4 * RET_W), row),
        pl.BlockSpec((tm, DIL_W), row),
        pl.BlockSpec((tm, DIL_W), row),
        pl.BlockSpec((tm, DIL_W), row),
        pl.BlockSpec((tm, KV_LATENT), row),
        pl.BlockSpec((tm, IDX_DIM), row),
        pl.BlockSpec((1, IDX_HEADS * IDX_DIM, tm), tr),
        pl.BlockSpec((1, DSA_W, tm), tr),
        pl.BlockSpec((1, widx_rows, tm), tr),
    )
    return pl.pallas_call(
        _inproj_kernel,
        grid=(T // tm,),
        in_specs=[
            pl.BlockSpec((tm, D), row),
            _resident((1, D), lambda i: (0, 0)),
            _resident(w_main.shape, lambda i: (0, 0)),
            _resident(w_t.shape, lambda i: (0, 0)),
        ],
        out_specs=out_specs,
        out_shape=out_shape,
        compiler_params=_params("parallel"),
        name="inproj",
    )(x2, nw, w_main, w_t)


def _retention_tables(S):
    half = HEAD_DIM // 2
    inv = RET_ROPE_BASE ** (-jnp.arange(half, dtype=F32) / half)
    ang = jnp.arange(S, dtype=F32)[:, None] * inv[None, :]
    cos, sin = jnp.cos(ang), jnp.sin(ang)
    cos_h = jnp.concatenate([cos, cos], axis=-1)
    sin_h = jnp.concatenate([-sin, sin], axis=-1)
    cosf = jnp.tile(cos_h, (1, RET_HEADS))
    sinf = jnp.tile(sin_h, (1, RET_HEADS))
    C = RET_CHUNK
    log_g = np.log(1.0 - 2.0 ** (-5.0 - np.arange(RET_HEADS, dtype=np.float64)))
    i = np.arange(C, dtype=np.float64)
    diff = i[:, None] - i[None, :]
    decay = np.exp(np.maximum(diff, 0.0)[None] * log_g[:, None, None]) * (diff >= 0)[None]
    zeta = np.exp((C - 1.0 - i)[None, :] * log_g[:, None])
    xi = np.exp((i + 1.0)[None, :] * log_g[:, None])
    g_chunk = np.exp(C * log_g)
    rep = lambda t: np.repeat(t.T, HEAD_DIM, axis=1)
    g_rows = np.repeat(g_chunk, HEAD_DIM)[:, None] * np.ones((1, RET_W))
    return (cosf, sinf, jnp.asarray(decay, F32), jnp.asarray(rep(zeta), F32), jnp.asarray(rep(xi), F32),
            jnp.asarray(g_rows, F32))


def _retention_kernel(cols_ref, cos_ref, sin_ref, decay_ref, zeta_ref, xi_ref, grow_ref, nw_ref, o_ref, state_ref):
    W = RET_W

    @pl.when(pl.program_id(1) == 0)
    def _():
        state_ref[...] = jnp.zeros_like(state_ref)

    lane = lax.broadcasted_iota(I32, (1, W), 1)
    first_half = (lane % HEAD_DIM) < (HEAD_DIM // 2)
    cosf, sinf = cos_ref[...], sin_ref[...]

    def rot(t):
        partner = jnp.where(first_half, pltpu.roll(t, W - HEAD_DIM // 2, 1), pltpu.roll(t, HEAD_DIM // 2, 1))
        return t * cosf + partner * sinf

    q = rot(cols_ref[:, 0:W])
    k = rot(cols_ref[:, W:2 * W]) * (HEAD_DIM ** -0.5)
    v = cols_ref[:, 2 * W:3 * W].astype(BF16)
    g = cols_ref[:, 3 * W:4 * W]

    head_of_lane = lane // HEAD_DIM
    kb = k.astype(BF16)
    inner = jnp.zeros((RET_CHUNK, W), F32)
    for h in range(RET_HEADS):
        sel = head_of_lane == h
        qh = jnp.where(sel, q, 0.0).astype(BF16)
        a = (_dot_nt(qh, kb) * decay_ref[h]).astype(BF16)
        inner = inner + jnp.where(sel, _dot(a, v), 0.0)

    state = state_ref[...]
    cross = _dot(q.astype(BF16), state.astype(BF16)) * xi_ref[...]
    o = inner + cross

    kzT = (k * zeta_ref[...]).T.astype(BF16)
    kv = _dot(kzT, v)
    r_head = lax.broadcasted_iota(I32, (W, W), 0) // HEAD_DIM
    c_head = lax.broadcasted_iota(I32, (W, W), 1) // HEAD_DIM
    state_ref[...] = grow_ref[...] * state + jnp.where(r_head == c_head, kv, 0.0)

    mu = jnp.zeros_like(o)
    for h in range(RET_HEADS):
        sel = head_of_lane == h
        mu = mu + jnp.where(sel, jnp.sum(jnp.where(sel, o, 0.0), axis=-1, keepdims=True), 0.0)
    mu = mu * (1.0 / HEAD_DIM)
    d = o - mu
    var = jnp.zeros_like(o)
    for h in range(RET_HEADS):
        sel = head_of_lane == h
        var = var + jnp.where(sel, jnp.sum(jnp.where(sel, d * d, 0.0), axis=-1, keepdims=True), 0.0)
    var = var * (1.0 / HEAD_DIM)
    y = d * lax.rsqrt(var + NORM_EPS) * nw_ref[...]
    o_ref[...] = (jax.nn.silu(g) * y).astype(o_ref.dtype)


def _retention(ret_cols, tables, norm_w, B, S):
    T = ret_cols.shape[0]
    C, W = RET_CHUNK, RET_W
    n = S // C
    cosf, sinf, decay, zeta, xi, g_rows = tables
    tok = lambda b, j: (b * n + j, 0)
    pos = lambda b, j: (j, 0)
    const2 = lambda b, j: (0, 0)
    return pl.pallas_call(
        _retention_kernel,
        grid=(B, n),
        in_specs=[
            pl.BlockSpec((C, 4 * W), tok),
            pl.BlockSpec((C, W), pos),
            pl.BlockSpec((C, W), pos),
            _resident(decay.shape, lambda b, j: (0, 0, 0)),
            _resident((C, W), const2),
            _resident((C, W), const2),
            _resident((W, W), const2),
            _resident((1, W), const2),
        ],
        out_specs=pl.BlockSpec((C, W), tok),
        out_shape=jax.ShapeDtypeStruct((T, W), BF16),
        scratch_shapes=[pltpu.VMEM((W, W), F32)],
        compiler_params=_params("parallel", "arbitrary"),
        name="retention",
    )(ret_cols, cosf, sinf, decay, zeta, xi, g_rows, norm_w)


def _dsa_kv_kernel(ckv_ref, nw_ref, wuk_ref, wuvT_ref, k_ref, vT_ref):
    c = _rms(ckv_ref[...], nw_ref[...]).astype(BF16)
    k_ref[...] = _dot(c, wuk_ref[...]).astype(k_ref.dtype)
    vT_ref[0] = _dot_nt(wuvT_ref[...], c).astype(vT_ref.dtype)


def _dsa_kv(ckv, nw, w_uk, w_uvT):
    T = ckv.shape[0]
    tm = DSA_KC
    return pl.pallas_call(
        _dsa_kv_kernel,
        grid=(T // tm,),
        in_specs=[
            pl.BlockSpec((tm, KV_LATENT), lambda i: (i, 0)),
            _resident((1, KV_LATENT), lambda i: (0, 0)),
            _resident(w_uk.shape, lambda i: (0, 0)),
            _resident(w_uvT.shape, lambda i: (0, 0)),
        ],
        out_specs=(pl.BlockSpec((tm, DSA_W), lambda i: (i, 0)),
                   pl.BlockSpec((1, DSA_W, tm), lambda i: (i, 0, 0))),
        out_shape=(jax.ShapeDtypeStruct((T, DSA_W), BF16),
                   jax.ShapeDtypeStruct((T // tm, DSA_W, tm), BF16)),
        compiler_params=_params("parallel"),
        name="dsa_kv",
    )(ckv, nw, w_uk, w_uvT)


def _sortable(x):
    b = pltpu.bitcast(x, I32)
    return b ^ ((b >> 31) & 0x7FFFFFFF)


def _dsa_kernel(qidxT_ref, dsaqT_ref, widxT_ref, kidx_ref, k_ref, vT_ref, o_ref,
                key_ref, cut_ref, qh_ref, s_ref, bias_ref, p_ref, acc_ref, *, n_sel, S):
    QB, KC, SUB = DSA_QB, DSA_KC, DSA_SUB
    n_sub = KC // SUB
    i = pl.program_id(1)
    n_chunks = i + 1
    qpos = i * QB + lax.broadcasted_iota(I32, (1, QB), 1)
    krow = lax.broadcasted_iota(I32, (SUB, 1), 0)
    w_scale = IDX_HEADS ** -0.5 * IDX_DIM ** -0.5
    widxT = widxT_ref[0] * w_scale

    def score_chunk(c, carry):
        off = pl.multiple_of(c * (2 * KC), 2 * KC)
        for sub in range(2 * KC // DSA_SCORE_ROWS):
            r0 = off + sub * DSA_SCORE_ROWS
            kc = kidx_ref[pl.ds(r0, DSA_SCORE_ROWS), :]
            acc = jnp.zeros((DSA_SCORE_ROWS, QB), F32)
            for g in range(0, IDX_HEADS, DSA_IDX_GROUP):
                logits = [_dot(kc, qidxT_ref[0, h * IDX_DIM:(h + 1) * IDX_DIM, :])
                          for h in range(g, g + DSA_IDX_GROUP)]
                for h, logit in zip(range(g, g + DSA_IDX_GROUP), logits):
                    acc = acc + jnp.maximum(logit, 0.0) * widxT[h:h + 1, :]
            acc = jnp.where(acc == 0.0, 0.0, acc)
            kpos = r0 + lax.broadcasted_iota(I32, (DSA_SCORE_ROWS, 1), 0)
            score = jnp.where(kpos <= qpos, acc, -jnp.inf)
            key_ref[pl.ds(r0, DSA_SCORE_ROWS), :] = _sortable(score)
        return carry

    lax.fori_loop(0, (n_chunks + 1) // 2, score_chunk, 0)

    def count(pred):
        def body(c, cnt):
            off = pl.multiple_of(c * KC, KC)
            for sub in range(n_sub):
                r0 = off + sub * SUB
                cnt = cnt + jnp.where(pred(key_ref[pl.ds(r0, SUB), :], r0 + krow), 1.0, 0.0)
            return cnt
        cnt = lax.fori_loop(0, n_chunks, body, jnp.zeros((SUB, QB), F32))
        return jnp.sum(cnt, axis=0, keepdims=True)

    def bit_step(t, carry):
        ans, n_ge = carry
        cand = ans | jnp.left_shift(jnp.int32(1), 31 - t)
        cand_s = cand ^ INT_MIN
        cnt = count(lambda key, _: key >= cand_s)
        keep = cnt >= n_sel
        return jnp.where(keep, cand, ans), jnp.where(keep, cnt, n_ge)

    stored = (n_chunks * KC).astype(F32)
    ans, n_ge = lax.fori_loop(0, 32, bit_step, (jnp.zeros((1, QB), I32), jnp.full((1, QB), stored, F32)))
    thr = ans ^ INT_MIN

    needs_cut = (n_ge > n_sel) & (thr > KEY_NEG_INF)
    cut_ref[...] = jnp.full((1, QB), S, I32)

    @pl.when(jnp.max(jnp.where(needs_cut, 1.0, 0.0)) > 0.0)
    def _():
        tie_take = n_sel - count(lambda key, _: key > thr)

        def cut_step(t, cpos):
            cand = cpos | jnp.left_shift(jnp.int32(1), (S - 1).bit_length() - 1 - t)
            before = count(lambda key, kp: (key == thr) & (kp < cand))
            return jnp.where(before < tie_take, cand, cpos)
        cpos = lax.fori_loop(0, (S - 1).bit_length(), cut_step, jnp.zeros((1, QB), I32))
        cut_ref[...] = jnp.where(needs_cut, cpos, S)

    cut = cut_ref[...]

    row_head = lax.broadcasted_iota(I32, (DSA_W, 1), 0) // HEAD_DIM
    dsaqT = dsaqT_ref[0] * (HEAD_DIM ** -0.5)
    for h in range(DSA_HEADS):
        qh_ref[h] = jnp.where(row_head == h, dsaqT, 0.0).astype(BF16)
    acc_ref[...] = jnp.zeros_like(acc_ref)
    p_ref[...] = jnp.zeros_like(p_ref)

    def pv_update(c_prev, alphas):
        vch = vT_ref[c_prev]
        pvs = [_dot(vch[h * HEAD_DIM:(h + 1) * HEAD_DIM, :], p_ref[h]) for h in range(DSA_HEADS)]
        for h in range(DSA_HEADS):
            acc_ref[h] = alphas[h] * acc_ref[h] + pvs[h]

    def attend_chunk(c, carry):
        ms, ls, prev_alphas = carry
        off = pl.multiple_of(c * KC, KC)
        for sub in range(n_sub):
            r0 = off + sub * SUB
            key = key_ref[pl.ds(r0, SUB), :]
            sel = ((key > thr) | ((key == thr) & (r0 + krow <= cut))) & (key > KEY_NEG_INF)
            bias_ref[sub * SUB:(sub + 1) * SUB, :] = jnp.where(sel, 0.0, MASKED_SCORE)
        pv_update(jnp.maximum(c - 1, 0), prev_alphas)
        kch = k_ref[pl.ds(off, KC), :]
        for h in range(DSA_HEADS):
            s_ref[h] = _dot(kch, qh_ref[h])
        new_m, new_l, alphas = [], [], []
        for h in range(DSA_HEADS):
            mx = jnp.full((SUB, QB), MASKED_SCORE, F32)
            for sub in range(n_sub):
                rows = slice(sub * SUB, (sub + 1) * SUB)
                t = s_ref[h, rows, :] + bias_ref[rows, :]
                s_ref[h, rows, :] = t
                mx = jnp.maximum(mx, t)
            m_new = jnp.maximum(ms[h], jnp.max(mx, axis=0, keepdims=True))
            alpha = jnp.exp(ms[h] - m_new)
            psum = jnp.zeros((SUB, QB), F32)
            for sub in range(n_sub):
                rows = slice(sub * SUB, (sub + 1) * SUB)
                p = jnp.exp(s_ref[h, rows, :] - m_new)
                psum = psum + p
                p_ref[h, rows, :] = p.astype(BF16)
            new_m.append(m_new)
            new_l.append(alpha * ls[h] + jnp.sum(psum, axis=0, keepdims=True))
            alphas.append(alpha)
        return tuple(new_m), tuple(new_l), tuple(alphas)

    init = (tuple(jnp.full((1, QB), MASKED_SCORE, F32) for _ in range(DSA_HEADS)),
            tuple(jnp.zeros((1, QB), F32) for _ in range(DSA_HEADS)),
            tuple(jnp.ones((1, QB), F32) for _ in range(DSA_HEADS)))
    _, ls, last_alphas = lax.fori_loop(0, n_chunks, attend_chunk, init)
    pv_update(n_chunks - 1, last_alphas)
    oT = jnp.concatenate([acc_ref[h] / ls[h] for h in range(DSA_HEADS)], axis=0)
    o_ref[...] = oT.T.astype(o_ref.dtype)


def _dsa(qidxT, dsaqT, widxT, kidx, k, vT, B, S):
    QB, KC = DSA_QB, DSA_KC
    assert QB == KC and S % (2 * KC) == 0
    nq = S // QB
    nkc = S // KC
    n_sel = min(TOPK_MAX, S // 4)
    T = B * S
    kern = functools.partial(_dsa_kernel, n_sel=n_sel, S=S)
    return pl.pallas_call(
        kern,
        grid=(B, nq),
        in_specs=[
            pl.BlockSpec((1, IDX_HEADS * IDX_DIM, QB), lambda b, i: (b, 0, i)),
            pl.BlockSpec((1, DSA_W, QB), lambda b, i: (b, 0, i)),
            pl.BlockSpec((1, widxT.shape[1], QB), lambda b, i: (b, 0, i)),
            pl.BlockSpec((S, IDX_DIM), lambda b, i: (b, 0)),
            pl.BlockSpec((S, DSA_W), lambda b, i: (b, 0)),
            pl.BlockSpec((nkc, DSA_W, KC), lambda b, i: (b, 0, 0)),
        ],
        out_specs=pl.BlockSpec((QB, DSA_W), lambda b, i: (b * nq + i, 0)),
        out_shape=jax.ShapeDtypeStruct((T, DSA_W), BF16),
        scratch_shapes=[
            pltpu.VMEM((S, QB), I32),
            pltpu.VMEM((1, QB), I32),
            pltpu.VMEM((DSA_HEADS, DSA_W, QB), BF16),
            pltpu.VMEM((DSA_HEADS, KC, QB), F32),
            pltpu.VMEM((KC, QB), F32),
            pltpu.VMEM((DSA_HEADS, KC, QB), BF16),
            pltpu.VMEM((DSA_HEADS, HEAD_DIM, QB), F32),
        ],
        compiler_params=_params("parallel", "arbitrary"),
        name="dsa",
    )(qidxT, dsaqT, widxT, kidx, k, vT)


def _dilated_kernel(q_ref, kp_ref, kc_ref, vp_ref, vc_ref, o_ref, lse_ref, *, steps):
    L = DIL_BLOCK
    j = pl.program_id(2)
    qi = lax.broadcasted_iota(I32, (L, 2 * L), 0)
    kj = lax.broadcasted_iota(I32, (L, 2 * L), 1)
    dist = L + qi - kj
    mask = (dist >= 0) & (dist <= steps) & (j * L - L + kj >= 0)
    lane = lax.broadcasted_iota(I32, (1, 2 * HEAD_DIM), 1)
    low = lane < HEAD_DIM
    for pair in range(DIL_HEADS // 2):
        cs = slice(pair * 2 * HEAD_DIM, (pair + 1) * 2 * HEAD_DIM)
        q = q_ref[0, :, cs]
        kk = jnp.concatenate([kp_ref[0, :, cs], kc_ref[0, :, cs]], axis=0)
        vv = jnp.concatenate([vp_ref[0, :, cs], vc_ref[0, :, cs]], axis=0)
        outs, lses = [], []
        for sub in range(2):
            qh = jnp.where(low if sub == 0 else ~low, q, jnp.zeros_like(q))
            s = _dot_nt(qh, kk) * (HEAD_DIM ** -0.5)
            s = jnp.where(mask, s, -jnp.inf)
            m = jnp.max(s, axis=-1, keepdims=True)
            p = jnp.exp(s - m)
            den = jnp.sum(p, axis=-1, keepdims=True)
            outs.append(_dot(p.astype(BF16), vv) / den)
            lses.append(m + jnp.log(den))
        o_ref[0, :, cs] = jnp.where(low, outs[0], outs[1])
        lse_ref[0, :, cs] = jnp.where(low, lses[0], lses[1])


def _dilated(q, k, v, B, S, window, dilation):
    steps = window // dilation
    n = S // dilation
    L = DIL_BLOCK
    assert steps == L and n % L == 0, "dilated branch supports window/dilation == 128 and S % (128*dilation) == 0"
    W = DIL_W
    view = lambda t: t.reshape(B, n, dilation * W)
    own = lambda b, r, j: (b, j, r)
    prev = lambda b, r, j: (b, jnp.maximum(j - 1, 0), r)
    blk = (1, L, W)
    o, lse = pl.pallas_call(
        functools.partial(_dilated_kernel, steps=steps),
        grid=(B, dilation, n // L),
        in_specs=[pl.BlockSpec(blk, own), pl.BlockSpec(blk, prev), pl.BlockSpec(blk, own),
                  pl.BlockSpec(blk, prev), pl.BlockSpec(blk, own)],
        out_specs=(pl.BlockSpec(blk, own), pl.BlockSpec(blk, own)),
        out_shape=(jax.ShapeDtypeStruct((B, n, dilation * W), F32),
                   jax.ShapeDtypeStruct((B, n, dilation * W), F32)),
        compiler_params=_params("parallel", "parallel", "parallel"),
        name=f"dilated_d{dilation}",
    )(view(q), view(k), view(k), view(v), view(v))
    return o.reshape(B * S, W), lse.reshape(B * S, W)


def _dil_merge_kernel(o1, o2, o3, l1, l2, l3, out_ref):
    a, b, c = l1[...], l2[...], l3[...]
    m = jnp.maximum(jnp.maximum(a, b), c)
    ea, eb, ec = jnp.exp(a - m), jnp.exp(b - m), jnp.exp(c - m)
    tot = ea + eb + ec
    out_ref[...] = ((ea * o1[...] + eb * o2[...] + ec * o3[...]) / tot).astype(out_ref.dtype)


def _dil_merge(os, lses):
    T, W = os[0].shape
    tm = ROW_TILE
    spec = pl.BlockSpec((tm, W), lambda i: (i, 0))
    return pl.pallas_call(
        _dil_merge_kernel,
        grid=(T // tm,),
        in_specs=[spec] * 6,
        out_specs=spec,
        out_shape=jax.ShapeDtypeStruct((T, W), BF16),
        compiler_params=_params("parallel"),
        name="dil_merge",
    )(*os, *lses)


def _mlp_kernel(x_ref, ret_ref, dsa_ref, dil_ref, wout_ref, nw_ref, wup_ref, wdown_ref, o_ref):
    mixed = (_dot(ret_ref[...], wout_ref[0:RET_W, :])
             + _dot(dsa_ref[...], wout_ref[RET_W:RET_W + DSA_W, :])
             + _dot(dil_ref[...], wout_ref[RET_W + DSA_W:, :]))
    x = x_ref[...] + mixed
    h = _rms(x, nw_ref[...]).astype(BF16)
    ff = None
    for c in range(wup_ref.shape[1] // FF_CHUNK):
        cs = slice(c * FF_CHUNK, (c + 1) * FF_CHUNK)
        u = jnp.maximum(_dot(h, wup_ref[:, cs]), 0.0)
        part = _dot((u * u).astype(BF16), wdown_ref[cs, :])
        ff = part if ff is None else ff + part
    o_ref[...] = x + ff


def _mlp(x2, ret_o, dsa_o, dil_o, w_out, nw, w_up, w_down):
    T, D = x2.shape
    tm = ROW_TILE
    row = lambda i: (i, 0)
    const = lambda i: (0, 0)
    return pl.pallas_call(
        _mlp_kernel,
        grid=(T // tm,),
        in_specs=[
            pl.BlockSpec((tm, D), row),
            pl.BlockSpec((tm, RET_W), row),
            pl.BlockSpec((tm, DSA_W), row),
            pl.BlockSpec((tm, DIL_W), row),
            _resident(w_out.shape, const),
            _resident((1, D), const),
            _resident(w_up.shape, const),
            _resident(w_down.shape, const),
        ],
        out_specs=pl.BlockSpec((tm, D), row),
        out_shape=jax.ShapeDtypeStruct((T, D), F32),
        compiler_params=_params("parallel"),
        name="outproj_mlp",
    )(x2, ret_o, dsa_o, dil_o, w_out, nw, w_up, w_down)


def _final_norm_kernel(x_ref, w_ref, o_ref):
    o_ref[...] = _rms(x_ref[...], w_ref[...])


def _final_norm(x2, w):
    T, D = x2.shape
    tm = ROW_TILE
    return pl.pallas_call(
        _final_norm_kernel,
        grid=(T // tm,),
        in_specs=[pl.BlockSpec((tm, D), lambda i: (i, 0)), _resident((1, D), lambda i: (0, 0))],
        out_specs=pl.BlockSpec((tm, D), lambda i: (i, 0)),
        out_shape=jax.ShapeDtypeStruct((T, D), F32),
        compiler_params=_params("parallel"),
        name="final_norm",
    )(x2, w)


def _split_w_in(w_in):
    ret_cols = 4 * RET_W
    o = ret_cols
    dsa_q = w_in[:, o:o + DSA_W]; o += DSA_W
    c_kv = w_in[:, o:o + KV_LATENT]; o += KV_LATENT
    q_idx = w_in[:, o:o + IDX_HEADS * IDX_DIM]; o += IDX_HEADS * IDX_DIM
    k_idx = w_in[:, o:o + IDX_DIM]; o += IDX_DIM
    w_idx = w_in[:, o:o + IDX_HEADS]; o += IDX_HEADS
    dil = w_in[:, o:]
    w_main = jnp.concatenate([w_in[:, :ret_cols], dil, c_kv, k_idx], axis=1).astype(BF16)
    pad = jnp.zeros((w_in.shape[0], WIDX_ROWS - IDX_HEADS), w_in.dtype)
    w_t = jnp.concatenate([q_idx, dsa_q, w_idx, pad], axis=1).T.astype(BF16)
    return w_main, w_t


def kernel(x, attn_norm_w, w_in, ret_norm_w, dsa_kv_norm_w, dsa_w_uk, dsa_w_uv, w_out, mlp_norm_w, w_up, w_down, final_norm_w):
    B, S, D = x.shape
    depth = w_in.shape[0]
    assert S % ROW_TILE == 0 and S % DSA_KC == 0 and S % RET_CHUNK == 0
    tables = _retention_tables(S)
    x2 = x.reshape(B * S, D)
    for layer in range(depth):
        w_main, w_t = _split_w_in(w_in[layer])
        ret_cols, dq, dk, dv, ckv, kidx, qidxT, dsaqT, widxT = _inproj(
            x2, attn_norm_w[layer][None, :], w_main, w_t, B, S)
        ret_o = _retention(ret_cols, tables, ret_norm_w[layer].reshape(1, RET_W), B, S)
        k, vT = _dsa_kv(ckv, dsa_kv_norm_w[layer][None, :], dsa_w_uk[layer].astype(BF16),
                        dsa_w_uv[layer].T.astype(BF16))
        dsa_o = _dsa(qidxT, dsaqT, widxT, kidx, k, vT, B, S)
        branches = [_dilated(dq, dk, dv, B, S, window, dilation) for window, dilation in DIL_PATTERNS]
        dil_o = _dil_merge([o for o, _ in branches], [l for _, l in branches])
        x2 = _mlp(x2, ret_o, dsa_o, dil_o, w_out[layer].astype(BF16), mlp_norm_w[layer][None, :],
                  w_up[layer].astype(BF16), w_down[layer].astype(BF16))
    return _final_norm(x2, final_norm_w[None, :]).reshape(B, S, D)
```

```python
import functools
import math

import numpy as np
import jax
import jax.numpy as jnp
from jax import lax
from jax.experimental import pallas as pl
from jax.experimental.pallas import tpu as pltpu

F32 = jnp.float32
BF16 = jnp.bfloat16
I32 = jnp.int32
I16 = jnp.int16

HEAD_DIM = 64
RET_HEADS = 4
DSA_HEADS = 4
DIL_HEADS = 8
RET_W = RET_HEADS * HEAD_DIM
DSA_W = DSA_HEADS * HEAD_DIM
DIL_W = DIL_HEADS * HEAD_DIM
RET_CHUNK = 128
RET_ROPE_BASE = 10000.0
KV_LATENT = 128
IDX_HEADS = 8
IDX_DIM = 64
TOPK_MAX = 256
DIL_PATTERNS = ((128, 1), (512, 4), (2048, 16))
DIL_BLOCK = 128
NORM_EPS = 1e-6

V7X_VMEM_LIMIT_BYTES = 56 * 1024 * 1024

ROW_TILE = 512
DSA_QB = 256
DSA_KC = 256
DSA_SUB = 64
DSA_SCORE_ROWS = 128
DSA_IDX_GROUP = 4
FF_CHUNK = 1024

WIDX_ROWS = 16

INT_MIN = -(2 ** 31)
KEY_NEG_INF = int(np.int32(np.uint32(0xFF800000) ^ np.uint32(0x7FFFFFFF)))
MASKED_SCORE = -1e30


def _params(*semantics):
    return pltpu.CompilerParams(dimension_semantics=semantics, vmem_limit_bytes=V7X_VMEM_LIMIT_BYTES)


def _resident(shape, index_map):
    return pl.BlockSpec(shape, index_map, pipeline_mode=pl.Buffered(1))


def _dot(a, b):
    return jnp.dot(a, b, preferred_element_type=F32)


def _dot_nt(a, b):
    return lax.dot_general(a, b, (((1,), (1,)), ((), ())), preferred_element_type=F32)


def _rms(x, w):
    return x * lax.rsqrt(jnp.mean(x * x, axis=-1, keepdims=True) + NORM_EPS) * w


def _inproj_kernel(x_ref, nw_ref, w_ref, wt_ref, ret_ref, dq_ref, dk_ref, dv_ref, ckv_ref, kidx_ref,
                   qidxT_ref, dsaqT_ref, widxT_ref):
    h = _rms(x_ref[...], nw_ref[...]).astype(BF16)
    o = 0
    for ref in (ret_ref, dq_ref, dk_ref, dv_ref, ckv_ref, kidx_ref):
        n = ref.shape[-1]
        ref[...] = _dot(h, w_ref[:, o:o + n]).astype(ref.dtype)
        o += n
    o = 0
    for ref in (qidxT_ref, dsaqT_ref, widxT_ref):
        n = ref.shape[-2]
        ref[0] = _dot_nt(wt_ref[o:o + n, :], h).astype(ref.dtype)
        o += n


def _inproj(x2, nw, w_main, w_t, B, S):
    T, D = x2.shape
    tm = ROW_TILE
    nt = S // tm
    widx_rows = w_t.shape[0] - IDX_HEADS * IDX_DIM - DSA_W
    row = lambda i: (i, 0)
    tr = lambda i: (i // nt, 0, i % nt)
    out_shape = (
        jax.ShapeDtypeStruct((T, 4 * RET_W), F32),
        jax.ShapeDtypeStruct((T, DIL_W), BF16),
        jax.ShapeDtypeStruct((T, DIL_W), BF16),
        jax.ShapeDtypeStruct((T, DIL_W), BF16),
        jax.ShapeDtypeStruct((T, KV_LATENT), F32),
        jax.ShapeDtypeStruct((T, IDX_DIM), BF16),
        jax.ShapeDtypeStruct((B, IDX_HEADS * IDX_DIM, S), BF16),
        jax.ShapeDtypeStruct((B, DSA_W, S), BF16),
        jax.ShapeDtypeStruct((B, widx_rows, S), F32),
    )
    out_specs = (
        pl.BlockSpec((tm, 4 * RET_W), row),
        pl.BlockSpec((tm, DIL_W), row),
        pl.BlockSpec((tm, DIL_W), row),
        pl.BlockSpec((tm, DIL_W), row),
        pl.BlockSpec((tm, KV_LATENT), row),
        pl.BlockSpec((tm, IDX_DIM), row),
        pl.BlockSpec((1, IDX_HEADS * IDX_DIM, tm), tr),
        pl.BlockSpec((1, DSA_W, tm), tr),
        pl.BlockSpec((1, widx_rows, tm), tr),
    )
    return pl.pallas_call(
        _inproj_kernel,
        grid=(T // tm,),
        in_specs=[
            pl.BlockSpec((tm, D), row),
            _resident((1, D), lambda i: (0, 0)),
            _resident(w_main.shape, lambda i: (0, 0)),
            _resident(w_t.shape, lambda i: (0, 0)),
        ],
        out_specs=out_specs,
        out_shape=out_shape,
        compiler_params=_params("parallel"),
        name="inproj",
    )(x2, nw, w_main, w_t)


def _retention_tables(S):
    half = HEAD_DIM // 2
    inv = RET_ROPE_BASE ** (-jnp.arange(half, dtype=F32) / half)
    ang = jnp.arange(S, dtype=F32)[:, None] * inv[None, :]
    cos, sin = jnp.cos(ang), jnp.sin(ang)
    cos_h = jnp.concatenate([cos, cos], axis=-1)
    sin_h = jnp.concatenate([-sin, sin], axis=-1)
    cosf = jnp.tile(cos_h, (1, RET_HEADS))
    sinf = jnp.tile(sin_h, (1, RET_HEADS))
    C = RET_CHUNK
    log_g = np.log(1.0 - 2.0 ** (-5.0 - np.arange(RET_HEADS, dtype=np.float64)))
    i = np.arange(C, dtype=np.float64)
    diff = i[:, None] - i[None, :]
    decay = np.exp(np.maximum(diff, 0.0)[None] * log_g[:, None, None]) * (diff >= 0)[None]
    zeta = np.exp((C - 1.0 - i)[None, :] * log_g[:, None])
    xi = np.exp((i + 1.0)[None, :] * log_g[:, None])
    g_chunk = np.exp(C * log_g)
    rep = lambda t: np.repeat(t.T, HEAD_DIM, axis=1)
    g_rows = np.repeat(g_chunk, HEAD_DIM)[:, None] * np.ones((1, RET_W))
    return (cosf, sinf, jnp.asarray(decay, F32), jnp.asarray(rep(zeta), F32), jnp.asarray(rep(xi), F32),
            jnp.asarray(g_rows, F32))


def _retention_kernel(cols_ref, cos_ref, sin_ref, decay_ref, zeta_ref, xi_ref, grow_ref, nw_ref, o_ref, state_ref):
    W = RET_W

    @pl.when(pl.program_id(1) == 0)
    def _():
        state_ref[...] = jnp.zeros_like(state_ref)

    lane = lax.broadcasted_iota(I32, (1, W), 1)
    first_half = (lane % HEAD_DIM) < (HEAD_DIM // 2)
    cosf, sinf = cos_ref[...], sin_ref[...]

    def rot(t):
        partner = jnp.where(first_half, pltpu.roll(t, W - HEAD_DIM // 2, 1), pltpu.roll(t, HEAD_DIM // 2, 1))
        return t * cosf + partner * sinf

    q = rot(cols_ref[:, 0:W])
    k = rot(cols_ref[:, W:2 * W]) * (HEAD_DIM ** -0.5)
    v = cols_ref[:, 2 * W:3 * W].astype(BF16)
    g = cols_ref[:, 3 * W:4 * W]

    head_of_lane = lane // HEAD_DIM
    kb = k.astype(BF16)
    inner = jnp.zeros((RET_CHUNK, W), F32)
    for h in range(RET_HEADS):
        sel = head_of_lane == h
        qh = jnp.where(sel, q, 0.0).astype(BF16)
        a = (_dot_nt(qh, kb) * decay_ref[h]).astype(BF16)
        inner = inner + jnp.where(sel, _dot(a, v), 0.0)

    state = state_ref[...]
    cross = _dot(q.astype(BF16), state.astype(BF16)) * xi_ref[...]
    o = inner + cross

    kzT = (k * zeta_ref[...]).T.astype(BF16)
    kv = _dot(kzT, v)
    r_head = lax.broadcasted_iota(I32, (W, W), 0) // HEAD_DIM
    c_head = lax.broadcasted_iota(I32, (W, W), 1) // HEAD_DIM
    state_ref[...] = grow_ref[...] * state + jnp.where(r_head == c_head, kv, 0.0)

    mu = jnp.zeros_like(o)
    for h in range(RET_HEADS):
        sel = head_of_lane == h
        mu = mu + jnp.where(sel, jnp.sum(jnp.where(sel, o, 0.0), axis=-1, keepdims=True), 0.0)
    mu = mu * (1.0 / HEAD_DIM)
    d = o - mu
    var = jnp.zeros_like(o)
    for h in range(RET_HEADS):
        sel = head_of_lane == h
        var = var + jnp.where(sel, jnp.sum(jnp.where(sel, d * d, 0.0), axis=-1, keepdims=True), 0.0)
    var = var * (1.0 / HEAD_DIM)
    y = d * lax.rsqrt(var + NORM_EPS) * nw_ref[...]
    o_ref[...] = (jax.nn.silu(g) * y).astype(o_ref.dtype)


def _retention(ret_cols, tables, norm_w, B, S):
    T = ret_cols.shape[0]
    C, W = RET_CHUNK, RET_W
    n = S // C
    cosf, sinf, decay, zeta, xi, g_rows = tables
    tok = lambda b, j: (b * n + j, 0)
    pos = lambda b, j: (j, 0)
    const2 = lambda b, j: (0, 0)
    return pl.pallas_call(
        _retention_kernel,
        grid=(B, n),
        in_specs=[
            pl.BlockSpec((C, 4 * W), tok),
            pl.BlockSpec((C, W), pos),
            pl.BlockSpec((C, W), pos),
            _resident(decay.shape, lambda b, j: (0, 0, 0)),
            _resident((C, W), const2),
            _resident((C, W), const2),
            _resident((W, W), const2),
            _resident((1, W), const2),
        ],
        out_specs=pl.BlockSpec((C, W), tok),
        out_shape=jax.ShapeDtypeStruct((T, W), BF16),
        scratch_shapes=[pltpu.VMEM((W, W), F32)],
        compiler_params=_params("parallel", "arbitrary"),
        name="retention",
    )(ret_cols, cosf, sinf, decay, zeta, xi, g_rows, norm_w)


def _dsa_kv_kernel(ckv_ref, nw_ref, wuk_ref, wuvT_ref, k_ref, vT_ref):
    c = _rms(ckv_ref[...], nw_ref[...]).astype(BF16)
    k_ref[...] = _dot(c, wuk_ref[...]).astype(k_ref.dtype)
    vT_ref[0] = _dot_nt(wuvT_ref[...], c).astype(vT_ref.dtype)


def _dsa_kv(ckv, nw, w_uk, w_uvT):
    T = ckv.shape[0]
    tm = DSA_KC
    return pl.pallas_call(
        _dsa_kv_kernel,
        grid=(T // tm,),
        in_specs=[
            pl.BlockSpec((tm, KV_LATENT), lambda i: (i, 0)),
            _resident((1, KV_LATENT), lambda i: (0, 0)),
            _resident(w_uk.shape, lambda i: (0, 0)),
            _resident(w_uvT.shape, lambda i: (0, 0)),
        ],
        out_specs=(pl.BlockSpec((tm, DSA_W), lambda i: (i, 0)),
                   pl.BlockSpec((1, DSA_W, tm), lambda i: (i, 0, 0))),
        out_shape=(jax.ShapeDtypeStruct((T, DSA_W), BF16),
                   jax.ShapeDtypeStruct((T // tm, DSA_W, tm), BF16)),
        compiler_params=_params("parallel"),
        name="dsa_kv",
    )(ckv, nw, w_uk, w_uvT)


def _sortable(x):
    b = pltpu.bitcast(x, I32)
    return b ^ ((b >> 31) & 0x7FFFFFFF)


def _dsa_kernel(qidxT_ref, dsaqT_ref, widxT_ref, kidx_ref, k_ref, vT_ref, o_ref,
                key_ref, hi_ref, lo_ref, digit_ref, cut_ref, qh_ref, s_ref, bias_ref, p_ref, acc_ref,
                *, n_sel, S):
    QB, KC, SUB = DSA_QB, DSA_KC, DSA_SUB
    n_sub = KC // SUB
    i = pl.program_id(1)
    n_chunks = i + 1
    qpos = i * QB + lax.broadcasted_iota(I32, (1, QB), 1)
    krow = lax.broadcasted_iota(I32, (SUB, 1), 0)
    w_scale = IDX_HEADS ** -0.5 * IDX_DIM ** -0.5
    widxT = widxT_ref[0] * w_scale

    def score_chunk(c, carry):
        off = pl.multiple_of(c * (2 * KC), 2 * KC)
        for sub in range(2 * KC // DSA_SCORE_ROWS):
            r0 = off + sub * DSA_SCORE_ROWS
            kc = kidx_ref[pl.ds(r0, DSA_SCORE_ROWS), :]
            acc = jnp.zeros((DSA_SCORE_ROWS, QB), F32)
            for g in range(0, IDX_HEADS, DSA_IDX_GROUP):
                logits = [_dot(kc, qidxT_ref[0, h * IDX_DIM:(h + 1) * IDX_DIM, :])
                          for h in range(g, g + DSA_IDX_GROUP)]
                for h, logit in zip(range(g, g + DSA_IDX_GROUP), logits):
                    acc = acc + jnp.maximum(logit, 0.0) * widxT[h:h + 1, :]
            for part in range(DSA_SCORE_ROWS // SUB):
                a = acc[part * SUB:(part + 1) * SUB, :]
                a = jnp.where(a == 0.0, 0.0, a)
                score = jnp.where(r0 + part * SUB + krow <= qpos, a, -jnp.inf)
                key = _sortable(score)
                rows = pl.ds(r0 + part * SUB, SUB)
                key_ref[rows, :] = key
                hi_ref[rows, :] = (key >> 16).astype(I16)
                lo_ref[rows, :] = ((key & 0xFFFF) - 0x8000).astype(I16)
        return carry

    lax.fori_loop(0, (n_chunks + 1) // 2, score_chunk, 0)

    def count(pred):
        def body(c, cnt):
            off = pl.multiple_of(c * KC, KC)
            for sub in range(n_sub):
                r0 = off + sub * SUB
                cnt = cnt + jnp.where(pred(key_ref[pl.ds(r0, SUB), :], r0 + krow), 1.0, 0.0)
            return cnt
        cnt = lax.fori_loop(0, n_chunks, body, jnp.zeros((SUB, QB), F32))
        return jnp.sum(cnt, axis=0, keepdims=True)

    one_b, zero_b = jnp.ones((SUB, QB), BF16), jnp.zeros((SUB, QB), BF16)

    def count_packed(ref, pred):
        def body(c, cnt):
            off = pl.multiple_of(c * KC, KC)
            for sub in range(n_sub):
                cnt = cnt + jnp.where(pred(ref[pl.ds(off + sub * SUB, SUB), :]), one_b, zero_b)
            return cnt
        cnt = lax.fori_loop(0, n_chunks, body, zero_b)
        return jnp.sum(cnt.astype(F32), axis=0, keepdims=True)

    def signed_digit(u):
        return (u - 0x8000).astype(I16)

    def search_digit(ref, n_above, n_ge_zero):
        def step(t, carry):
            d, n_ge = carry
            cand = d | jnp.left_shift(jnp.int32(1), 15 - t)
            cand_s = signed_digit(cand)
            cnt = n_above + count_packed(ref, lambda x: x >= cand_s)
            keep = cnt >= n_sel
            return jnp.where(keep, cand, d), jnp.where(keep, cnt, n_ge)
        return lax.fori_loop(0, 16, step, (jnp.zeros((1, QB), I32), n_ge_zero))

    stored = jnp.full((1, QB), (n_chunks * KC).astype(F32), F32)
    u_hi, n_ge = search_digit(hi_ref, 0.0, stored)
    thr_hi = signed_digit(u_hi)
    n_above = count_packed(hi_ref, lambda x: x > thr_hi)
    lowest = jnp.full((SUB, QB), -0x8000, I16)

    def low_digits_of_ties(c, carry):
        off = pl.multiple_of(c * KC, KC)
        for sub in range(n_sub):
            rows = pl.ds(off + sub * SUB, SUB)
            digit_ref[rows, :] = jnp.where(hi_ref[rows, :] == thr_hi, lo_ref[rows, :], lowest)
        return carry

    lax.fori_loop(0, n_chunks, low_digits_of_ties, 0)
    u_lo, n_ge = search_digit(digit_ref, n_above, n_ge)
    thr = ((u_hi - 0x8000) << 16) | u_lo

    needs_cut = (n_ge > n_sel) & (thr > KEY_NEG_INF)
    cut_ref[...] = jnp.full((1, QB), S, I32)

    @pl.when(jnp.max(jnp.where(needs_cut, 1.0, 0.0)) > 0.0)
    def _():
        tie_take = n_sel - count(lambda key, _: key > thr)

        def cut_step(t, cpos):
            cand = cpos | jnp.left_shift(jnp.int32(1), (S - 1).bit_length() - 1 - t)
            before = count(lambda key, kp: (key == thr) & (kp < cand))
            return jnp.where(before < tie_take, cand, cpos)
        cpos = lax.fori_loop(0, (S - 1).bit_length(), cut_step, jnp.zeros((1, QB), I32))
        cut_ref[...] = jnp.where(needs_cut, cpos, S)

    cut = cut_ref[...]

    row_head = lax.broadcasted_iota(I32, (DSA_W, 1), 0) // HEAD_DIM
    dsaqT = dsaqT_ref[0] * (HEAD_DIM ** -0.5)
    for h in range(DSA_HEADS):
        qh_ref[h] = jnp.where(row_head == h, dsaqT, 0.0).astype(BF16)
    acc_ref[...] = jnp.zeros_like(acc_ref)
    p_ref[...] = jnp.zeros_like(p_ref)

    def pv_update(c_prev, alphas):
        vch = vT_ref[c_prev]
        pvs = [_dot(vch[h * HEAD_DIM:(h + 1) * HEAD_DIM, :], p_ref[h]) for h in range(DSA_HEADS)]
        for h in range(DSA_HEADS):
            acc_ref[h] = alphas[h] * acc_ref[h] + pvs[h]

    def attend_chunk(c, carry):
        ms, ls, prev_alphas = carry
        off = pl.multiple_of(c * KC, KC)
        for sub in range(n_sub):
            r0 = off + sub * SUB
            key = key_ref[pl.ds(r0, SUB), :]
            sel = ((key > thr) | ((key == thr) & (r0 + krow <= cut))) & (key > KEY_NEG_INF)
            bias_ref[sub * SUB:(sub + 1) * SUB, :] = jnp.where(sel, 0.0, MASKED_SCORE)
        pv_update(jnp.maximum(c - 1, 0), prev_alphas)
        kch = k_ref[pl.ds(off, KC), :]
        for h in range(DSA_HEADS):
            s_ref[h] = _dot(kch, qh_ref[h])
        new_m, new_l, alphas = [], [], []
        for h in range(DSA_HEADS):
            mx = jnp.full((SUB, QB), MASKED_SCORE, F32)
            for sub in range(n_sub):
                rows = slice(sub * SUB, (sub + 1) * SUB)
                t = s_ref[h, rows, :] + bias_ref[rows, :]
                s_ref[h, rows, :] = t
                mx = jnp.maximum(mx, t)
            m_new = jnp.maximum(ms[h], jnp.max(mx, axis=0, keepdims=True))
            alpha = jnp.exp(ms[h] - m_new)
            psum = jnp.zeros((SUB, QB), F32)
            for sub in range(n_sub):
                rows = slice(sub * SUB, (sub + 1) * SUB)
                p = jnp.exp(s_ref[h, rows, :] - m_new)
                psum = psum + p
                p_ref[h, rows, :] = p.astype(BF16)
            new_m.append(m_new)
            new_l.append(alpha * ls[h] + jnp.sum(psum, axis=0, keepdims=True))
            alphas.append(alpha)
        return tuple(new_m), tuple(new_l), tuple(alphas)

    init = (tuple(jnp.full((1, QB), MASKED_SCORE, F32) for _ in range(DSA_HEADS)),
            tuple(jnp.zeros((1, QB), F32) for _ in range(DSA_HEADS)),
            tuple(jnp.ones((1, QB), F32) for _ in range(DSA_HEADS)))
    _, ls, last_alphas = lax.fori_loop(0, n_chunks, attend_chunk, init)
    pv_update(n_chunks - 1, last_alphas)
    oT = jnp.concatenate([acc_ref[h] / ls[h] for h in range(DSA_HEADS)], axis=0)
    o_ref[...] = oT.T.astype(o_ref.dtype)


def _dsa(qidxT, dsaqT, widxT, kidx, k, vT, B, S):
    QB, KC = DSA_QB, DSA_KC
    assert QB == KC and S % (2 * KC) == 0
    n_sel = min(TOPK_MAX, S // 4)
    assert n_sel <= KC, "the first causal chunk must be able to hold every selected key"
    assert S // DSA_SUB <= 256, "packed counts are exact only up to 256 adds per accumulator lane"
    nq = S // QB
    nkc = S // KC
    n_sel = min(TOPK_MAX, S // 4)
    T = B * S
    kern = functools.partial(_dsa_kernel, n_sel=n_sel, S=S)
    return pl.pallas_call(
        kern,
        grid=(B, nq),
        in_specs=[
            pl.BlockSpec((1, IDX_HEADS * IDX_DIM, QB), lambda b, i: (b, 0, i)),
            pl.BlockSpec((1, DSA_W, QB), lambda b, i: (b, 0, i)),
            pl.BlockSpec((1, widxT.shape[1], QB), lambda b, i: (b, 0, i)),
            _resident((S, IDX_DIM), lambda b, i: (b, 0)),
            _resident((S, DSA_W), lambda b, i: (b, 0)),
            _resident((nkc, DSA_W, KC), lambda b, i: (b, 0, 0)),
        ],
        out_specs=pl.BlockSpec((QB, DSA_W), lambda b, i: (b * nq + i, 0)),
        out_shape=jax.ShapeDtypeStruct((T, DSA_W), BF16),
        scratch_shapes=[
            pltpu.VMEM((S, QB), I32),
            pltpu.VMEM((S, QB), I16),
            pltpu.VMEM((S, QB), I16),
            pltpu.VMEM((S, QB), I16),
            pltpu.VMEM((1, QB), I32),
            pltpu.VMEM((DSA_HEADS, DSA_W, QB), BF16),
            pltpu.VMEM((DSA_HEADS, KC, QB), F32),
            pltpu.VMEM((KC, QB), F32),
            pltpu.VMEM((DSA_HEADS, KC, QB), BF16),
            pltpu.VMEM((DSA_HEADS, HEAD_DIM, QB), F32),
        ],
        compiler_params=_params("parallel", "arbitrary"),
        name="dsa",
    )(qidxT, dsaqT, widxT, kidx, k, vT)


def _dilated_kernel(q_ref, kp_ref, kc_ref, vp_ref, vc_ref, o_ref, lse_ref, *, steps):
    L = DIL_BLOCK
    j = pl.program_id(2)
    qi = lax.broadcasted_iota(I32, (L, 2 * L), 0)
    kj = lax.broadcasted_iota(I32, (L, 2 * L), 1)
    dist = L + qi - kj
    mask = (dist >= 0) & (dist <= steps) & (j * L - L + kj >= 0)
    lane = lax.broadcasted_iota(I32, (1, 2 * HEAD_DIM), 1)
    low = lane < HEAD_DIM
    for pair in range(DIL_HEADS // 2):
        cs = slice(pair * 2 * HEAD_DIM, (pair + 1) * 2 * HEAD_DIM)
        q = q_ref[0, :, cs]
        kk = jnp.concatenate([kp_ref[0, :, cs], kc_ref[0, :, cs]], axis=0)
        vv = jnp.concatenate([vp_ref[0, :, cs], vc_ref[0, :, cs]], axis=0)
        outs, lses = [], []
        for sub in range(2):
            qh = jnp.where(low if sub == 0 else ~low, q, jnp.zeros_like(q))
            s = _dot_nt(qh, kk) * (HEAD_DIM ** -0.5)
            s = jnp.where(mask, s, -jnp.inf)
            m = jnp.max(s, axis=-1, keepdims=True)
            p = jnp.exp(s - m)
            den = jnp.sum(p, axis=-1, keepdims=True)
            outs.append(_dot(p.astype(BF16), vv) / den)
            lses.append(m + jnp.log(den))
        o_ref[0, :, cs] = jnp.where(low, outs[0], outs[1])
        lse_ref[0, :, cs] = jnp.where(low, lses[0], lses[1])


def _dilated(q, k, v, B, S, window, dilation):
    steps = window // dilation
    n = S // dilation
    L = DIL_BLOCK
    assert steps == L and n % L == 0, "dilated branch supports window/dilation == 128 and S % (128*dilation) == 0"
    W = DIL_W
    view = lambda t: t.reshape(B, n, dilation * W)
    own = lambda b, r, j: (b, j, r)
    prev = lambda b, r, j: (b, jnp.maximum(j - 1, 0), r)
    blk = (1, L, W)
    o, lse = pl.pallas_call(
        functools.partial(_dilated_kernel, steps=steps),
        grid=(B, dilation, n // L),
        in_specs=[pl.BlockSpec(blk, own), pl.BlockSpec(blk, prev), pl.BlockSpec(blk, own),
                  pl.BlockSpec(blk, prev), pl.BlockSpec(blk, own)],
        out_specs=(pl.BlockSpec(blk, own), pl.BlockSpec(blk, own)),
        out_shape=(jax.ShapeDtypeStruct((B, n, dilation * W), F32),
                   jax.ShapeDtypeStruct((B, n, dilation * W), F32)),
        compiler_params=_params("parallel", "parallel", "parallel"),
        name=f"dilated_d{dilation}",
    )(view(q), view(k), view(k), view(v), view(v))
    return o.reshape(B * S, W), lse.reshape(B * S, W)


def _dil_merge_kernel(o1, o2, o3, l1, l2, l3, out_ref):
    a, b, c = l1[...], l2[...], l3[...]
    m = jnp.maximum(jnp.maximum(a, b), c)
    ea, eb, ec = jnp.exp(a - m), jnp.exp(b - m), jnp.exp(c - m)
    tot = ea + eb + ec
    out_ref[...] = ((ea * o1[...] + eb * o2[...] + ec * o3[...]) / tot).astype(out_ref.dtype)


def _dil_merge(os, lses):
    T, W = os[0].shape
    tm = ROW_TILE
    spec = pl.BlockSpec((tm, W), lambda i: (i, 0))
    return pl.pallas_call(
        _dil_merge_kernel,
        grid=(T // tm,),
        in_specs=[spec] * 6,
        out_specs=spec,
        out_shape=jax.ShapeDtypeStruct((T, W), BF16),
        compiler_params=_params("parallel"),
        name="dil_merge",
    )(*os, *lses)


def _mlp_kernel(x_ref, ret_ref, dsa_ref, dil_ref, wout_ref, nw_ref, wup_ref, wdown_ref, o_ref):
    mixed = (_dot(ret_ref[...], wout_ref[0:RET_W, :])
             + _dot(dsa_ref[...], wout_ref[RET_W:RET_W + DSA_W, :])
             + _dot(dil_ref[...], wout_ref[RET_W + DSA_W:, :]))
    x = x_ref[...] + mixed
    h = _rms(x, nw_ref[...]).astype(BF16)
    ff = None
    for c in range(wup_ref.shape[1] // FF_CHUNK):
        cs = slice(c * FF_CHUNK, (c + 1) * FF_CHUNK)
        u = jnp.maximum(_dot(h, wup_ref[:, cs]), 0.0)
        part = _dot((u * u).astype(BF16), wdown_ref[cs, :])
        ff = part if ff is None else ff + part
    o_ref[...] = x + ff


def _mlp(x2, ret_o, dsa_o, dil_o, w_out, nw, w_up, w_down):
    T, D = x2.shape
    tm = ROW_TILE
    row = lambda i: (i, 0)
    const = lambda i: (0, 0)
    return pl.pallas_call(
        _mlp_kernel,
        grid=(T // tm,),
        in_specs=[
            pl.BlockSpec((tm, D), row),
            pl.BlockSpec((tm, RET_W), row),
            pl.BlockSpec((tm, DSA_W), row),
            pl.BlockSpec((tm, DIL_W), row),
            _resident(w_out.shape, const),
            _resident((1, D), const),
            _resident(w_up.shape, const),
            _resident(w_down.shape, const),
        ],
        out_specs=pl.BlockSpec((tm, D), row),
        out_shape=jax.ShapeDtypeStruct((T, D), F32),
        compiler_params=_params("parallel"),
        name="outproj_mlp",
    )(x2, ret_o, dsa_o, dil_o, w_out, nw, w_up, w_down)


def _final_norm_kernel(x_ref, w_ref, o_ref):
    o_ref[...] = _rms(x_ref[...], w_ref[...])


def _final_norm(x2, w):
    T, D = x2.shape
    tm = ROW_TILE
    return pl.pallas_call(
        _final_norm_kernel,
        grid=(T // tm,),
        in_specs=[pl.BlockSpec((tm, D), lambda i: (i, 0)), _resident((1, D), lambda i: (0, 0))],
        out_specs=pl.BlockSpec((tm, D), lambda i: (i, 0)),
        out_shape=jax.ShapeDtypeStruct((T, D), F32),
        compiler_params=_params("parallel"),
        name="final_norm",
    )(x2, w)


def _split_w_in(w_in):
    ret_cols = 4 * RET_W
    o = ret_cols
    dsa_q = w_in[:, o:o + DSA_W]; o += DSA_W
    c_kv = w_in[:, o:o + KV_LATENT]; o += KV_LATENT
    q_idx = w_in[:, o:o + IDX_HEADS * IDX_DIM]; o += IDX_HEADS * IDX_DIM
    k_idx = w_in[:, o:o + IDX_DIM]; o += IDX_DIM
    w_idx = w_in[:, o:o + IDX_HEADS]; o += IDX_HEADS
    dil = w_in[:, o:]
    w_main = jnp.concatenate([w_in[:, :ret_cols], dil, c_kv, k_idx], axis=1).astype(BF16)
    pad = jnp.zeros((w_in.shape[0], WIDX_ROWS - IDX_HEADS), w_in.dtype)
    w_t = jnp.concatenate([q_idx, dsa_q, w_idx, pad], axis=1).T.astype(BF16)
    return w_main, w_t


def kernel(x, attn_norm_w, w_in, ret_norm_w, dsa_kv_norm_w, dsa_w_uk, dsa_w_uv, w_out, mlp_norm_w, w_up, w_down, final_norm_w):
    B, S, D = x.shape
    depth = w_in.shape[0]
    assert S % ROW_TILE == 0 and S % DSA_KC == 0 and S % RET_CHUNK == 0
    tables = _retention_tables(S)
    x2 = x.reshape(B * S, D)
    for layer in range(depth):
        w_main, w_t = _split_w_in(w_in[layer])
        ret_cols, dq, dk, dv, ckv, kidx, qidxT, dsaqT, widxT = _inproj(
            x2, attn_norm_w[layer][None, :], w_main, w_t, B, S)
        ret_o = _retention(ret_cols, tables, ret_norm_w[layer].reshape(1, RET_W), B, S)
        k, vT = _dsa_kv(ckv, dsa_kv_norm_w[layer][None, :], dsa_w_uk[layer].astype(BF16),
                        dsa_w_uv[layer].T.astype(BF16))
        dsa_o = _dsa(qidxT, dsaqT, widxT, kidx, k, vT, B, S)
        branches = [_dilated(dq, dk, dv, B, S, window, dilation) for window, dilation in DIL_PATTERNS]
        dil_o = _dil_merge([o for o, _ in branches], [l for _, l in branches])
        x2 = _mlp(x2, ret_o, dsa_o, dil_o, w_out[layer].astype(BF16), mlp_norm_w[layer][None, :],
                  w_up[layer].astype(BF16), w_down[layer].astype(BF16))
    return _final_norm(x2, final_norm_w[None, :]).reshape(B, S, D)
```

```python
import functools
import math

import numpy as np
import jax
import jax.numpy as jnp
from jax import lax
from jax.experimental import pallas as pl
from jax.experimental.pallas import tpu as pltpu

F32 = jnp.float32
BF16 = jnp.bfloat16
I32 = jnp.int32
I16 = jnp.int16

HEAD_DIM = 64
RET_HEADS = 4
DSA_HEADS = 4
DIL_HEADS = 8
RET_W = RET_HEADS * HEAD_DIM
DSA_W = DSA_HEADS * HEAD_DIM
DIL_W = DIL_HEADS * HEAD_DIM
RET_CHUNK = 128
RET_ROPE_BASE = 10000.0
KV_LATENT = 128
IDX_HEADS = 8
IDX_DIM = 64
TOPK_MAX = 256
DIL_PATTERNS = ((128, 1), (512, 4), (2048, 16))
DIL_BLOCK = 128
DIL_SPAN = 2048
DIL_GROUP = 4
DIL_MERGE_ROWS = 256
NORM_EPS = 1e-6

V7X_VMEM_LIMIT_BYTES = 56 * 1024 * 1024

ROW_TILE = 512
DSA_QB = 256
DSA_KC = 256
DSA_SUB = 64
DSA_SCORE_ROWS = 128
DSA_IDX_GROUP = 4
FF_CHUNK = 1024

WIDX_ROWS = 16

INT_MIN = -(2 ** 31)
KEY_NEG_INF = int(np.int32(np.uint32(0xFF800000) ^ np.uint32(0x7FFFFFFF)))
MASKED_SCORE = -1e30


def _params(*semantics):
    return pltpu.CompilerParams(dimension_semantics=semantics, vmem_limit_bytes=V7X_VMEM_LIMIT_BYTES)


def _resident(shape, index_map):
    return pl.BlockSpec(shape, index_map, pipeline_mode=pl.Buffered(1))


def _dot(a, b):
    return jnp.dot(a, b, preferred_element_type=F32)


def _dot_nt(a, b):
    return lax.dot_general(a, b, (((1,), (1,)), ((), ())), preferred_element_type=F32)


def _rms(x, w):
    return x * lax.rsqrt(jnp.mean(x * x, axis=-1, keepdims=True) + NORM_EPS) * w


def _inproj_kernel(x_ref, nw_ref, w_ref, wt_ref, ret_ref, dq_ref, dk_ref, dv_ref, ckv_ref, kidx_ref,
                   qidxT_ref, dsaqT_ref, widxT_ref):
    h = _rms(x_ref[...], nw_ref[...]).astype(BF16)
    o = 0
    for ref in (ret_ref, dq_ref, dk_ref, dv_ref, ckv_ref, kidx_ref):
        n = ref.shape[-1]
        ref[...] = _dot(h, w_ref[:, o:o + n]).astype(ref.dtype)
        o += n
    o = 0
    for ref in (qidxT_ref, dsaqT_ref, widxT_ref):
        n = ref.shape[-2]
        ref[0] = _dot_nt(wt_ref[o:o + n, :], h).astype(ref.dtype)
        o += n


def _inproj(x2, nw, w_main, w_t, B, S):
    T, D = x2.shape
    tm = ROW_TILE
    nt = S // tm
    widx_rows = w_t.shape[0] - IDX_HEADS * IDX_DIM - DSA_W
    row = lambda i: (i, 0)
    tr = lambda i: (i // nt, 0, i % nt)
    out_shape = (
        jax.ShapeDtypeStruct((T, 4 * RET_W), F32),
        jax.ShapeDtypeStruct((T, DIL_W), BF16),
        jax.ShapeDtypeStruct((T, DIL_W), BF16),
        jax.ShapeDtypeStruct((T, DIL_W), BF16),
        jax.ShapeDtypeStruct((T, KV_LATENT), F32),
        jax.ShapeDtypeStruct((T, IDX_DIM), BF16),
        jax.ShapeDtypeStruct((B, IDX_HEADS * IDX_DIM, S), BF16),
        jax.ShapeDtypeStruct((B, DSA_W, S), BF16),
        jax.ShapeDtypeStruct((B, widx_rows, S), F32),
    )
    out_specs = (
        pl.BlockSpec((tm, 4 * RET_W), row),
        pl.BlockSpec((tm, DIL_W), row),
        pl.BlockSpec((tm, DIL_W), row),
        pl.BlockSpec((tm, DIL_W), row),
        pl.BlockSpec((tm, KV_LATENT), row),
        pl.BlockSpec((tm, IDX_DIM), row),
        pl.BlockSpec((1, IDX_HEADS * IDX_DIM, tm), tr),
        pl.BlockSpec((1, DSA_W, tm), tr),
        pl.BlockSpec((1, widx_rows, tm), tr),
    )
    return pl.pallas_call(
        _inproj_kernel,
        grid=(T // tm,),
        in_specs=[
            pl.BlockSpec((tm, D), row),
            _resident((1, D), lambda i: (0, 0)),
            _resident(w_main.shape, lambda i: (0, 0)),
            _resident(w_t.shape, lambda i: (0, 0)),
        ],
        out_specs=out_specs,
        out_shape=out_shape,
        compiler_params=_params("parallel"),
        name="inproj",
    )(x2, nw, w_main, w_t)


def _retention_tables(S):
    half = HEAD_DIM // 2
    inv = RET_ROPE_BASE ** (-jnp.arange(half, dtype=F32) / half)
    ang = jnp.arange(S, dtype=F32)[:, None] * inv[None, :]
    cos, sin = jnp.cos(ang), jnp.sin(ang)
    cos_h = jnp.concatenate([cos, cos], axis=-1)
    sin_h = jnp.concatenate([-sin, sin], axis=-1)
    cosf = jnp.tile(cos_h, (1, RET_HEADS))
    sinf = jnp.tile(sin_h, (1, RET_HEADS))
    C = RET_CHUNK
    log_g = np.log(1.0 - 2.0 ** (-5.0 - np.arange(RET_HEADS, dtype=np.float64)))
    i = np.arange(C, dtype=np.float64)
    diff = i[:, None] - i[None, :]
    decay = np.exp(np.maximum(diff, 0.0)[None] * log_g[:, None, None]) * (diff >= 0)[None]
    zeta = np.exp((C - 1.0 - i)[None, :] * log_g[:, None])
    xi = np.exp((i + 1.0)[None, :] * log_g[:, None])
    g_chunk = np.exp(C * log_g)
    rep = lambda t: np.repeat(t.T, HEAD_DIM, axis=1)
    g_rows = np.repeat(g_chunk, HEAD_DIM)[:, None] * np.ones((1, RET_W))
    return (cosf, sinf, jnp.asarray(decay, F32), jnp.asarray(rep(zeta), F32), jnp.asarray(rep(xi), F32),
            jnp.asarray(g_rows, F32))


def _retention_kernel(cols_ref, cos_ref, sin_ref, decay_ref, zeta_ref, xi_ref, grow_ref, nw_ref, o_ref, state_ref):
    W = RET_W

    @pl.when(pl.program_id(1) == 0)
    def _():
        state_ref[...] = jnp.zeros_like(state_ref)

    lane = lax.broadcasted_iota(I32, (1, W), 1)
    first_half = (lane % HEAD_DIM) < (HEAD_DIM // 2)
    cosf, sinf = cos_ref[...], sin_ref[...]

    def rot(t):
        partner = jnp.where(first_half, pltpu.roll(t, W - HEAD_DIM // 2, 1), pltpu.roll(t, HEAD_DIM // 2, 1))
        return t * cosf + partner * sinf

    q = rot(cols_ref[:, 0:W])
    k = rot(cols_ref[:, W:2 * W]) * (HEAD_DIM ** -0.5)
    v = cols_ref[:, 2 * W:3 * W].astype(BF16)
    g = cols_ref[:, 3 * W:4 * W]

    head_of_lane = lane // HEAD_DIM
    kb = k.astype(BF16)
    inner = jnp.zeros((RET_CHUNK, W), F32)
    for h in range(RET_HEADS):
        sel = head_of_lane == h
        qh = jnp.where(sel, q, 0.0).astype(BF16)
        a = (_dot_nt(qh, kb) * decay_ref[h]).astype(BF16)
        inner = inner + jnp.where(sel, _dot(a, v), 0.0)

    state = state_ref[...]
    cross = _dot(q.astype(BF16), state.astype(BF16)) * xi_ref[...]
    o = inner + cross

    kzT = (k * zeta_ref[...]).T.astype(BF16)
    kv = _dot(kzT, v)
    r_head = lax.broadcasted_iota(I32, (W, W), 0) // HEAD_DIM
    c_head = lax.broadcasted_iota(I32, (W, W), 1) // HEAD_DIM
    state_ref[...] = grow_ref[...] * state + jnp.where(r_head == c_head, kv, 0.0)

    mu = jnp.zeros_like(o)
    for h in range(RET_HEADS):
        sel = head_of_lane == h
        mu = mu + jnp.where(sel, jnp.sum(jnp.where(sel, o, 0.0), axis=-1, keepdims=True), 0.0)
    mu = mu * (1.0 / HEAD_DIM)
    d = o - mu
    var = jnp.zeros_like(o)
    for h in range(RET_HEADS):
        sel = head_of_lane == h
        var = var + jnp.where(sel, jnp.sum(jnp.where(sel, d * d, 0.0), axis=-1, keepdims=True), 0.0)
    var = var * (1.0 / HEAD_DIM)
    y = d * lax.rsqrt(var + NORM_EPS) * nw_ref[...]
    o_ref[...] = (jax.nn.silu(g) * y).astype(o_ref.dtype)


def _retention(ret_cols, tables, norm_w, B, S):
    T = ret_cols.shape[0]
    C, W = RET_CHUNK, RET_W
    n = S // C
    cosf, sinf, decay, zeta, xi, g_rows = tables
    tok = lambda b, j: (b * n + j, 0)
    pos = lambda b, j: (j, 0)
    const2 = lambda b, j: (0, 0)
    return pl.pallas_call(
        _retention_kernel,
        grid=(B, n),
        in_specs=[
            pl.BlockSpec((C, 4 * W), tok),
            pl.BlockSpec((C, W), pos),
            pl.BlockSpec((C, W), pos),
            _resident(decay.shape, lambda b, j: (0, 0, 0)),
            _resident((C, W), const2),
            _resident((C, W), const2),
            _resident((W, W), const2),
            _resident((1, W), const2),
        ],
        out_specs=pl.BlockSpec((C, W), tok),
        out_shape=jax.ShapeDtypeStruct((T, W), BF16),
        scratch_shapes=[pltpu.VMEM((W, W), F32)],
        compiler_params=_params("parallel", "arbitrary"),
        name="retention",
    )(ret_cols, cosf, sinf, decay, zeta, xi, g_rows, norm_w)


def _dsa_kv_kernel(ckv_ref, nw_ref, wuk_ref, wuvT_ref, k_ref, vT_ref):
    c = _rms(ckv_ref[...], nw_ref[...]).astype(BF16)
    k_ref[...] = _dot(c, wuk_ref[...]).astype(k_ref.dtype)
    vT_ref[0] = _dot_nt(wuvT_ref[...], c).astype(vT_ref.dtype)


def _dsa_kv(ckv, nw, w_uk, w_uvT):
    T = ckv.shape[0]
    tm = DSA_KC
    return pl.pallas_call(
        _dsa_kv_kernel,
        grid=(T // tm,),
        in_specs=[
            pl.BlockSpec((tm, KV_LATENT), lambda i: (i, 0)),
            _resident((1, KV_LATENT), lambda i: (0, 0)),
            _resident(w_uk.shape, lambda i: (0, 0)),
            _resident(w_uvT.shape, lambda i: (0, 0)),
        ],
        out_specs=(pl.BlockSpec((tm, DSA_W), lambda i: (i, 0)),
                   pl.BlockSpec((1, DSA_W, tm), lambda i: (i, 0, 0))),
        out_shape=(jax.ShapeDtypeStruct((T, DSA_W), BF16),
                   jax.ShapeDtypeStruct((T // tm, DSA_W, tm), BF16)),
        compiler_params=_params("parallel"),
        name="dsa_kv",
    )(ckv, nw, w_uk, w_uvT)


def _sortable(x):
    b = pltpu.bitcast(x, I32)
    return b ^ ((b >> 31) & 0x7FFFFFFF)


def _dsa_kernel(qidxT_ref, dsaqT_ref, widxT_ref, kidx_ref, k_ref, vT_ref, o_ref,
                key_ref, hi_ref, lo_ref, digit_ref, cut_ref, qh_ref, s_ref, bias_ref, p_ref, acc_ref,
                *, n_sel, S):
    QB, KC, SUB = DSA_QB, DSA_KC, DSA_SUB
    n_sub = KC // SUB
    i = pl.program_id(1)
    n_chunks = i + 1
    qpos = i * QB + lax.broadcasted_iota(I32, (1, QB), 1)
    krow = lax.broadcasted_iota(I32, (SUB, 1), 0)
    w_scale = IDX_HEADS ** -0.5 * IDX_DIM ** -0.5
    widxT = widxT_ref[0] * w_scale

    def score_chunk(c, carry):
        off = pl.multiple_of(c * (2 * KC), 2 * KC)
        for sub in range(2 * KC // DSA_SCORE_ROWS):
            r0 = off + sub * DSA_SCORE_ROWS
            kc = kidx_ref[pl.ds(r0, DSA_SCORE_ROWS), :]
            acc = jnp.zeros((DSA_SCORE_ROWS, QB), F32)
            for g in range(0, IDX_HEADS, DSA_IDX_GROUP):
                logits = [_dot(kc, qidxT_ref[0, h * IDX_DIM:(h + 1) * IDX_DIM, :])
                          for h in range(g, g + DSA_IDX_GROUP)]
                for h, logit in zip(range(g, g + DSA_IDX_GROUP), logits):
                    acc = acc + jnp.maximum(logit, 0.0) * widxT[h:h + 1, :]
            for part in range(DSA_SCORE_ROWS // SUB):
                a = acc[part * SUB:(part + 1) * SUB, :]
                a = jnp.where(a == 0.0, 0.0, a)
                score = jnp.where(r0 + part * SUB + krow <= qpos, a, -jnp.inf)
                key = _sortable(score)
                rows = pl.ds(r0 + part * SUB, SUB)
                key_ref[rows, :] = key
                hi_ref[rows, :] = (key >> 16).astype(I16)
                lo_ref[rows, :] = ((key & 0xFFFF) - 0x8000).astype(I16)
        return carry

    lax.fori_loop(0, (n_chunks + 1) // 2, score_chunk, 0)

    def count(pred):
        def body(c, cnt):
            off = pl.multiple_of(c * KC, KC)
            for sub in range(n_sub):
                r0 = off + sub * SUB
                cnt = cnt + jnp.where(pred(key_ref[pl.ds(r0, SUB), :], r0 + krow), 1.0, 0.0)
            return cnt
        cnt = lax.fori_loop(0, n_chunks, body, jnp.zeros((SUB, QB), F32))
        return jnp.sum(cnt, axis=0, keepdims=True)

    one_b, zero_b = jnp.ones((SUB, QB), BF16), jnp.zeros((SUB, QB), BF16)

    def count_packed(ref, pred):
        def body(c, cnt):
            off = pl.multiple_of(c * KC, KC)
            for sub in range(n_sub):
                cnt = cnt + jnp.where(pred(ref[pl.ds(off + sub * SUB, SUB), :]), one_b, zero_b)
            return cnt
        cnt = lax.fori_loop(0, n_chunks, body, zero_b)
        return jnp.sum(cnt.astype(F32), axis=0, keepdims=True)

    def signed_digit(u):
        return (u - 0x8000).astype(I16)

    def search_digit(ref, n_above, n_ge_zero):
        def step(t, carry):
            d, n_ge = carry
            cand = d | jnp.left_shift(jnp.int32(1), 15 - t)
            cand_s = signed_digit(cand)
            cnt = n_above + count_packed(ref, lambda x: x >= cand_s)
            keep = cnt >= n_sel
            return jnp.where(keep, cand, d), jnp.where(keep, cnt, n_ge)
        return lax.fori_loop(0, 16, step, (jnp.zeros((1, QB), I32), n_ge_zero))

    stored = jnp.full((1, QB), (n_chunks * KC).astype(F32), F32)
    u_hi, n_ge = search_digit(hi_ref, 0.0, stored)
    thr_hi = signed_digit(u_hi)
    n_above = count_packed(hi_ref, lambda x: x > thr_hi)
    lowest = jnp.full((SUB, QB), -0x8000, I16)

    def low_digits_of_ties(c, carry):
        off = pl.multiple_of(c * KC, KC)
        for sub in range(n_sub):
            rows = pl.ds(off + sub * SUB, SUB)
            digit_ref[rows, :] = jnp.where(hi_ref[rows, :] == thr_hi, lo_ref[rows, :], lowest)
        return carry

    lax.fori_loop(0, n_chunks, low_digits_of_ties, 0)
    u_lo, n_ge = search_digit(digit_ref, n_above, n_ge)
    thr = ((u_hi - 0x8000) << 16) | u_lo

    needs_cut = (n_ge > n_sel) & (thr > KEY_NEG_INF)
    cut_ref[...] = jnp.full((1, QB), S, I32)

    @pl.when(jnp.max(jnp.where(needs_cut, 1.0, 0.0)) > 0.0)
    def _():
        tie_take = n_sel - count(lambda key, _: key > thr)

        def cut_step(t, cpos):
            cand = cpos | jnp.left_shift(jnp.int32(1), (S - 1).bit_length() - 1 - t)
            before = count(lambda key, kp: (key == thr) & (kp < cand))
            return jnp.where(before < tie_take, cand, cpos)
        cpos = lax.fori_loop(0, (S - 1).bit_length(), cut_step, jnp.zeros((1, QB), I32))
        cut_ref[...] = jnp.where(needs_cut, cpos, S)

    cut = cut_ref[...]

    row_head = lax.broadcasted_iota(I32, (DSA_W, 1), 0) // HEAD_DIM
    dsaqT = dsaqT_ref[0] * (HEAD_DIM ** -0.5)
    for h in range(DSA_HEADS):
        qh_ref[h] = jnp.where(row_head == h, dsaqT, 0.0).astype(BF16)
    acc_ref[...] = jnp.zeros_like(acc_ref)
    p_ref[...] = jnp.zeros_like(p_ref)

    def pv_update(c_prev, alphas):
        vch = vT_ref[c_prev]
        pvs = [_dot(vch[h * HEAD_DIM:(h + 1) * HEAD_DIM, :], p_ref[h]) for h in range(DSA_HEADS)]
        for h in range(DSA_HEADS):
            acc_ref[h] = alphas[h] * acc_ref[h] + pvs[h]

    def attend_chunk(c, carry):
        ms, ls, prev_alphas = carry
        off = pl.multiple_of(c * KC, KC)
        for sub in range(n_sub):
            r0 = off + sub * SUB
            key = key_ref[pl.ds(r0, SUB), :]
            sel = ((key > thr) | ((key == thr) & (r0 + krow <= cut))) & (key > KEY_NEG_INF)
            bias_ref[sub * SUB:(sub + 1) * SUB, :] = jnp.where(sel, 0.0, MASKED_SCORE)
        pv_update(jnp.maximum(c - 1, 0), prev_alphas)
        kch = k_ref[pl.ds(off, KC), :]
        for h in range(DSA_HEADS):
            s_ref[h] = _dot(kch, qh_ref[h])
        new_m, new_l, alphas = [], [], []
        for h in range(DSA_HEADS):
            mx = jnp.full((SUB, QB), MASKED_SCORE, F32)
            for sub in range(n_sub):
                rows = slice(sub * SUB, (sub + 1) * SUB)
                t = s_ref[h, rows, :] + bias_ref[rows, :]
                s_ref[h, rows, :] = t
                mx = jnp.maximum(mx, t)
            m_new = jnp.maximum(ms[h], jnp.max(mx, axis=0, keepdims=True))
            alpha = jnp.exp(ms[h] - m_new)
            psum = jnp.zeros((SUB, QB), F32)
            for sub in range(n_sub):
                rows = slice(sub * SUB, (sub + 1) * SUB)
                p = jnp.exp(s_ref[h, rows, :] - m_new)
                psum = psum + p
                p_ref[h, rows, :] = p.astype(BF16)
            new_m.append(m_new)
            new_l.append(alpha * ls[h] + jnp.sum(psum, axis=0, keepdims=True))
            alphas.append(alpha)
        return tuple(new_m), tuple(new_l), tuple(alphas)

    init = (tuple(jnp.full((1, QB), MASKED_SCORE, F32) for _ in range(DSA_HEADS)),
            tuple(jnp.zeros((1, QB), F32) for _ in range(DSA_HEADS)),
            tuple(jnp.ones((1, QB), F32) for _ in range(DSA_HEADS)))
    _, ls, last_alphas = lax.fori_loop(0, n_chunks, attend_chunk, init)
    pv_update(n_chunks - 1, last_alphas)
    oT = jnp.concatenate([acc_ref[h] / ls[h] for h in range(DSA_HEADS)], axis=0)
    o_ref[...] = oT.T.astype(o_ref.dtype)


def _dsa(qidxT, dsaqT, widxT, kidx, k, vT, B, S):
    QB, KC = DSA_QB, DSA_KC
    assert QB == KC and S % (2 * KC) == 0
    n_sel = min(TOPK_MAX, S // 4)
    assert n_sel <= KC, "the first causal chunk must be able to hold every selected key"
    assert S // DSA_SUB <= 256, "packed counts are exact only up to 256 adds per accumulator lane"
    nq = S // QB
    nkc = S // KC
    n_sel = min(TOPK_MAX, S // 4)
    T = B * S
    kern = functools.partial(_dsa_kernel, n_sel=n_sel, S=S)
    return pl.pallas_call(
        kern,
        grid=(B, nq),
        in_specs=[
            pl.BlockSpec((1, IDX_HEADS * IDX_DIM, QB), lambda b, i: (b, 0, i)),
            pl.BlockSpec((1, DSA_W, QB), lambda b, i: (b, 0, i)),
            pl.BlockSpec((1, widxT.shape[1], QB), lambda b, i: (b, 0, i)),
            _resident((S, IDX_DIM), lambda b, i: (b, 0)),
            _resident((S, DSA_W), lambda b, i: (b, 0)),
            _resident((nkc, DSA_W, KC), lambda b, i: (b, 0, 0)),
        ],
        out_specs=pl.BlockSpec((QB, DSA_W), lambda b, i: (b * nq + i, 0)),
        out_shape=jax.ShapeDtypeStruct((T, DSA_W), BF16),
        scratch_shapes=[
            pltpu.VMEM((S, QB), I32),
            pltpu.VMEM((S, QB), I16),
            pltpu.VMEM((S, QB), I16),
            pltpu.VMEM((S, QB), I16),
            pltpu.VMEM((1, QB), I32),
            pltpu.VMEM((DSA_HEADS, DSA_W, QB), BF16),
            pltpu.VMEM((DSA_HEADS, KC, QB), F32),
            pltpu.VMEM((KC, QB), F32),
            pltpu.VMEM((DSA_HEADS, KC, QB), BF16),
            pltpu.VMEM((DSA_HEADS, HEAD_DIM, QB), F32),
        ],
        compiler_params=_params("parallel", "arbitrary"),
        name="dsa",
    )(qidxT, dsaqT, widxT, kidx, k, vT)


def _dil_units(dilation):
    blocks = DIL_SPAN // (DIL_BLOCK * dilation)
    return [(res, jb) for res in range(dilation) for jb in range(blocks)]


def _dilated_fused_kernel(q_ref, kp_ref, kc_ref, vp_ref, vc_ref, out_ref,
                          qf_ref, kf_ref, vf_ref, s_ref, p_ref, o_ref, lse_ref):
    L, SPAN = DIL_BLOCK, DIL_SPAN
    first_span = pl.program_id(1) == 0
    qf_ref[...] = q_ref[...].astype(F32) * (HEAD_DIM ** -0.5)
    kf_ref[0:SPAN, :] = kp_ref[...].astype(F32)
    kf_ref[SPAN:, :] = kc_ref[...].astype(F32)
    vf_ref[0:SPAN, :] = vp_ref[...].astype(F32)
    vf_ref[SPAN:, :] = vc_ref[...].astype(F32)

    lane = lax.broadcasted_iota(I32, (1, 2 * HEAD_DIM), 1)
    low = lane < HEAD_DIM
    qi = lax.broadcasted_iota(I32, (L, 2 * L), 0)
    kj = lax.broadcasted_iota(I32, (L, 2 * L), 1)
    dist = L + qi - kj
    band = (dist >= 0) & (dist <= L)
    bias_band = jnp.where(band, 0.0, -jnp.inf)
    bias_head = jnp.where(band & ((kj >= L) | jnp.logical_not(first_span)), 0.0, -jnp.inf)

    def rows(start, count, stride):
        return pl.ds(start, count, stride=stride) if stride > 1 else pl.ds(start, count)

    for branch, (window, r) in enumerate(DIL_PATTERNS):
        units = _dil_units(r)
        for g0 in range(0, len(units), DIL_GROUP):
            group = units[g0:g0 + DIL_GROUP]
            vvs = []
            for u, (res, jb) in enumerate(group):
                q = qf_ref[rows(res + r * L * jb, L, r), :].astype(BF16)
                k0 = SPAN + res + r * L * (jb - 1)
                kk = kf_ref[rows(k0, 2 * L, r), :].astype(BF16)
                vv = vf_ref[rows(k0, 2 * L, r), :]
                for sub in range(2):
                    mine = low if sub == 0 else jnp.logical_not(low)
                    s_ref[2 * u + sub] = _dot_nt(jnp.where(mine, q, jnp.zeros_like(q)), kk)
                    vvs.append(jnp.where(mine, vv, 1.0).astype(BF16))
            ms = []
            for u, (res, jb) in enumerate(group):
                bias = bias_head if jb == 0 else bias_band
                for sub in range(2):
                    t = s_ref[2 * u + sub] + bias
                    m = jnp.max(t, axis=-1, keepdims=True)
                    p_ref[2 * u + sub] = jnp.exp(t - m).astype(BF16)
                    ms.append(m)
            for u, (res, jb) in enumerate(group):
                halves = []
                for sub in range(2):
                    nd = _dot(p_ref[2 * u + sub], vvs[2 * u + sub])
                    halves.append((nd, pltpu.roll(nd, HEAD_DIM, 1), ms[2 * u + sub]))
                (n0, d0, m0), (n1, d1, m1) = halves
                dst = rows(res + r * L * jb, L, r)
                o_ref[branch, dst, :] = jnp.where(low, n0 / d0, n1 / d1)
                lse_ref[branch, dst, :] = jnp.where(low, m0 + jnp.log(d0), m1 + jnp.log(d1))

    for piece in range(SPAN // DIL_MERGE_ROWS):
        rs = slice(piece * DIL_MERGE_ROWS, (piece + 1) * DIL_MERGE_ROWS)
        a, b, c = lse_ref[0, rs, :], lse_ref[1, rs, :], lse_ref[2, rs, :]
        m = jnp.maximum(jnp.maximum(a, b), c)
        ea, eb, ec = jnp.exp(a - m), jnp.exp(b - m), jnp.exp(c - m)
        mix = (ea * o_ref[0, rs, :] + eb * o_ref[1, rs, :] + ec * o_ref[2, rs, :]) / (ea + eb + ec)
        out_ref[rs, :] = mix.astype(out_ref.dtype)


def _dilated_fused(q, k, v, B, S):
    T, W = q.shape
    SPAN, L = DIL_SPAN, DIL_BLOCK
    for window, r in DIL_PATTERNS:
        assert window // r == L and SPAN % (L * r) == 0 and L * r <= SPAN
    assert S % SPAN == 0
    ns = S // SPAN
    PW = 2 * HEAD_DIM
    own = lambda b, t, hp: (b * ns + t, hp)
    prev = lambda b, t, hp: (b * ns + jnp.maximum(t - 1, 0), hp)
    blk = (SPAN, PW)
    return pl.pallas_call(
        _dilated_fused_kernel,
        grid=(B, ns, W // PW),
        in_specs=[pl.BlockSpec(blk, own), pl.BlockSpec(blk, prev), pl.BlockSpec(blk, own),
                  pl.BlockSpec(blk, prev), pl.BlockSpec(blk, own)],
        out_specs=pl.BlockSpec(blk, own),
        out_shape=jax.ShapeDtypeStruct((T, W), BF16),
        scratch_shapes=[
            pltpu.VMEM((SPAN, PW), F32),
            pltpu.VMEM((2 * SPAN, PW), F32),
            pltpu.VMEM((2 * SPAN, PW), F32),
            pltpu.VMEM((2 * DIL_GROUP, L, 2 * L), F32),
            pltpu.VMEM((2 * DIL_GROUP, L, 2 * L), BF16),
            pltpu.VMEM((len(DIL_PATTERNS), SPAN, PW), F32),
            pltpu.VMEM((len(DIL_PATTERNS), SPAN, PW), F32),
        ],
        compiler_params=_params("parallel", "parallel", "parallel"),
        name="dilated",
    )(q, k, k, v, v)


def _mlp_kernel(x_ref, ret_ref, dsa_ref, dil_ref, wout_ref, nw_ref, wup_ref, wdown_ref, o_ref):
    mixed = (_dot(ret_ref[...], wout_ref[0:RET_W, :])
             + _dot(dsa_ref[...], wout_ref[RET_W:RET_W + DSA_W, :])
             + _dot(dil_ref[...], wout_ref[RET_W + DSA_W:, :]))
    x = x_ref[...] + mixed
    h = _rms(x, nw_ref[...]).astype(BF16)
    ff = None
    for c in range(wup_ref.shape[1] // FF_CHUNK):
        cs = slice(c * FF_CHUNK, (c + 1) * FF_CHUNK)
        u = jnp.maximum(_dot(h, wup_ref[:, cs]), 0.0)
        part = _dot((u * u).astype(BF16), wdown_ref[cs, :])
        ff = part if ff is None else ff + part
    o_ref[...] = x + ff


def _mlp(x2, ret_o, dsa_o, dil_o, w_out, nw, w_up, w_down):
    T, D = x2.shape
    tm = ROW_TILE
    row = lambda i: (i, 0)
    const = lambda i: (0, 0)
    return pl.pallas_call(
        _mlp_kernel,
        grid=(T // tm,),
        in_specs=[
            pl.BlockSpec((tm, D), row),
            pl.BlockSpec((tm, RET_W), row),
            pl.BlockSpec((tm, DSA_W), row),
            pl.BlockSpec((tm, DIL_W), row),
            _resident(w_out.shape, const),
            _resident((1, D), const),
            _resident(w_up.shape, const),
            _resident(w_down.shape, const),
        ],
        out_specs=pl.BlockSpec((tm, D), row),
        out_shape=jax.ShapeDtypeStruct((T, D), F32),
        compiler_params=_params("parallel"),
        name="outproj_mlp",
    )(x2, ret_o, dsa_o, dil_o, w_out, nw, w_up, w_down)


def _final_norm_kernel(x_ref, w_ref, o_ref):
    o_ref[...] = _rms(x_ref[...], w_ref[...])


def _final_norm(x2, w):
    T, D = x2.shape
    tm = ROW_TILE
    return pl.pallas_call(
        _final_norm_kernel,
        grid=(T // tm,),
        in_specs=[pl.BlockSpec((tm, D), lambda i: (i, 0)), _resident((1, D), lambda i: (0, 0))],
        out_specs=pl.BlockSpec((tm, D), lambda i: (i, 0)),
        out_shape=jax.ShapeDtypeStruct((T, D), F32),
        compiler_params=_params("parallel"),
        name="final_norm",
    )(x2, w)


def _split_w_in(w_in):
    ret_cols = 4 * RET_W
    o = ret_cols
    dsa_q = w_in[:, o:o + DSA_W]; o += DSA_W
    c_kv = w_in[:, o:o + KV_LATENT]; o += KV_LATENT
    q_idx = w_in[:, o:o + IDX_HEADS * IDX_DIM]; o += IDX_HEADS * IDX_DIM
    k_idx = w_in[:, o:o + IDX_DIM]; o += IDX_DIM
    w_idx = w_in[:, o:o + IDX_HEADS]; o += IDX_HEADS
    dil = w_in[:, o:]
    w_main = jnp.concatenate([w_in[:, :ret_cols], dil, c_kv, k_idx], axis=1).astype(BF16)
    pad = jnp.zeros((w_in.shape[0], WIDX_ROWS - IDX_HEADS), w_in.dtype)
    w_t = jnp.concatenate([q_idx, dsa_q, w_idx, pad], axis=1).T.astype(BF16)
    return w_main, w_t


def kernel(x, attn_norm_w, w_in, ret_norm_w, dsa_kv_norm_w, dsa_w_uk, dsa_w_uv, w_out, mlp_norm_w, w_up, w_down, final_norm_w):
    B, S, D = x.shape
    depth = w_in.shape[0]
    assert S % ROW_TILE == 0 and S % DSA_KC == 0 and S % RET_CHUNK == 0
    tables = _retention_tables(S)
    x2 = x.reshape(B * S, D)
    for layer in range(depth):
        w_main, w_t = _split_w_in(w_in[layer])
        ret_cols, dq, dk, dv, ckv, kidx, qidxT, dsaqT, widxT = _inproj(
            x2, attn_norm_w[layer][None, :], w_main, w_t, B, S)
        ret_o = _retention(ret_cols, tables, ret_norm_w[layer].reshape(1, RET_W), B, S)
        k, vT = _dsa_kv(ckv, dsa_kv_norm_w[layer][None, :], dsa_w_uk[layer].astype(BF16),
                        dsa_w_uv[layer].T.astype(BF16))
        dsa_o = _dsa(qidxT, dsaqT, widxT, kidx, k, vT, B, S)
        dil_o = _dilated_fused(dq, dk, dv, B, S)
        x2 = _mlp(x2, ret_o, dsa_o, dil_o, w_out[layer].astype(BF16), mlp_norm_w[layer][None, :],
                  w_up[layer].astype(BF16), w_down[layer].astype(BF16))
    return _final_norm(x2, final_norm_w[None, :]).reshape(B, S, D)
```

```python
import functools
import math

import numpy as np
import jax
import jax.numpy as jnp
from jax import lax
from jax.experimental import pallas as pl
from jax.experimental.pallas import tpu as pltpu

F32 = jnp.float32
BF16 = jnp.bfloat16
I32 = jnp.int32
I16 = jnp.int16

HEAD_DIM = 64
RET_HEADS = 4
DSA_HEADS = 4
DIL_HEADS = 8
RET_W = RET_HEADS * HEAD_DIM
DSA_W = DSA_HEADS * HEAD_DIM
DIL_W = DIL_HEADS * HEAD_DIM
RET_CHUNK = 128
RET_ROPE_BASE = 10000.0
KV_LATENT = 128
IDX_HEADS = 8
IDX_DIM = 64
TOPK_MAX = 256
DIL_PATTERNS = ((128, 1), (512, 4), (2048, 16))
DIL_BLOCK = 128
DIL_SPAN = 2048
DIL_GROUP = 4
DIL_MERGE_ROWS = 256
NORM_EPS = 1e-6

V7X_VMEM_LIMIT_BYTES = 56 * 1024 * 1024

ROW_TILE = 512
DSA_QB = 256
DSA_KC = 256
DSA_SUB = 64
DSA_SCORE_ROWS = 128
DSA_IDX_GROUP = 4
FF_CHUNK = 1024

WIDX_ROWS = 16

INT_MIN = -(2 ** 31)
KEY_NEG_INF = int(np.int32(np.uint32(0xFF800000) ^ np.uint32(0x7FFFFFFF)))
MASKED_SCORE = -1e30


def _params(*semantics):
    return pltpu.CompilerParams(dimension_semantics=semantics, vmem_limit_bytes=V7X_VMEM_LIMIT_BYTES)


def _resident(shape, index_map):
    return pl.BlockSpec(shape, index_map, pipeline_mode=pl.Buffered(1))


def _dot(a, b):
    return jnp.dot(a, b, preferred_element_type=F32)


def _dot_nt(a, b):
    return lax.dot_general(a, b, (((1,), (1,)), ((), ())), preferred_element_type=F32)


def _rms(x, w):
    return x * lax.rsqrt(jnp.mean(x * x, axis=-1, keepdims=True) + NORM_EPS) * w


def _inproj_kernel(x_ref, nw_ref, w_ref, wt_ref, ret_ref, dq_ref, dk_ref, dv_ref, ckv_ref, kidx_ref,
                   qidxT_ref, dsaqT_ref, widxT_ref):
    h = _rms(x_ref[...], nw_ref[...]).astype(BF16)
    o = 0
    for ref in (ret_ref, dq_ref, dk_ref, dv_ref, ckv_ref, kidx_ref):
        n = ref.shape[-1]
        ref[...] = _dot(h, w_ref[:, o:o + n]).astype(ref.dtype)
        o += n
    o = 0
    for ref in (qidxT_ref, dsaqT_ref, widxT_ref):
        n = ref.shape[-2]
        ref[0] = _dot_nt(wt_ref[o:o + n, :], h).astype(ref.dtype)
        o += n


def _inproj(x2, nw, w_main, w_t, B, S):
    T, D = x2.shape
    tm = ROW_TILE
    nt = S // tm
    widx_rows = w_t.shape[0] - IDX_HEADS * IDX_DIM - DSA_W
    row = lambda i: (i, 0)
    tr = lambda i: (i // nt, 0, i % nt)
    out_shape = (
        jax.ShapeDtypeStruct((T, 4 * RET_W), F32),
        jax.ShapeDtypeStruct((T, DIL_W), BF16),
        jax.ShapeDtypeStruct((T, DIL_W), BF16),
        jax.ShapeDtypeStruct((T, DIL_W), BF16),
        jax.ShapeDtypeStruct((T, KV_LATENT), F32),
        jax.ShapeDtypeStruct((T, IDX_DIM), BF16),
        jax.ShapeDtypeStruct((B, IDX_HEADS * IDX_DIM, S), BF16),
        jax.ShapeDtypeStruct((B, DSA_W, S), BF16),
        jax.ShapeDtypeStruct((B, widx_rows, S), F32),
    )
    out_specs = (
        pl.BlockSpec((tm, 4 * RET_W), row),
        pl.BlockSpec((tm, DIL_W), row),
        pl.BlockSpec((tm, DIL_W), row),
        pl.BlockSpec((tm, DIL_W), row),
        pl.BlockSpec((tm, KV_LATENT), row),
        pl.BlockSpec((tm, IDX_DIM), row),
        pl.BlockSpec((1, IDX_HEADS * IDX_DIM, tm), tr),
        pl.BlockSpec((1, DSA_W, tm), tr),
        pl.BlockSpec((1, widx_rows, tm), tr),
    )
    return pl.pallas_call(
        _inproj_kernel,
        grid=(T // tm,),
        in_specs=[
            pl.BlockSpec((tm, D), row),
            _resident((1, D), lambda i: (0, 0)),
            _resident(w_main.shape, lambda i: (0, 0)),
            _resident(w_t.shape, lambda i: (0, 0)),
        ],
        out_specs=out_specs,
        out_shape=out_shape,
        compiler_params=_params("parallel"),
        name="inproj",
    )(x2, nw, w_main, w_t)


def _retention_tables(S):
    half = HEAD_DIM // 2
    inv = RET_ROPE_BASE ** (-jnp.arange(half, dtype=F32) / half)
    ang = jnp.arange(S, dtype=F32)[:, None] * inv[None, :]
    cos, sin = jnp.cos(ang), jnp.sin(ang)
    cos_h = jnp.concatenate([cos, cos], axis=-1)
    sin_h = jnp.concatenate([-sin, sin], axis=-1)
    cosf = jnp.tile(cos_h, (1, RET_HEADS))
    sinf = jnp.tile(sin_h, (1, RET_HEADS))
    C = RET_CHUNK
    log_g = np.log(1.0 - 2.0 ** (-5.0 - np.arange(RET_HEADS, dtype=np.float64)))
    i = np.arange(C, dtype=np.float64)
    diff = i[:, None] - i[None, :]
    decay = np.exp(np.maximum(diff, 0.0)[None] * log_g[:, None, None]) * (diff >= 0)[None]
    zeta = np.exp((C - 1.0 - i)[None, :] * log_g[:, None])
    xi = np.exp((i + 1.0)[None, :] * log_g[:, None])
    g_chunk = np.exp(C * log_g)
    rep = lambda t: np.repeat(t.T, HEAD_DIM, axis=1)
    g_rows = np.repeat(g_chunk, HEAD_DIM)[:, None] * np.ones((1, RET_W))
    return (cosf, sinf, jnp.asarray(decay, F32), jnp.asarray(rep(zeta), F32), jnp.asarray(rep(xi), F32),
            jnp.asarray(g_rows, F32))


def _retention_kernel(cols_ref, cos_ref, sin_ref, decay_ref, zeta_ref, xi_ref, grow_ref, nw_ref, o_ref, state_ref):
    W = RET_W

    @pl.when(pl.program_id(1) == 0)
    def _():
        state_ref[...] = jnp.zeros_like(state_ref)

    lane = lax.broadcasted_iota(I32, (1, W), 1)
    first_half = (lane % HEAD_DIM) < (HEAD_DIM // 2)
    cosf, sinf = cos_ref[...], sin_ref[...]

    def rot(t):
        partner = jnp.where(first_half, pltpu.roll(t, W - HEAD_DIM // 2, 1), pltpu.roll(t, HEAD_DIM // 2, 1))
        return t * cosf + partner * sinf

    q = rot(cols_ref[:, 0:W])
    k = rot(cols_ref[:, W:2 * W]) * (HEAD_DIM ** -0.5)
    v = cols_ref[:, 2 * W:3 * W].astype(BF16)
    g = cols_ref[:, 3 * W:4 * W]

    head_of_lane = lane // HEAD_DIM
    kb = k.astype(BF16)
    inner = jnp.zeros((RET_CHUNK, W), F32)
    for h in range(RET_HEADS):
        sel = head_of_lane == h
        qh = jnp.where(sel, q, 0.0).astype(BF16)
        a = (_dot_nt(qh, kb) * decay_ref[h]).astype(BF16)
        inner = inner + jnp.where(sel, _dot(a, v), 0.0)

    state = state_ref[...]
    cross = _dot(q.astype(BF16), state.astype(BF16)) * xi_ref[...]
    o = inner + cross

    kzT = (k * zeta_ref[...]).T.astype(BF16)
    kv = _dot(kzT, v)
    r_head = lax.broadcasted_iota(I32, (W, W), 0) // HEAD_DIM
    c_head = lax.broadcasted_iota(I32, (W, W), 1) // HEAD_DIM
    state_ref[...] = grow_ref[...] * state + jnp.where(r_head == c_head, kv, 0.0)

    mu = jnp.zeros_like(o)
    for h in range(RET_HEADS):
        sel = head_of_lane == h
        mu = mu + jnp.where(sel, jnp.sum(jnp.where(sel, o, 0.0), axis=-1, keepdims=True), 0.0)
    mu = mu * (1.0 / HEAD_DIM)
    d = o - mu
    var = jnp.zeros_like(o)
    for h in range(RET_HEADS):
        sel = head_of_lane == h
        var = var + jnp.where(sel, jnp.sum(jnp.where(sel, d * d, 0.0), axis=-1, keepdims=True), 0.0)
    var = var * (1.0 / HEAD_DIM)
    y = d * lax.rsqrt(var + NORM_EPS) * nw_ref[...]
    o_ref[...] = (jax.nn.silu(g) * y).astype(o_ref.dtype)


def _retention(ret_cols, tables, norm_w, B, S):
    T = ret_cols.shape[0]
    C, W = RET_CHUNK, RET_W
    n = S // C
    cosf, sinf, decay, zeta, xi, g_rows = tables
    tok = lambda b, j: (b * n + j, 0)
    pos = lambda b, j: (j, 0)
    const2 = lambda b, j: (0, 0)
    return pl.pallas_call(
        _retention_kernel,
        grid=(B, n),
        in_specs=[
            pl.BlockSpec((C, 4 * W), tok),
            pl.BlockSpec((C, W), pos),
            pl.BlockSpec((C, W), pos),
            _resident(decay.shape, lambda b, j: (0, 0, 0)),
            _resident((C, W), const2),
            _resident((C, W), const2),
            _resident((W, W), const2),
            _resident((1, W), const2),
        ],
        out_specs=pl.BlockSpec((C, W), tok),
        out_shape=jax.ShapeDtypeStruct((T, W), BF16),
        scratch_shapes=[pltpu.VMEM((W, W), F32)],
        compiler_params=_params("parallel", "arbitrary"),
        name="retention",
    )(ret_cols, cosf, sinf, decay, zeta, xi, g_rows, norm_w)


def _dsa_kv_kernel(ckv_ref, nw_ref, wuk_ref, wuvT_ref, k_ref, vT_ref):
    c = _rms(ckv_ref[...], nw_ref[...]).astype(BF16)
    k_ref[...] = _dot(c, wuk_ref[...]).astype(k_ref.dtype)
    vT_ref[0] = _dot_nt(wuvT_ref[...], c).astype(vT_ref.dtype)


def _dsa_kv(ckv, nw, w_uk, w_uvT):
    T = ckv.shape[0]
    tm = DSA_KC
    return pl.pallas_call(
        _dsa_kv_kernel,
        grid=(T // tm,),
        in_specs=[
            pl.BlockSpec((tm, KV_LATENT), lambda i: (i, 0)),
            _resident((1, KV_LATENT), lambda i: (0, 0)),
            _resident(w_uk.shape, lambda i: (0, 0)),
            _resident(w_uvT.shape, lambda i: (0, 0)),
        ],
        out_specs=(pl.BlockSpec((tm, DSA_W), lambda i: (i, 0)),
                   pl.BlockSpec((1, DSA_W, tm), lambda i: (i, 0, 0))),
        out_shape=(jax.ShapeDtypeStruct((T, DSA_W), BF16),
                   jax.ShapeDtypeStruct((T // tm, DSA_W, tm), BF16)),
        compiler_params=_params("parallel"),
        name="dsa_kv",
    )(ckv, nw, w_uk, w_uvT)


def _sortable(x):
    b = pltpu.bitcast(x, I32)
    return b ^ ((b >> 31) & 0x7FFFFFFF)


def _dsa_kernel(qidxT_ref, dsaqT_ref, widxT_ref, kidx_ref, k_ref, vT_ref, o_ref,
                key_ref, hi_ref, lo_ref, digit_ref, cut_ref, qh_ref, s_ref, bias_ref, p_ref, acc_ref,
                *, n_sel, S):
    QB, KC, SUB = DSA_QB, DSA_KC, DSA_SUB
    n_sub = KC // SUB
    i = pl.program_id(1)
    n_chunks = i + 1
    qpos = i * QB + lax.broadcasted_iota(I32, (1, QB), 1)
    krow = lax.broadcasted_iota(I32, (SUB, 1), 0)
    w_scale = IDX_HEADS ** -0.5 * IDX_DIM ** -0.5
    widxT = widxT_ref[0] * w_scale

    def score_chunk(c, carry):
        off = pl.multiple_of(c * (2 * KC), 2 * KC)
        for sub in range(2 * KC // DSA_SCORE_ROWS):
            r0 = off + sub * DSA_SCORE_ROWS
            kc = kidx_ref[pl.ds(r0, DSA_SCORE_ROWS), :]
            acc = jnp.zeros((DSA_SCORE_ROWS, QB), F32)
            for g in range(0, IDX_HEADS, DSA_IDX_GROUP):
                logits = [_dot(kc, qidxT_ref[0, h * IDX_DIM:(h + 1) * IDX_DIM, :])
                          for h in range(g, g + DSA_IDX_GROUP)]
                for h, logit in zip(range(g, g + DSA_IDX_GROUP), logits):
                    acc = acc + jnp.maximum(logit, 0.0) * widxT[h:h + 1, :]
            for part in range(DSA_SCORE_ROWS // SUB):
                a = acc[part * SUB:(part + 1) * SUB, :]
                a = jnp.where(a == 0.0, 0.0, a)
                score = jnp.where(r0 + part * SUB + krow <= qpos, a, -jnp.inf)
                key = _sortable(score)
                rows = pl.ds(r0 + part * SUB, SUB)
                key_ref[rows, :] = key
                hi_ref[rows, :] = (key >> 16).astype(I16)
                lo_ref[rows, :] = ((key & 0xFFFF) - 0x8000).astype(I16)
        return carry

    lax.fori_loop(0, (n_chunks + 1) // 2, score_chunk, 0)

    def count(pred):
        def body(c, cnt):
            off = pl.multiple_of(c * KC, KC)
            for sub in range(n_sub):
                r0 = off + sub * SUB
                cnt = cnt + jnp.where(pred(key_ref[pl.ds(r0, SUB), :], r0 + krow), 1.0, 0.0)
            return cnt
        cnt = lax.fori_loop(0, n_chunks, body, jnp.zeros((SUB, QB), F32))
        return jnp.sum(cnt, axis=0, keepdims=True)

    one_b, zero_b = jnp.ones((SUB, QB), BF16), jnp.zeros((SUB, QB), BF16)

    def count_packed(ref, pred):
        def body(c, cnt):
            off = pl.multiple_of(c * KC, KC)
            for sub in range(n_sub):
                cnt = cnt + jnp.where(pred(ref[pl.ds(off + sub * SUB, SUB), :]), one_b, zero_b)
            return cnt
        cnt = lax.fori_loop(0, n_chunks, body, zero_b)
        return jnp.sum(cnt.astype(F32), axis=0, keepdims=True)

    def signed_digit(u):
        return (u - 0x8000).astype(I16)

    def search_digit(ref, n_above, n_ge_zero):
        def step(t, carry):
            d, n_ge = carry
            cand = d | jnp.left_shift(jnp.int32(1), 15 - t)
            cand_s = signed_digit(cand)
            cnt = n_above + count_packed(ref, lambda x: x >= cand_s)
            keep = cnt >= n_sel
            return jnp.where(keep, cand, d), jnp.where(keep, cnt, n_ge)
        return lax.fori_loop(0, 16, step, (jnp.zeros((1, QB), I32), n_ge_zero))

    stored = jnp.full((1, QB), (n_chunks * KC).astype(F32), F32)
    u_hi, n_ge = search_digit(hi_ref, 0.0, stored)
    thr_hi = signed_digit(u_hi)
    n_above = count_packed(hi_ref, lambda x: x > thr_hi)
    lowest = jnp.full((SUB, QB), -0x8000, I16)

    def low_digits_of_ties(c, carry):
        off = pl.multiple_of(c * KC, KC)
        for sub in range(n_sub):
            rows = pl.ds(off + sub * SUB, SUB)
            digit_ref[rows, :] = jnp.where(hi_ref[rows, :] == thr_hi, lo_ref[rows, :], lowest)
        return carry

    lax.fori_loop(0, n_chunks, low_digits_of_ties, 0)
    u_lo, n_ge = search_digit(digit_ref, n_above, n_ge)
    thr = ((u_hi - 0x8000) << 16) | u_lo

    needs_cut = (n_ge > n_sel) & (thr > KEY_NEG_INF)
    any_ties = jnp.max(jnp.where(needs_cut, 1.0, 0.0)) > 0.0
    cut_ref[...] = jnp.full((1, QB), S, I32)

    @pl.when(any_ties)
    def _():
        tie_take = n_sel - count(lambda key, _: key > thr)

        def cut_step(t, cpos):
            cand = cpos | jnp.left_shift(jnp.int32(1), (S - 1).bit_length() - 1 - t)
            before = count(lambda key, kp: (key == thr) & (kp < cand))
            return jnp.where(before < tie_take, cand, cpos)
        cpos = lax.fori_loop(0, (S - 1).bit_length(), cut_step, jnp.zeros((1, QB), I32))
        cut_ref[...] = jnp.where(needs_cut, cpos, S)

    cut = cut_ref[...]

    row_head = lax.broadcasted_iota(I32, (DSA_W, 1), 0) // HEAD_DIM
    dsaqT = dsaqT_ref[0] * (HEAD_DIM ** -0.5)
    for h in range(DSA_HEADS):
        qh_ref[h] = jnp.where(row_head == h, dsaqT, 0.0).astype(BF16)
    acc_ref[...] = jnp.zeros_like(acc_ref)
    p_ref[...] = jnp.zeros_like(p_ref)

    thr_floor = jnp.maximum(thr, KEY_NEG_INF + 1)

    def write_bias(slot, c, ties):
        off = pl.multiple_of(c * KC, KC)
        for sub in range(n_sub):
            r0 = off + sub * SUB
            key = key_ref[pl.ds(r0, SUB), :]
            if ties:
                sel = ((key > thr) | ((key == thr) & (r0 + krow <= cut))) & (key > KEY_NEG_INF)
            else:
                sel = key >= thr_floor
            bias_ref[slot, sub * SUB:(sub + 1) * SUB, :] = jnp.where(sel, 0.0, MASKED_SCORE)

    def issue_scores(slot, c):
        kch = k_ref[pl.ds(pl.multiple_of(c * KC, KC), KC), :]
        for h in range(DSA_HEADS):
            s_ref[slot, h] = _dot(kch, qh_ref[h])

    def pv_update(slot, c, alphas):
        vch = vT_ref[c]
        pvs = [_dot(vch[h * HEAD_DIM:(h + 1) * HEAD_DIM, :], p_ref[slot, h]) for h in range(DSA_HEADS)]
        for h in range(DSA_HEADS):
            acc_ref[h] = alphas[h] * acc_ref[h] + pvs[h]

    def softmax_chunk(slot, ms, ls):
        new_m, new_l, alphas = [], [], []
        for h in range(DSA_HEADS):
            mx = jnp.full((SUB, QB), MASKED_SCORE, F32)
            for sub in range(n_sub):
                rows = slice(sub * SUB, (sub + 1) * SUB)
                t = s_ref[slot, h, rows, :] + bias_ref[slot, rows, :]
                s_ref[slot, h, rows, :] = t
                mx = jnp.maximum(mx, t)
            m_new = jnp.maximum(ms[h], jnp.max(mx, axis=0, keepdims=True))
            alpha = jnp.exp(ms[h] - m_new)
            psum = jnp.zeros((SUB, QB), F32)
            for sub in range(n_sub):
                rows = slice(sub * SUB, (sub + 1) * SUB)
                p = jnp.exp(s_ref[slot, h, rows, :] - m_new)
                psum = psum + p
                p_ref[slot, h, rows, :] = p.astype(BF16)
            new_m.append(m_new)
            new_l.append(alpha * ls[h] + jnp.sum(psum, axis=0, keepdims=True))
            alphas.append(alpha)
        return tuple(new_m), tuple(new_l), tuple(alphas)

    def attend(ties):
        def attend_pair(c2, carry):
            ms, ls, prev_alphas = carry
            ca, cb = 2 * c2, 2 * c2 + 1
            write_bias(0, ca, ties)
            write_bias(1, cb, ties)
            pv_update(1, jnp.maximum(ca - 1, 0), prev_alphas)
            issue_scores(0, ca)
            issue_scores(1, cb)
            ms, ls, alphas_a = softmax_chunk(0, ms, ls)
            pv_update(0, ca, alphas_a)
            ms, ls, alphas_b = softmax_chunk(1, ms, ls)
            return ms, ls, alphas_b

        init = (tuple(jnp.full((1, QB), MASKED_SCORE, F32) for _ in range(DSA_HEADS)),
                tuple(jnp.zeros((1, QB), F32) for _ in range(DSA_HEADS)),
                tuple(jnp.ones((1, QB), F32) for _ in range(DSA_HEADS)))
        n_pairs = (n_chunks + 1) // 2
        _, ls, last_alphas = lax.fori_loop(0, n_pairs, attend_pair, init)
        pv_update(1, 2 * n_pairs - 1, last_alphas)
        for h in range(DSA_HEADS):
            acc_ref[h] = acc_ref[h] / ls[h]

    pl.when(any_ties)(lambda: attend(True))
    pl.when(jnp.logical_not(any_ties))(lambda: attend(False))
    o_ref[...] = acc_ref[...].reshape(DSA_W, QB).T.astype(o_ref.dtype)


def _dsa(qidxT, dsaqT, widxT, kidx, k, vT, B, S):
    QB, KC = DSA_QB, DSA_KC
    assert QB == KC and S % (2 * KC) == 0
    n_sel = min(TOPK_MAX, S // 4)
    assert n_sel <= KC, "the first causal chunk must be able to hold every selected key"
    assert S // DSA_SUB <= 256, "packed counts are exact only up to 256 adds per accumulator lane"
    nq = S // QB
    nkc = S // KC
    n_sel = min(TOPK_MAX, S // 4)
    T = B * S
    kern = functools.partial(_dsa_kernel, n_sel=n_sel, S=S)
    return pl.pallas_call(
        kern,
        grid=(B, nq),
        in_specs=[
            pl.BlockSpec((1, IDX_HEADS * IDX_DIM, QB), lambda b, i: (b, 0, i)),
            pl.BlockSpec((1, DSA_W, QB), lambda b, i: (b, 0, i)),
            pl.BlockSpec((1, widxT.shape[1], QB), lambda b, i: (b, 0, i)),
            _resident((S, IDX_DIM), lambda b, i: (b, 0)),
            _resident((S, DSA_W), lambda b, i: (b, 0)),
            _resident((nkc, DSA_W, KC), lambda b, i: (b, 0, 0)),
        ],
        out_specs=pl.BlockSpec((QB, DSA_W), lambda b, i: (b * nq + i, 0)),
        out_shape=jax.ShapeDtypeStruct((T, DSA_W), BF16),
        scratch_shapes=[
            pltpu.VMEM((S, QB), I32),
            pltpu.VMEM((S, QB), I16),
            pltpu.VMEM((S, QB), I16),
            pltpu.VMEM((S, QB), I16),
            pltpu.VMEM((1, QB), I32),
            pltpu.VMEM((DSA_HEADS, DSA_W, QB), BF16),
            pltpu.VMEM((2, DSA_HEADS, KC, QB), F32),
            pltpu.VMEM((2, KC, QB), F32),
            pltpu.VMEM((2, DSA_HEADS, KC, QB), BF16),
            pltpu.VMEM((DSA_HEADS, HEAD_DIM, QB), F32),
        ],
        compiler_params=_params("parallel", "arbitrary"),
        name="dsa",
    )(qidxT, dsaqT, widxT, kidx, k, vT)


def _dil_units(dilation):
    blocks = DIL_SPAN // (DIL_BLOCK * dilation)
    return [(res, jb) for res in range(dilation) for jb in range(blocks)]


def _dilated_fused_kernel(q_ref, kp_ref, kc_ref, vp_ref, vc_ref, out_ref,
                          qf_ref, kf_ref, vf_ref, s_ref, p_ref, o_ref, lse_ref):
    L, SPAN = DIL_BLOCK, DIL_SPAN
    first_span = pl.program_id(1) == 0
    qf_ref[...] = q_ref[...].astype(F32) * (HEAD_DIM ** -0.5)
    kf_ref[0:SPAN, :] = kp_ref[...].astype(F32)
    kf_ref[SPAN:, :] = kc_ref[...].astype(F32)
    vf_ref[0:SPAN, :] = vp_ref[...].astype(F32)
    vf_ref[SPAN:, :] = vc_ref[...].astype(F32)

    lane = lax.broadcasted_iota(I32, (1, 2 * HEAD_DIM), 1)
    low = lane < HEAD_DIM
    qi = lax.broadcasted_iota(I32, (L, 2 * L), 0)
    kj = lax.broadcasted_iota(I32, (L, 2 * L), 1)
    dist = L + qi - kj
    band = (dist >= 0) & (dist <= L)
    bias_band = jnp.where(band, 0.0, -jnp.inf)
    bias_head = jnp.where(band & ((kj >= L) | jnp.logical_not(first_span)), 0.0, -jnp.inf)

    def rows(start, count, stride):
        return pl.ds(start, count, stride=stride) if stride > 1 else pl.ds(start, count)

    for branch, (window, r) in enumerate(DIL_PATTERNS):
        units = _dil_units(r)
        for g0 in range(0, len(units), DIL_GROUP):
            group = units[g0:g0 + DIL_GROUP]
            vvs = []
            for u, (res, jb) in enumerate(group):
                q = qf_ref[rows(res + r * L * jb, L, r), :].astype(BF16)
                k0 = SPAN + res + r * L * (jb - 1)
                kk = kf_ref[rows(k0, 2 * L, r), :].astype(BF16)
                vv = vf_ref[rows(k0, 2 * L, r), :]
                for sub in range(2):
                    mine = low if sub == 0 else jnp.logical_not(low)
                    s_ref[2 * u + sub] = _dot_nt(jnp.where(mine, q, jnp.zeros_like(q)), kk)
                    vvs.append(jnp.where(mine, vv, 1.0).astype(BF16))
            ms = []
            for u, (res, jb) in enumerate(group):
                bias = bias_head if jb == 0 else bias_band
                for sub in range(2):
                    t = s_ref[2 * u + sub] + bias
                    m = jnp.max(t, axis=-1, keepdims=True)
                    p_ref[2 * u + sub] = jnp.exp(t - m).astype(BF16)
                    ms.append(m)
            for u, (res, jb) in enumerate(group):
                halves = []
                for sub in range(2):
                    nd = _dot(p_ref[2 * u + sub], vvs[2 * u + sub])
                    halves.append((nd, pltpu.roll(nd, HEAD_DIM, 1), ms[2 * u + sub]))
                (n0, d0, m0), (n1, d1, m1) = halves
                dst = rows(res + r * L * jb, L, r)
                o_ref[branch, dst, :] = jnp.where(low, n0 / d0, n1 / d1)
                lse_ref[branch, dst, :] = jnp.where(low, m0 + jnp.log(d0), m1 + jnp.log(d1))

    for piece in range(SPAN // DIL_MERGE_ROWS):
        rs = slice(piece * DIL_MERGE_ROWS, (piece + 1) * DIL_MERGE_ROWS)
        a, b, c = lse_ref[0, rs, :], lse_ref[1, rs, :], lse_ref[2, rs, :]
        m = jnp.maximum(jnp.maximum(a, b), c)
        ea, eb, ec = jnp.exp(a - m), jnp.exp(b - m), jnp.exp(c - m)
        mix = (ea * o_ref[0, rs, :] + eb * o_ref[1, rs, :] + ec * o_ref[2, rs, :]) / (ea + eb + ec)
        out_ref[rs, :] = mix.astype(out_ref.dtype)


def _dilated_fused(q, k, v, B, S):
    T, W = q.shape
    SPAN, L = DIL_SPAN, DIL_BLOCK
    for window, r in DIL_PATTERNS:
        assert window // r == L and SPAN % (L * r) == 0 and L * r <= SPAN
    assert S % SPAN == 0
    ns = S // SPAN
    PW = 2 * HEAD_DIM
    own = lambda b, t, hp: (b * ns + t, hp)
    prev = lambda b, t, hp: (b * ns + jnp.maximum(t - 1, 0), hp)
    blk = (SPAN, PW)
    return pl.pallas_call(
        _dilated_fused_kernel,
        grid=(B, ns, W // PW),
        in_specs=[pl.BlockSpec(blk, own), pl.BlockSpec(blk, prev), pl.BlockSpec(blk, own),
                  pl.BlockSpec(blk, prev), pl.BlockSpec(blk, own)],
        out_specs=pl.BlockSpec(blk, own),
        out_shape=jax.ShapeDtypeStruct((T, W), BF16),
        scratch_shapes=[
            pltpu.VMEM((SPAN, PW), F32),
            pltpu.VMEM((2 * SPAN, PW), F32),
            pltpu.VMEM((2 * SPAN, PW), F32),
            pltpu.VMEM((2 * DIL_GROUP, L, 2 * L), F32),
            pltpu.VMEM((2 * DIL_GROUP, L, 2 * L), BF16),
            pltpu.VMEM((len(DIL_PATTERNS), SPAN, PW), F32),
            pltpu.VMEM((len(DIL_PATTERNS), SPAN, PW), F32),
        ],
        compiler_params=_params("parallel", "parallel", "parallel"),
        name="dilated",
    )(q, k, k, v, v)


def _mlp_kernel(x_ref, ret_ref, dsa_ref, dil_ref, wout_ref, nw_ref, wup_ref, wdown_ref, o_ref):
    mixed = (_dot(ret_ref[...], wout_ref[0:RET_W, :])
             + _dot(dsa_ref[...], wout_ref[RET_W:RET_W + DSA_W, :])
             + _dot(dil_ref[...], wout_ref[RET_W + DSA_W:, :]))
    x = x_ref[...] + mixed
    h = _rms(x, nw_ref[...]).astype(BF16)
    ff = None
    for c in range(wup_ref.shape[1] // FF_CHUNK):
        cs = slice(c * FF_CHUNK, (c + 1) * FF_CHUNK)
        u = jnp.maximum(_dot(h, wup_ref[:, cs]), 0.0)
        part = _dot((u * u).astype(BF16), wdown_ref[cs, :])
        ff = part if ff is None else ff + part
    o_ref[...] = x + ff


def _mlp(x2, ret_o, dsa_o, dil_o, w_out, nw, w_up, w_down):
    T, D = x2.shape
    tm = ROW_TILE
    row = lambda i: (i, 0)
    const = lambda i: (0, 0)
    return pl.pallas_call(
        _mlp_kernel,
        grid=(T // tm,),
        in_specs=[
            pl.BlockSpec((tm, D), row),
            pl.BlockSpec((tm, RET_W), row),
            pl.BlockSpec((tm, DSA_W), row),
            pl.BlockSpec((tm, DIL_W), row),
            _resident(w_out.shape, const),
            _resident((1, D), const),
            _resident(w_up.shape, const),
            _resident(w_down.shape, const),
        ],
        out_specs=pl.BlockSpec((tm, D), row),
        out_shape=jax.ShapeDtypeStruct((T, D), F32),
        compiler_params=_params("parallel"),
        name="outproj_mlp",
    )(x2, ret_o, dsa_o, dil_o, w_out, nw, w_up, w_down)


def _final_norm_kernel(x_ref, w_ref, o_ref):
    o_ref[...] = _rms(x_ref[...], w_ref[...])


def _final_norm(x2, w):
    T, D = x2.shape
    tm = ROW_TILE
    return pl.pallas_call(
        _final_norm_kernel,
        grid=(T // tm,),
        in_specs=[pl.BlockSpec((tm, D), lambda i: (i, 0)), _resident((1, D), lambda i: (0, 0))],
        out_specs=pl.BlockSpec((tm, D), lambda i: (i, 0)),
        out_shape=jax.ShapeDtypeStruct((T, D), F32),
        compiler_params=_params("parallel"),
        name="final_norm",
    )(x2, w)


def _split_w_in(w_in):
    ret_cols = 4 * RET_W
    o = ret_cols
    dsa_q = w_in[:, o:o + DSA_W]; o += DSA_W
    c_kv = w_in[:, o:o + KV_LATENT]; o += KV_LATENT
    q_idx = w_in[:, o:o + IDX_HEADS * IDX_DIM]; o += IDX_HEADS * IDX_DIM
    k_idx = w_in[:, o:o + IDX_DIM]; o += IDX_DIM
    w_idx = w_in[:, o:o + IDX_HEADS]; o += IDX_HEADS
    dil = w_in[:, o:]
    w_main = jnp.concatenate([w_in[:, :ret_cols], dil, c_kv, k_idx], axis=1).astype(BF16)
    pad = jnp.zeros((w_in.shape[0], WIDX_ROWS - IDX_HEADS), w_in.dtype)
    w_t = jnp.concatenate([q_idx, dsa_q, w_idx, pad], axis=1).T.astype(BF16)
    return w_main, w_t


def kernel(x, attn_norm_w, w_in, ret_norm_w, dsa_kv_norm_w, dsa_w_uk, dsa_w_uv, w_out, mlp_norm_w, w_up, w_down, final_norm_w):
    B, S, D = x.shape
    depth = w_in.shape[0]
    assert S % ROW_TILE == 0 and S % DSA_KC == 0 and S % RET_CHUNK == 0
    tables = _retention_tables(S)
    x2 = x.reshape(B * S, D)
    for layer in range(depth):
        w_main, w_t = _split_w_in(w_in[layer])
        ret_cols, dq, dk, dv, ckv, kidx, qidxT, dsaqT, widxT = _inproj(
            x2, attn_norm_w[layer][None, :], w_main, w_t, B, S)
        ret_o = _retention(ret_cols, tables, ret_norm_w[layer].reshape(1, RET_W), B, S)
        k, vT = _dsa_kv(ckv, dsa_kv_norm_w[layer][None, :], dsa_w_uk[layer].astype(BF16),
                        dsa_w_uv[layer].T.astype(BF16))
        dsa_o = _dsa(qidxT, dsaqT, widxT, kidx, k, vT, B, S)
        dil_o = _dilated_fused(dq, dk, dv, B, S)
        x2 = _mlp(x2, ret_o, dsa_o, dil_o, w_out[layer].astype(BF16), mlp_norm_w[layer][None, :],
                  w_up[layer].astype(BF16), w_down[layer].astype(BF16))
    return _final_norm(x2, final_norm_w[None, :]).reshape(B, S, D)
```

```python
import functools
import math

import numpy as np
import jax
import jax.numpy as jnp
from jax import lax
from jax.experimental import pallas as pl
from jax.experimental.pallas import tpu as pltpu

F32 = jnp.float32
BF16 = jnp.bfloat16
I32 = jnp.int32
I16 = jnp.int16

HEAD_DIM = 64
RET_HEADS = 4
DSA_HEADS = 4
DIL_HEADS = 8
RET_W = RET_HEADS * HEAD_DIM
DSA_W = DSA_HEADS * HEAD_DIM
DIL_W = DIL_HEADS * HEAD_DIM
RET_CHUNK = 128
RET_ROPE_BASE = 10000.0
KV_LATENT = 128
IDX_HEADS = 8
IDX_DIM = 64
TOPK_MAX = 256
DIL_PATTERNS = ((128, 1), (512, 4), (2048, 16))
DIL_BLOCK = 128
DIL_SPAN = 2048
DIL_GROUP = 4
DIL_MERGE_ROWS = 256
NORM_EPS = 1e-6

V7X_VMEM_LIMIT_BYTES = 56 * 1024 * 1024

ROW_TILE = 512
DSA_QB = 256
DSA_KC = 256
DSA_SUB = 64
DSA_SCORE_ROWS = 128
DSA_IDX_GROUP = 4
FF_CHUNK = 1024

WIDX_ROWS = 16

INT_MIN = -(2 ** 31)
KEY_NEG_INF = int(np.int32(np.uint32(0xFF800000) ^ np.uint32(0x7FFFFFFF)))
MASKED_SCORE = -1e30


def _params(*semantics):
    return pltpu.CompilerParams(dimension_semantics=semantics, vmem_limit_bytes=V7X_VMEM_LIMIT_BYTES)


def _resident(shape, index_map):
    return pl.BlockSpec(shape, index_map, pipeline_mode=pl.Buffered(1))


def _dot(a, b):
    return jnp.dot(a, b, preferred_element_type=F32)


def _dot_nt(a, b):
    return lax.dot_general(a, b, (((1,), (1,)), ((), ())), preferred_element_type=F32)


def _rms(x, w):
    return x * lax.rsqrt(jnp.mean(x * x, axis=-1, keepdims=True) + NORM_EPS) * w


def _inproj_kernel(x_ref, nw_ref, w_ref, wt_ref, ret_ref, dq_ref, dk_ref, dv_ref, ckv_ref, kidx_ref,
                   qidxT_ref, dsaqT_ref, widxT_ref):
    h = _rms(x_ref[...], nw_ref[...]).astype(BF16)
    o = 0
    for ref in (ret_ref, dq_ref, dk_ref, dv_ref, ckv_ref, kidx_ref):
        n = ref.shape[-1]
        ref[...] = _dot(h, w_ref[:, o:o + n]).astype(ref.dtype)
        o += n
    o = 0
    for ref in (qidxT_ref, dsaqT_ref, widxT_ref):
        n = ref.shape[-2]
        ref[0] = _dot_nt(wt_ref[o:o + n, :], h).astype(ref.dtype)
        o += n


def _inproj(x2, nw, w_main, w_t, B, S):
    T, D = x2.shape
    tm = ROW_TILE
    nt = S // tm
    widx_rows = w_t.shape[0] - IDX_HEADS * IDX_DIM - DSA_W
    row = lambda i: (i, 0)
    tr = lambda i: (i // nt, 0, i % nt)
    out_shape = (
        jax.ShapeDtypeStruct((T, 4 * RET_W), F32),
        jax.ShapeDtypeStruct((T, DIL_W), BF16),
        jax.ShapeDtypeStruct((T, DIL_W), BF16),
        jax.ShapeDtypeStruct((T, DIL_W), BF16),
        jax.ShapeDtypeStruct((T, KV_LATENT), F32),
        jax.ShapeDtypeStruct((T, IDX_DIM), BF16),
        jax.ShapeDtypeStruct((B, IDX_HEADS * IDX_DIM, S), BF16),
        jax.ShapeDtypeStruct((B, DSA_W, S), BF16),
        jax.ShapeDtypeStruct((B, widx_rows, S), F32),
    )
    out_specs = (
        pl.BlockSpec((tm, 4 * RET_W), row),
        pl.BlockSpec((tm, DIL_W), row),
        pl.BlockSpec((tm, DIL_W), row),
        pl.BlockSpec((tm, DIL_W), row),
        pl.BlockSpec((tm, KV_LATENT), row),
        pl.BlockSpec((tm, IDX_DIM), row),
        pl.BlockSpec((1, IDX_HEADS * IDX_DIM, tm), tr),
        pl.BlockSpec((1, DSA_W, tm), tr),
        pl.BlockSpec((1, widx_rows, tm), tr),
    )
    return pl.pallas_call(
        _inproj_kernel,
        grid=(T // tm,),
        in_specs=[
            pl.BlockSpec((tm, D), row),
            _resident((1, D), lambda i: (0, 0)),
            _resident(w_main.shape, lambda i: (0, 0)),
            _resident(w_t.shape, lambda i: (0, 0)),
        ],
        out_specs=out_specs,
        out_shape=out_shape,
        compiler_params=_params("parallel"),
        name="inproj",
    )(x2, nw, w_main, w_t)


def _retention_tables(S):
    half = HEAD_DIM // 2
    inv = RET_ROPE_BASE ** (-jnp.arange(half, dtype=F32) / half)
    ang = jnp.arange(S, dtype=F32)[:, None] * inv[None, :]
    cos, sin = jnp.cos(ang), jnp.sin(ang)
    cos_h = jnp.concatenate([cos, cos], axis=-1)
    sin_h = jnp.concatenate([-sin, sin], axis=-1)
    cosf = jnp.tile(cos_h, (1, RET_HEADS))
    sinf = jnp.tile(sin_h, (1, RET_HEADS))
    C = RET_CHUNK
    log_g = np.log(1.0 - 2.0 ** (-5.0 - np.arange(RET_HEADS, dtype=np.float64)))
    i = np.arange(C, dtype=np.float64)
    diff = i[:, None] - i[None, :]
    decay = np.exp(np.maximum(diff, 0.0)[None] * log_g[:, None, None]) * (diff >= 0)[None]
    zeta = np.exp((C - 1.0 - i)[None, :] * log_g[:, None])
    xi = np.exp((i + 1.0)[None, :] * log_g[:, None])
    g_chunk = np.exp(C * log_g)
    rep = lambda t: np.repeat(t.T, HEAD_DIM, axis=1)
    g_rows = np.repeat(g_chunk, HEAD_DIM)[:, None] * np.ones((1, RET_W))
    return (cosf, sinf, jnp.asarray(decay, F32), jnp.asarray(rep(zeta), F32), jnp.asarray(rep(xi), F32),
            jnp.asarray(g_rows, F32))


def _retention_kernel(cols_ref, cos_ref, sin_ref, decay_ref, zeta_ref, xi_ref, grow_ref, nw_ref, o_ref, state_ref):
    W = RET_W

    @pl.when(pl.program_id(1) == 0)
    def _():
        state_ref[...] = jnp.zeros_like(state_ref)

    lane = lax.broadcasted_iota(I32, (1, W), 1)
    first_half = (lane % HEAD_DIM) < (HEAD_DIM // 2)
    cosf, sinf = cos_ref[...], sin_ref[...]

    def rot(t):
        partner = jnp.where(first_half, pltpu.roll(t, W - HEAD_DIM // 2, 1), pltpu.roll(t, HEAD_DIM // 2, 1))
        return t * cosf + partner * sinf

    q = rot(cols_ref[:, 0:W])
    k = rot(cols_ref[:, W:2 * W]) * (HEAD_DIM ** -0.5)
    v = cols_ref[:, 2 * W:3 * W].astype(BF16)
    g = cols_ref[:, 3 * W:4 * W]

    head_of_lane = lane // HEAD_DIM
    kb = k.astype(BF16)
    inner = jnp.zeros((RET_CHUNK, W), F32)
    for h in range(RET_HEADS):
        sel = head_of_lane == h
        qh = jnp.where(sel, q, 0.0).astype(BF16)
        a = (_dot_nt(qh, kb) * decay_ref[h]).astype(BF16)
        inner = inner + jnp.where(sel, _dot(a, v), 0.0)

    state = state_ref[...]
    cross = _dot(q.astype(BF16), state.astype(BF16)) * xi_ref[...]
    o = inner + cross

    kzT = (k * zeta_ref[...]).T.astype(BF16)
    kv = _dot(kzT, v)
    r_head = lax.broadcasted_iota(I32, (W, W), 0) // HEAD_DIM
    c_head = lax.broadcasted_iota(I32, (W, W), 1) // HEAD_DIM
    state_ref[...] = grow_ref[...] * state + jnp.where(r_head == c_head, kv, 0.0)

    mu = jnp.zeros_like(o)
    for h in range(RET_HEADS):
        sel = head_of_lane == h
        mu = mu + jnp.where(sel, jnp.sum(jnp.where(sel, o, 0.0), axis=-1, keepdims=True), 0.0)
    mu = mu * (1.0 / HEAD_DIM)
    d = o - mu
    var = jnp.zeros_like(o)
    for h in range(RET_HEADS):
        sel = head_of_lane == h
        var = var + jnp.where(sel, jnp.sum(jnp.where(sel, d * d, 0.0), axis=-1, keepdims=True), 0.0)
    var = var * (1.0 / HEAD_DIM)
    y = d * lax.rsqrt(var + NORM_EPS) * nw_ref[...]
    o_ref[...] = (jax.nn.silu(g) * y).astype(o_ref.dtype)


def _retention(ret_cols, tables, norm_w, B, S):
    T = ret_cols.shape[0]
    C, W = RET_CHUNK, RET_W
    n = S // C
    cosf, sinf, decay, zeta, xi, g_rows = tables
    tok = lambda b, j: (b * n + j, 0)
    pos = lambda b, j: (j, 0)
    const2 = lambda b, j: (0, 0)
    return pl.pallas_call(
        _retention_kernel,
        grid=(B, n),
        in_specs=[
            pl.BlockSpec((C, 4 * W), tok),
            pl.BlockSpec((C, W), pos),
            pl.BlockSpec((C, W), pos),
            _resident(decay.shape, lambda b, j: (0, 0, 0)),
            _resident((C, W), const2),
            _resident((C, W), const2),
            _resident((W, W), const2),
            _resident((1, W), const2),
        ],
        out_specs=pl.BlockSpec((C, W), tok),
        out_shape=jax.ShapeDtypeStruct((T, W), BF16),
        scratch_shapes=[pltpu.VMEM((W, W), F32)],
        compiler_params=_params("parallel", "arbitrary"),
        name="retention",
    )(ret_cols, cosf, sinf, decay, zeta, xi, g_rows, norm_w)


def _dsa_kv_kernel(ckv_ref, nw_ref, wuk_ref, wuvT_ref, k_ref, vT_ref):
    c = _rms(ckv_ref[...], nw_ref[...]).astype(BF16)
    k_ref[...] = _dot(c, wuk_ref[...]).astype(k_ref.dtype)
    vT_ref[0] = _dot_nt(wuvT_ref[...], c).astype(vT_ref.dtype)


def _dsa_kv(ckv, nw, w_uk, w_uvT):
    T = ckv.shape[0]
    tm = DSA_KC
    return pl.pallas_call(
        _dsa_kv_kernel,
        grid=(T // tm,),
        in_specs=[
            pl.BlockSpec((tm, KV_LATENT), lambda i: (i, 0)),
            _resident((1, KV_LATENT), lambda i: (0, 0)),
            _resident(w_uk.shape, lambda i: (0, 0)),
            _resident(w_uvT.shape, lambda i: (0, 0)),
        ],
        out_specs=(pl.BlockSpec((tm, DSA_W), lambda i: (i, 0)),
                   pl.BlockSpec((1, DSA_W, tm), lambda i: (i, 0, 0))),
        out_shape=(jax.ShapeDtypeStruct((T, DSA_W), BF16),
                   jax.ShapeDtypeStruct((T // tm, DSA_W, tm), BF16)),
        compiler_params=_params("parallel"),
        name="dsa_kv",
    )(ckv, nw, w_uk, w_uvT)


def _sortable(x):
    b = pltpu.bitcast(x, I32)
    return b ^ ((b >> 31) & 0x7FFFFFFF)


def _dsa_kernel(qidxT_ref, dsaqT_ref, widxT_ref, kidx_ref, k_ref, vT_ref, o_ref,
                key_ref, hi_ref, lo_ref, digit_ref, cut_ref, qh_ref, s_ref, bias_ref, p_ref, acc_ref,
                *, n_sel, S):
    QB, KC, SUB = DSA_QB, DSA_KC, DSA_SUB
    n_sub = KC // SUB
    i = pl.program_id(1)
    n_chunks = i + 1
    n_pairs = (n_chunks + 1) // 2
    qpos = i * QB + lax.broadcasted_iota(I32, (1, QB), 1)
    krow = lax.broadcasted_iota(I32, (SUB, 1), 0)
    w_scale = IDX_HEADS ** -0.5 * IDX_DIM ** -0.5
    widxT = widxT_ref[0] * w_scale

    def score_chunk(c, carry):
        off = pl.multiple_of(c * (2 * KC), 2 * KC)
        for sub in range(2 * KC // DSA_SCORE_ROWS):
            r0 = off + sub * DSA_SCORE_ROWS
            kc = kidx_ref[pl.ds(r0, DSA_SCORE_ROWS), :]
            acc = jnp.zeros((DSA_SCORE_ROWS, QB), F32)
            for g in range(0, IDX_HEADS, DSA_IDX_GROUP):
                logits = [_dot(kc, qidxT_ref[0, h * IDX_DIM:(h + 1) * IDX_DIM, :])
                          for h in range(g, g + DSA_IDX_GROUP)]
                for h, logit in zip(range(g, g + DSA_IDX_GROUP), logits):
                    acc = acc + jnp.maximum(logit, 0.0) * widxT[h:h + 1, :]
            for part in range(DSA_SCORE_ROWS // SUB):
                a = acc[part * SUB:(part + 1) * SUB, :]
                a = jnp.where(a == 0.0, 0.0, a)
                score = jnp.where(r0 + part * SUB + krow <= qpos, a, -jnp.inf)
                key = _sortable(score)
                rows = pl.ds(r0 + part * SUB, SUB)
                key_ref[rows, :] = key
                hi_ref[rows, :] = (key >> 16).astype(I16)
                lo_ref[rows, :] = ((key & 0xFFFF) - 0x8000).astype(I16)
        return carry

    lax.fori_loop(0, n_pairs, score_chunk, 0)

    def count(pred):
        def body(c, cnt):
            off = pl.multiple_of(c * KC, KC)
            for sub in range(n_sub):
                r0 = off + sub * SUB
                cnt = cnt + jnp.where(pred(key_ref[pl.ds(r0, SUB), :], r0 + krow), 1.0, 0.0)
            return cnt
        cnt = lax.fori_loop(0, n_chunks, body, jnp.zeros((SUB, QB), F32))
        return jnp.sum(cnt, axis=0, keepdims=True)

    one_b, zero_b = jnp.ones((SUB, QB), BF16), jnp.zeros((SUB, QB), BF16)

    def count_packed(ref, pred):
        def body(c2, cnt):
            off = pl.multiple_of(c2 * (2 * KC), 2 * KC)
            for sub in range(2 * n_sub):
                cnt = cnt + jnp.where(pred(ref[pl.ds(off + sub * SUB, SUB), :]), one_b, zero_b)
            return cnt
        cnt = lax.fori_loop(0, n_pairs, body, zero_b)
        return jnp.sum(cnt.astype(F32), axis=0, keepdims=True)

    def signed_digit(u):
        return (u - 0x8000).astype(I16)

    def search_digit(ref, n_above, n_ge_zero):
        def step(t, carry):
            d, n_ge = carry
            cand = d | jnp.left_shift(jnp.int32(1), 15 - t)
            cand_s = signed_digit(cand)
            cnt = n_above + count_packed(ref, lambda x: x >= cand_s)
            keep = cnt >= n_sel
            return jnp.where(keep, cand, d), jnp.where(keep, cnt, n_ge)
        return lax.fori_loop(0, 16, step, (jnp.zeros((1, QB), I32), n_ge_zero))

    stored = jnp.full((1, QB), (n_pairs * (2 * KC)).astype(F32), F32)
    u_hi, n_ge = search_digit(hi_ref, 0.0, stored)
    thr_hi = signed_digit(u_hi)
    n_above = count_packed(hi_ref, lambda x: x > thr_hi)
    lowest = jnp.full((SUB, QB), -0x8000, I16)

    def low_digits_of_ties(c2, carry):
        off = pl.multiple_of(c2 * (2 * KC), 2 * KC)
        for sub in range(2 * n_sub):
            rows = pl.ds(off + sub * SUB, SUB)
            digit_ref[rows, :] = jnp.where(hi_ref[rows, :] == thr_hi, lo_ref[rows, :], lowest)
        return carry

    lax.fori_loop(0, n_pairs, low_digits_of_ties, 0)
    u_lo, n_ge = search_digit(digit_ref, n_above, n_ge)
    thr = ((u_hi - 0x8000) << 16) | u_lo

    needs_cut = (n_ge > n_sel) & (thr > KEY_NEG_INF)
    any_ties = jnp.max(jnp.where(needs_cut, 1.0, 0.0)) > 0.0
    cut_ref[...] = jnp.full((1, QB), S, I32)

    @pl.when(any_ties)
    def _():
        tie_take = n_sel - count(lambda key, _: key > thr)

        def cut_step(t, cpos):
            cand = cpos | jnp.left_shift(jnp.int32(1), (S - 1).bit_length() - 1 - t)
            before = count(lambda key, kp: (key == thr) & (kp < cand))
            return jnp.where(before < tie_take, cand, cpos)
        cpos = lax.fori_loop(0, (S - 1).bit_length(), cut_step, jnp.zeros((1, QB), I32))
        cut_ref[...] = jnp.where(needs_cut, cpos, S)

    cut = cut_ref[...]

    row_head = lax.broadcasted_iota(I32, (DSA_W, 1), 0) // HEAD_DIM
    dsaqT = dsaqT_ref[0] * (HEAD_DIM ** -0.5)
    for h in range(DSA_HEADS):
        qh_ref[h] = jnp.where(row_head == h, dsaqT, 0.0).astype(BF16)
    acc_ref[...] = jnp.zeros_like(acc_ref)
    p_ref[...] = jnp.zeros_like(p_ref)

    thr_floor = jnp.maximum(thr, KEY_NEG_INF + 1)

    def write_bias(slot, c, ties):
        off = pl.multiple_of(c * KC, KC)
        for sub in range(n_sub):
            r0 = off + sub * SUB
            key = key_ref[pl.ds(r0, SUB), :]
            if ties:
                sel = ((key > thr) | ((key == thr) & (r0 + krow <= cut))) & (key > KEY_NEG_INF)
            else:
                sel = key >= thr_floor
            bias_ref[slot, sub * SUB:(sub + 1) * SUB, :] = jnp.where(sel, 0.0, MASKED_SCORE)

    def issue_scores(slot, c):
        kch = k_ref[pl.ds(pl.multiple_of(c * KC, KC), KC), :]
        for h in range(DSA_HEADS):
            s_ref[slot, h] = _dot(kch, qh_ref[h])

    def pv_update(slot, c, alphas):
        vch = vT_ref[c]
        pvs = [_dot(vch[h * HEAD_DIM:(h + 1) * HEAD_DIM, :], p_ref[slot, h]) for h in range(DSA_HEADS)]
        for h in range(DSA_HEADS):
            acc_ref[h] = alphas[h] * acc_ref[h] + pvs[h]

    def softmax_chunk(slot, ms, ls):
        new_m, new_l, alphas = [], [], []
        for h in range(DSA_HEADS):
            mx = jnp.full((SUB, QB), MASKED_SCORE, F32)
            for sub in range(n_sub):
                rows = slice(sub * SUB, (sub + 1) * SUB)
                t = s_ref[slot, h, rows, :] + bias_ref[slot, rows, :]
                s_ref[slot, h, rows, :] = t
                mx = jnp.maximum(mx, t)
            m_new = jnp.maximum(ms[h], jnp.max(mx, axis=0, keepdims=True))
            alpha = jnp.exp(ms[h] - m_new)
            psum = jnp.zeros((SUB, QB), F32)
            for sub in range(n_sub):
                rows = slice(sub * SUB, (sub + 1) * SUB)
                p = jnp.exp(s_ref[slot, h, rows, :] - m_new)
                psum = psum + p
                p_ref[slot, h, rows, :] = p.astype(BF16)
            new_m.append(m_new)
            new_l.append(alpha * ls[h] + jnp.sum(psum, axis=0, keepdims=True))
            alphas.append(alpha)
        return tuple(new_m), tuple(new_l), tuple(alphas)

    def attend(ties):
        def attend_pair(c2, carry):
            ms, ls, prev_alphas = carry
            ca, cb = 2 * c2, 2 * c2 + 1
            write_bias(0, ca, ties)
            write_bias(1, cb, ties)
            pv_update(1, jnp.maximum(ca - 1, 0), prev_alphas)
            issue_scores(0, ca)
            issue_scores(1, cb)
            ms, ls, alphas_a = softmax_chunk(0, ms, ls)
            pv_update(0, ca, alphas_a)
            ms, ls, alphas_b = softmax_chunk(1, ms, ls)
            return ms, ls, alphas_b

        init = (tuple(jnp.full((1, QB), MASKED_SCORE, F32) for _ in range(DSA_HEADS)),
                tuple(jnp.zeros((1, QB), F32) for _ in range(DSA_HEADS)),
                tuple(jnp.ones((1, QB), F32) for _ in range(DSA_HEADS)))
        _, ls, last_alphas = lax.fori_loop(0, n_pairs, attend_pair, init)
        pv_update(1, 2 * n_pairs - 1, last_alphas)
        for h in range(DSA_HEADS):
            acc_ref[h] = acc_ref[h] / ls[h]

    pl.when(any_ties)(lambda: attend(True))
    pl.when(jnp.logical_not(any_ties))(lambda: attend(False))
    o_ref[...] = acc_ref[...].reshape(DSA_W, QB).T.astype(o_ref.dtype)


def _dsa(qidxT, dsaqT, widxT, kidx, k, vT, B, S):
    QB, KC = DSA_QB, DSA_KC
    assert QB == KC and S % (2 * KC) == 0
    n_sel = min(TOPK_MAX, S // 4)
    assert n_sel <= KC, "the first causal chunk must be able to hold every selected key"
    assert S // DSA_SUB <= 256, "packed counts are exact only up to 256 adds per accumulator lane"
    nq = S // QB
    nkc = S // KC
    n_sel = min(TOPK_MAX, S // 4)
    T = B * S
    kern = functools.partial(_dsa_kernel, n_sel=n_sel, S=S)
    return pl.pallas_call(
        kern,
        grid=(B, nq),
        in_specs=[
            pl.BlockSpec((1, IDX_HEADS * IDX_DIM, QB), lambda b, i: (b, 0, i)),
            pl.BlockSpec((1, DSA_W, QB), lambda b, i: (b, 0, i)),
            pl.BlockSpec((1, widxT.shape[1], QB), lambda b, i: (b, 0, i)),
            _resident((S, IDX_DIM), lambda b, i: (b, 0)),
            _resident((S, DSA_W), lambda b, i: (b, 0)),
            _resident((nkc, DSA_W, KC), lambda b, i: (b, 0, 0)),
        ],
        out_specs=pl.BlockSpec((QB, DSA_W), lambda b, i: (b * nq + i, 0)),
        out_shape=jax.ShapeDtypeStruct((T, DSA_W), BF16),
        scratch_shapes=[
            pltpu.VMEM((S, QB), I32),
            pltpu.VMEM((S, QB), I16),
            pltpu.VMEM((S, QB), I16),
            pltpu.VMEM((S, QB), I16),
            pltpu.VMEM((1, QB), I32),
            pltpu.VMEM((DSA_HEADS, DSA_W, QB), BF16),
            pltpu.VMEM((2, DSA_HEADS, KC, QB), F32),
            pltpu.VMEM((2, KC, QB), F32),
            pltpu.VMEM((2, DSA_HEADS, KC, QB), BF16),
            pltpu.VMEM((DSA_HEADS, HEAD_DIM, QB), F32),
        ],
        compiler_params=_params("parallel", "arbitrary"),
        name="dsa",
    )(qidxT, dsaqT, widxT, kidx, k, vT)


def _dil_units(dilation):
    blocks = DIL_SPAN // (DIL_BLOCK * dilation)
    return [(res, jb) for res in range(dilation) for jb in range(blocks)]


def _dilated_fused_kernel(q_ref, kp_ref, kc_ref, vp_ref, vc_ref, out_ref,
                          qf_ref, kf_ref, vf_ref, s_ref, p_ref, o_ref, lse_ref):
    L, SPAN = DIL_BLOCK, DIL_SPAN
    first_span = pl.program_id(1) == 0
    qf_ref[...] = q_ref[...].astype(F32) * (HEAD_DIM ** -0.5)
    kf_ref[0:SPAN, :] = kp_ref[...].astype(F32)
    kf_ref[SPAN:, :] = kc_ref[...].astype(F32)
    vf_ref[0:SPAN, :] = vp_ref[...].astype(F32)
    vf_ref[SPAN:, :] = vc_ref[...].astype(F32)

    lane = lax.broadcasted_iota(I32, (1, 2 * HEAD_DIM), 1)
    low = lane < HEAD_DIM
    qi = lax.broadcasted_iota(I32, (L, 2 * L), 0)
    kj = lax.broadcasted_iota(I32, (L, 2 * L), 1)
    dist = L + qi - kj
    band = (dist >= 0) & (dist <= L)
    bias_band = jnp.where(band, 0.0, -jnp.inf)
    bias_head = jnp.where(band & ((kj >= L) | jnp.logical_not(first_span)), 0.0, -jnp.inf)

    def rows(start, count, stride):
        return pl.ds(start, count, stride=stride) if stride > 1 else pl.ds(start, count)

    for branch, (window, r) in enumerate(DIL_PATTERNS):
        units = _dil_units(r)
        for g0 in range(0, len(units), DIL_GROUP):
            group = units[g0:g0 + DIL_GROUP]
            vvs = []
            for u, (res, jb) in enumerate(group):
                q = qf_ref[rows(res + r * L * jb, L, r), :].astype(BF16)
                k0 = SPAN + res + r * L * (jb - 1)
                kk = kf_ref[rows(k0, 2 * L, r), :].astype(BF16)
                vv = vf_ref[rows(k0, 2 * L, r), :]
                for sub in range(2):
                    mine = low if sub == 0 else jnp.logical_not(low)
                    s_ref[2 * u + sub] = _dot_nt(jnp.where(mine, q, jnp.zeros_like(q)), kk)
                    vvs.append(jnp.where(mine, vv, 1.0).astype(BF16))
            ms = []
            for u, (res, jb) in enumerate(group):
                bias = bias_head if jb == 0 else bias_band
                for sub in range(2):
                    t = s_ref[2 * u + sub] + bias
                    m = jnp.max(t, axis=-1, keepdims=True)
                    p_ref[2 * u + sub] = jnp.exp(t - m).astype(BF16)
                    ms.append(m)
            for u, (res, jb) in enumerate(group):
                halves = []
                for sub in range(2):
                    nd = _dot(p_ref[2 * u + sub], vvs[2 * u + sub])
                    halves.append((nd, pltpu.roll(nd, HEAD_DIM, 1), ms[2 * u + sub]))
                (n0, d0, m0), (n1, d1, m1) = halves
                dst = rows(res + r * L * jb, L, r)
                o_ref[branch, dst, :] = jnp.where(low, n0 / d0, n1 / d1)
                lse_ref[branch, dst, :] = jnp.where(low, m0 + jnp.log(d0), m1 + jnp.log(d1))

    for piece in range(SPAN // DIL_MERGE_ROWS):
        rs = slice(piece * DIL_MERGE_ROWS, (piece + 1) * DIL_MERGE_ROWS)
        a, b, c = lse_ref[0, rs, :], lse_ref[1, rs, :], lse_ref[2, rs, :]
        m = jnp.maximum(jnp.maximum(a, b), c)
        ea, eb, ec = jnp.exp(a - m), jnp.exp(b - m), jnp.exp(c - m)
        mix = (ea * o_ref[0, rs, :] + eb * o_ref[1, rs, :] + ec * o_ref[2, rs, :]) / (ea + eb + ec)
        out_ref[rs, :] = mix.astype(out_ref.dtype)


def _dilated_fused(q, k, v, B, S):
    T, W = q.shape
    SPAN, L = DIL_SPAN, DIL_BLOCK
    for window, r in DIL_PATTERNS:
        assert window // r == L and SPAN % (L * r) == 0 and L * r <= SPAN
    assert S % SPAN == 0
    ns = S // SPAN
    PW = 2 * HEAD_DIM
    own = lambda b, t, hp: (b * ns + t, hp)
    prev = lambda b, t, hp: (b * ns + jnp.maximum(t - 1, 0), hp)
    blk = (SPAN, PW)
    return pl.pallas_call(
        _dilated_fused_kernel,
        grid=(B, ns, W // PW),
        in_specs=[pl.BlockSpec(blk, own), pl.BlockSpec(blk, prev), pl.BlockSpec(blk, own),
                  pl.BlockSpec(blk, prev), pl.BlockSpec(blk, own)],
        out_specs=pl.BlockSpec(blk, own),
        out_shape=jax.ShapeDtypeStruct((T, W), BF16),
        scratch_shapes=[
            pltpu.VMEM((SPAN, PW), F32),
            pltpu.VMEM((2 * SPAN, PW), F32),
            pltpu.VMEM((2 * SPAN, PW), F32),
            pltpu.VMEM((2 * DIL_GROUP, L, 2 * L), F32),
            pltpu.VMEM((2 * DIL_GROUP, L, 2 * L), BF16),
            pltpu.VMEM((len(DIL_PATTERNS), SPAN, PW), F32),
            pltpu.VMEM((len(DIL_PATTERNS), SPAN, PW), F32),
        ],
        compiler_params=_params("parallel", "parallel", "parallel"),
        name="dilated",
    )(q, k, k, v, v)


def _mlp_kernel(x_ref, ret_ref, dsa_ref, dil_ref, wout_ref, nw_ref, wup_ref, wdown_ref, o_ref):
    mixed = (_dot(ret_ref[...], wout_ref[0:RET_W, :])
             + _dot(dsa_ref[...], wout_ref[RET_W:RET_W + DSA_W, :])
             + _dot(dil_ref[...], wout_ref[RET_W + DSA_W:, :]))
    x = x_ref[...] + mixed
    h = _rms(x, nw_ref[...]).astype(BF16)
    ff = None
    for c in range(wup_ref.shape[1] // FF_CHUNK):
        cs = slice(c * FF_CHUNK, (c + 1) * FF_CHUNK)
        u = jnp.maximum(_dot(h, wup_ref[:, cs]), 0.0)
        part = _dot((u * u).astype(BF16), wdown_ref[cs, :])
        ff = part if ff is None else ff + part
    o_ref[...] = x + ff


def _mlp(x2, ret_o, dsa_o, dil_o, w_out, nw, w_up, w_down):
    T, D = x2.shape
    tm = ROW_TILE
    row = lambda i: (i, 0)
    const = lambda i: (0, 0)
    return pl.pallas_call(
        _mlp_kernel,
        grid=(T // tm,),
        in_specs=[
            pl.BlockSpec((tm, D), row),
            pl.BlockSpec((tm, RET_W), row),
            pl.BlockSpec((tm, DSA_W), row),
            pl.BlockSpec((tm, DIL_W), row),
            _resident(w_out.shape, const),
            _resident((1, D), const),
            _resident(w_up.shape, const),
            _resident(w_down.shape, const),
        ],
        out_specs=pl.BlockSpec((tm, D), row),
        out_shape=jax.ShapeDtypeStruct((T, D), F32),
        compiler_params=_params("parallel"),
        name="outproj_mlp",
    )(x2, ret_o, dsa_o, dil_o, w_out, nw, w_up, w_down)


def _final_norm_kernel(x_ref, w_ref, o_ref):
    o_ref[...] = _rms(x_ref[...], w_ref[...])


def _final_norm(x2, w):
    T, D = x2.shape
    tm = ROW_TILE
    return pl.pallas_call(
        _final_norm_kernel,
        grid=(T // tm,),
        in_specs=[pl.BlockSpec((tm, D), lambda i: (i, 0)), _resident((1, D), lambda i: (0, 0))],
        out_specs=pl.BlockSpec((tm, D), lambda i: (i, 0)),
        out_shape=jax.ShapeDtypeStruct((T, D), F32),
        compiler_params=_params("parallel"),
        name="final_norm",
    )(x2, w)


def _split_w_in(w_in):
    ret_cols = 4 * RET_W
    o = ret_cols
    dsa_q = w_in[:, o:o + DSA_W]; o += DSA_W
    c_kv = w_in[:, o:o + KV_LATENT]; o += KV_LATENT
    q_idx = w_in[:, o:o + IDX_HEADS * IDX_DIM]; o += IDX_HEADS * IDX_DIM
    k_idx = w_in[:, o:o + IDX_DIM]; o += IDX_DIM
    w_idx = w_in[:, o:o + IDX_HEADS]; o += IDX_HEADS
    dil = w_in[:, o:]
    w_main = jnp.concatenate([w_in[:, :ret_cols], dil, c_kv, k_idx], axis=1).astype(BF16)
    pad = jnp.zeros((w_in.shape[0], WIDX_ROWS - IDX_HEADS), w_in.dtype)
    w_t = jnp.concatenate([q_idx, dsa_q, w_idx, pad], axis=1).T.astype(BF16)
    return w_main, w_t


def kernel(x, attn_norm_w, w_in, ret_norm_w, dsa_kv_norm_w, dsa_w_uk, dsa_w_uv, w_out, mlp_norm_w, w_up, w_down, final_norm_w):
    B, S, D = x.shape
    depth = w_in.shape[0]
    assert S % ROW_TILE == 0 and S % DSA_KC == 0 and S % RET_CHUNK == 0
    tables = _retention_tables(S)
    x2 = x.reshape(B * S, D)
    for layer in range(depth):
        w_main, w_t = _split_w_in(w_in[layer])
        ret_cols, dq, dk, dv, ckv, kidx, qidxT, dsaqT, widxT = _inproj(
            x2, attn_norm_w[layer][None, :], w_main, w_t, B, S)
        ret_o = _retention(ret_cols, tables, ret_norm_w[layer].reshape(1, RET_W), B, S)
        k, vT = _dsa_kv(ckv, dsa_kv_norm_w[layer][None, :], dsa_w_uk[layer].astype(BF16),
                        dsa_w_uv[layer].T.astype(BF16))
        dsa_o = _dsa(qidxT, dsaqT, widxT, kidx, k, vT, B, S)
        dil_o = _dilated_fused(dq, dk, dv, B, S)
        x2 = _mlp(x2, ret_o, dsa_o, dil_o, w_out[layer].astype(BF16), mlp_norm_w[layer][None, :],
                  w_up[layer].astype(BF16), w_down[layer].astype(BF16))
    return _final_norm(x2, final_norm_w[None, :]).reshape(B, S, D)
```

```python
import functools
import math

import numpy as np
import jax
import jax.numpy as jnp
from jax import lax
from jax.experimental import pallas as pl
from jax.experimental.pallas import tpu as pltpu

F32 = jnp.float32
BF16 = jnp.bfloat16
I32 = jnp.int32
I16 = jnp.int16

HEAD_DIM = 64
RET_HEADS = 4
DSA_HEADS = 4
DIL_HEADS = 8
RET_W = RET_HEADS * HEAD_DIM
DSA_W = DSA_HEADS * HEAD_DIM
DIL_W = DIL_HEADS * HEAD_DIM
RET_CHUNK = 128
RET_ROPE_BASE = 10000.0
KV_LATENT = 128
IDX_HEADS = 8
IDX_DIM = 64
TOPK_MAX = 256
DIL_PATTERNS = ((128, 1), (512, 4), (2048, 16))
DIL_BLOCK = 128
DIL_SPAN = 2048
DIL_GROUP = 4
DIL_MERGE_ROWS = 256
NORM_EPS = 1e-6

V7X_VMEM_LIMIT_BYTES = 56 * 1024 * 1024

ROW_TILE = 512
DSA_QB = 256
DSA_KC = 256
DSA_SUB = 64
DSA_SCORE_ROWS = 128
DSA_IDX_GROUP = 4
FF_CHUNK = 1024

WIDX_ROWS = 16

INT_MIN = -(2 ** 31)
KEY_NEG_INF = int(np.int32(np.uint32(0xFF800000) ^ np.uint32(0x7FFFFFFF)))
MASKED_SCORE = -1e30


def _params(*semantics):
    return pltpu.CompilerParams(dimension_semantics=semantics, vmem_limit_bytes=V7X_VMEM_LIMIT_BYTES)


def _resident(shape, index_map):
    return pl.BlockSpec(shape, index_map, pipeline_mode=pl.Buffered(1))


def _dot(a, b):
    return jnp.dot(a, b, preferred_element_type=F32)


def _dot_nt(a, b):
    return lax.dot_general(a, b, (((1,), (1,)), ((), ())), preferred_element_type=F32)


def _rms(x, w):
    return x * lax.rsqrt(jnp.mean(x * x, axis=-1, keepdims=True) + NORM_EPS) * w


def _inproj_kernel(x_ref, nw_ref, w_ref, wt_ref, ret_ref, dq_ref, dk_ref, dv_ref, ckv_ref, kidx_ref,
                   qidxT_ref, dsaqT_ref, widxT_ref):
    h = _rms(x_ref[...], nw_ref[...]).astype(BF16)
    o = 0
    for ref in (ret_ref, dq_ref, dk_ref, dv_ref, ckv_ref, kidx_ref):
        n = ref.shape[-1]
        ref[...] = _dot(h, w_ref[:, o:o + n]).astype(ref.dtype)
        o += n
    o = 0
    for ref in (qidxT_ref, dsaqT_ref, widxT_ref):
        n = ref.shape[-2]
        ref[0] = _dot_nt(wt_ref[o:o + n, :], h).astype(ref.dtype)
        o += n


def _inproj(x2, nw, w_main, w_t, B, S):
    T, D = x2.shape
    tm = ROW_TILE
    nt = S // tm
    widx_rows = w_t.shape[0] - IDX_HEADS * IDX_DIM - DSA_W
    row = lambda i: (i, 0)
    tr = lambda i: (i // nt, 0, i % nt)
    out_shape = (
        jax.ShapeDtypeStruct((T, 4 * RET_W), F32),
        jax.ShapeDtypeStruct((T, DIL_W), BF16),
        jax.ShapeDtypeStruct((T, DIL_W), BF16),
        jax.ShapeDtypeStruct((T, DIL_W), BF16),
        jax.ShapeDtypeStruct((T, KV_LATENT), F32),
        jax.ShapeDtypeStruct((T, IDX_DIM), BF16),
        jax.ShapeDtypeStruct((B, IDX_HEADS * IDX_DIM, S), BF16),
        jax.ShapeDtypeStruct((B, DSA_W, S), BF16),
        jax.ShapeDtypeStruct((B, widx_rows, S), F32),
    )
    out_specs = (
        pl.BlockSpec((tm, 4 * RET_W), row),
        pl.BlockSpec((tm, DIL_W), row),
        pl.BlockSpec((tm, DIL_W), row),
        pl.BlockSpec((tm, DIL_W), row),
        pl.BlockSpec((tm, KV_LATENT), row),
        pl.BlockSpec((tm, IDX_DIM), row),
        pl.BlockSpec((1, IDX_HEADS * IDX_DIM, tm), tr),
        pl.BlockSpec((1, DSA_W, tm), tr),
        pl.BlockSpec((1, widx_rows, tm), tr),
    )
    return pl.pallas_call(
        _inproj_kernel,
        grid=(T // tm,),
        in_specs=[
            pl.BlockSpec((tm, D), row),
            _resident((1, D), lambda i: (0, 0)),
            _resident(w_main.shape, lambda i: (0, 0)),
            _resident(w_t.shape, lambda i: (0, 0)),
        ],
        out_specs=out_specs,
        out_shape=out_shape,
        compiler_params=_params("parallel"),
        name="inproj",
    )(x2, nw, w_main, w_t)


def _retention_tables(S):
    half = HEAD_DIM // 2
    inv = RET_ROPE_BASE ** (-jnp.arange(half, dtype=F32) / half)
    ang = jnp.arange(S, dtype=F32)[:, None] * inv[None, :]
    cos, sin = jnp.cos(ang), jnp.sin(ang)
    cos_h = jnp.concatenate([cos, cos], axis=-1)
    sin_h = jnp.concatenate([-sin, sin], axis=-1)
    cosf = jnp.tile(cos_h, (1, RET_HEADS))
    sinf = jnp.tile(sin_h, (1, RET_HEADS))
    C = RET_CHUNK
    log_g = np.log(1.0 - 2.0 ** (-5.0 - np.arange(RET_HEADS, dtype=np.float64)))
    i = np.arange(C, dtype=np.float64)
    diff = i[:, None] - i[None, :]
    decay = np.exp(np.maximum(diff, 0.0)[None] * log_g[:, None, None]) * (diff >= 0)[None]
    zeta = np.exp((C - 1.0 - i)[None, :] * log_g[:, None])
    xi = np.exp((i + 1.0)[None, :] * log_g[:, None])
    g_chunk = np.exp(C * log_g)
    rep = lambda t: np.repeat(t.T, HEAD_DIM, axis=1)
    g_rows = np.repeat(g_chunk, HEAD_DIM)[:, None] * np.ones((1, RET_W))
    return (cosf, sinf, jnp.asarray(decay, F32), jnp.asarray(rep(zeta), F32), jnp.asarray(rep(xi), F32),
            jnp.asarray(g_rows, F32))


def _retention_kernel(cols_ref, cos_ref, sin_ref, decay_ref, zeta_ref, xi_ref, grow_ref, nw_ref, o_ref, state_ref):
    W = RET_W

    @pl.when(pl.program_id(1) == 0)
    def _():
        state_ref[...] = jnp.zeros_like(state_ref)

    lane = lax.broadcasted_iota(I32, (1, W), 1)
    first_half = (lane % HEAD_DIM) < (HEAD_DIM // 2)
    cosf, sinf = cos_ref[...], sin_ref[...]

    def rot(t):
        partner = jnp.where(first_half, pltpu.roll(t, W - HEAD_DIM // 2, 1), pltpu.roll(t, HEAD_DIM // 2, 1))
        return t * cosf + partner * sinf

    q = rot(cols_ref[:, 0:W])
    k = rot(cols_ref[:, W:2 * W]) * (HEAD_DIM ** -0.5)
    v = cols_ref[:, 2 * W:3 * W].astype(BF16)
    g = cols_ref[:, 3 * W:4 * W]

    head_of_lane = lane // HEAD_DIM
    kb = k.astype(BF16)
    inner = jnp.zeros((RET_CHUNK, W), F32)
    for h in range(RET_HEADS):
        sel = head_of_lane == h
        qh = jnp.where(sel, q, 0.0).astype(BF16)
        a = (_dot_nt(qh, kb) * decay_ref[h]).astype(BF16)
        inner = inner + jnp.where(sel, _dot(a, v), 0.0)

    state = state_ref[...]
    cross = _dot(q.astype(BF16), state.astype(BF16)) * xi_ref[...]
    o = inner + cross

    kzT = (k * zeta_ref[...]).T.astype(BF16)
    kv = _dot(kzT, v)
    r_head = lax.broadcasted_iota(I32, (W, W), 0) // HEAD_DIM
    c_head = lax.broadcasted_iota(I32, (W, W), 1) // HEAD_DIM
    state_ref[...] = grow_ref[...] * state + jnp.where(r_head == c_head, kv, 0.0)

    mu = jnp.zeros_like(o)
    for h in range(RET_HEADS):
        sel = head_of_lane == h
        mu = mu + jnp.where(sel, jnp.sum(jnp.where(sel, o, 0.0), axis=-1, keepdims=True), 0.0)
    mu = mu * (1.0 / HEAD_DIM)
    d = o - mu
    var = jnp.zeros_like(o)
    for h in range(RET_HEADS):
        sel = head_of_lane == h
        var = var + jnp.where(sel, jnp.sum(jnp.where(sel, d * d, 0.0), axis=-1, keepdims=True), 0.0)
    var = var * (1.0 / HEAD_DIM)
    y = d * lax.rsqrt(var + NORM_EPS) * nw_ref[...]
    o_ref[...] = (jax.nn.silu(g) * y).astype(o_ref.dtype)


def _retention(ret_cols, tables, norm_w, B, S):
    T = ret_cols.shape[0]
    C, W = RET_CHUNK, RET_W
    n = S // C
    cosf, sinf, decay, zeta, xi, g_rows = tables
    tok = lambda b, j: (b * n + j, 0)
    pos = lambda b, j: (j, 0)
    const2 = lambda b, j: (0, 0)
    return pl.pallas_call(
        _retention_kernel,
        grid=(B, n),
        in_specs=[
            pl.BlockSpec((C, 4 * W), tok),
            pl.BlockSpec((C, W), pos),
            pl.BlockSpec((C, W), pos),
            _resident(decay.shape, lambda b, j: (0, 0, 0)),
            _resident((C, W), const2),
            _resident((C, W), const2),
            _resident((W, W), const2),
            _resident((1, W), const2),
        ],
        out_specs=pl.BlockSpec((C, W), tok),
        out_shape=jax.ShapeDtypeStruct((T, W), BF16),
        scratch_shapes=[pltpu.VMEM((W, W), F32)],
        compiler_params=_params("parallel", "arbitrary"),
        name="retention",
    )(ret_cols, cosf, sinf, decay, zeta, xi, g_rows, norm_w)


def _dsa_kv_kernel(ckv_ref, nw_ref, wuk_ref, wuvT_ref, k_ref, vT_ref):
    c = _rms(ckv_ref[...], nw_ref[...]).astype(BF16)
    k_ref[...] = _dot(c, wuk_ref[...]).astype(k_ref.dtype)
    vT_ref[0] = _dot_nt(wuvT_ref[...], c).astype(vT_ref.dtype)


def _dsa_kv(ckv, nw, w_uk, w_uvT):
    T = ckv.shape[0]
    tm = DSA_KC
    return pl.pallas_call(
        _dsa_kv_kernel,
        grid=(T // tm,),
        in_specs=[
            pl.BlockSpec((tm, KV_LATENT), lambda i: (i, 0)),
            _resident((1, KV_LATENT), lambda i: (0, 0)),
            _resident(w_uk.shape, lambda i: (0, 0)),
            _resident(w_uvT.shape, lambda i: (0, 0)),
        ],
        out_specs=(pl.BlockSpec((tm, DSA_W), lambda i: (i, 0)),
                   pl.BlockSpec((1, DSA_W, tm), lambda i: (i, 0, 0))),
        out_shape=(jax.ShapeDtypeStruct((T, DSA_W), BF16),
                   jax.ShapeDtypeStruct((T // tm, DSA_W, tm), BF16)),
        compiler_params=_params("parallel"),
        name="dsa_kv",
    )(ckv, nw, w_uk, w_uvT)


def _sortable(x):
    b = pltpu.bitcast(x, I32)
    return b ^ ((b >> 31) & 0x7FFFFFFF)


def _dsa_kernel(qidxT_ref, dsaqT_ref, widxT_ref, kidx_ref, k_ref, vT_ref, o_ref,
                key_ref, hi_ref, lo_ref, digit_ref, cut_ref, qh_ref, s_ref, bias_ref, p_ref, acc_ref,
                *, n_sel, S):
    QB, KC, SUB = DSA_QB, DSA_KC, DSA_SUB
    n_sub = KC // SUB
    i = pl.program_id(1)
    n_chunks = i + 1
    n_pairs = (n_chunks + 1) // 2
    qpos = i * QB + lax.broadcasted_iota(I32, (1, QB), 1)
    krow = lax.broadcasted_iota(I32, (SUB, 1), 0)
    w_scale = IDX_HEADS ** -0.5 * IDX_DIM ** -0.5
    widxT = widxT_ref[0] * w_scale

    def score_chunk(c, carry):
        off = pl.multiple_of(c * (2 * KC), 2 * KC)
        for sub in range(2 * KC // DSA_SCORE_ROWS):
            r0 = off + sub * DSA_SCORE_ROWS
            kc = kidx_ref[pl.ds(r0, DSA_SCORE_ROWS), :]
            acc = jnp.zeros((DSA_SCORE_ROWS, QB), F32)
            for g in range(0, IDX_HEADS, DSA_IDX_GROUP):
                logits = [_dot(kc, qidxT_ref[0, h * IDX_DIM:(h + 1) * IDX_DIM, :])
                          for h in range(g, g + DSA_IDX_GROUP)]
                for h, logit in zip(range(g, g + DSA_IDX_GROUP), logits):
                    acc = acc + jnp.maximum(logit, 0.0) * widxT[h:h + 1, :]
            for part in range(DSA_SCORE_ROWS // SUB):
                a = acc[part * SUB:(part + 1) * SUB, :]
                a = jnp.where(a == 0.0, 0.0, a)
                score = jnp.where(r0 + part * SUB + krow <= qpos, a, -jnp.inf)
                key = _sortable(score)
                rows = pl.ds(r0 + part * SUB, SUB)
                key_ref[rows, :] = key
                hi_ref[rows, :] = (key >> 16).astype(I16)
                lo_ref[rows, :] = ((key & 0xFFFF) - 0x8000).astype(I16)
        return carry

    lax.fori_loop(0, n_pairs, score_chunk, 0)

    one_b, zero_b = jnp.ones((SUB, QB), BF16), jnp.zeros((SUB, QB), BF16)

    def count_packed(ref, pred):
        def body(c2, cnt):
            off = pl.multiple_of(c2 * (2 * KC), 2 * KC)
            for sub in range(2 * n_sub):
                cnt = cnt + jnp.where(pred(ref[pl.ds(off + sub * SUB, SUB), :]), one_b, zero_b)
            return cnt
        cnt = lax.fori_loop(0, n_pairs, body, zero_b)
        return jnp.sum(cnt.astype(F32), axis=0, keepdims=True)

    def signed_digit(u):
        return (u - 0x8000).astype(I16)

    def search_digit(ref, n_above, n_ge_zero):
        def step(t, carry):
            d, n_ge = carry
            cand = d | jnp.left_shift(jnp.int32(1), 15 - t)
            cand_s = signed_digit(cand)
            cnt = n_above + count_packed(ref, lambda x: x >= cand_s)
            keep = cnt >= n_sel
            return jnp.where(keep, cand, d), jnp.where(keep, cnt, n_ge)
        return lax.fori_loop(0, 16, step, (jnp.zeros((1, QB), I32), n_ge_zero))

    stored = jnp.full((1, QB), (n_pairs * (2 * KC)).astype(F32), F32)
    u_hi, n_ge = search_digit(hi_ref, 0.0, stored)
    thr_hi = signed_digit(u_hi)
    n_above = count_packed(hi_ref, lambda x: x > thr_hi)
    lowest = jnp.full((SUB, QB), -0x8000, I16)

    def low_digits_of_ties(c2, carry):
        off = pl.multiple_of(c2 * (2 * KC), 2 * KC)
        for sub in range(2 * n_sub):
            rows = pl.ds(off + sub * SUB, SUB)
            digit_ref[rows, :] = jnp.where(hi_ref[rows, :] == thr_hi, lo_ref[rows, :], lowest)
        return carry

    lax.fori_loop(0, n_pairs, low_digits_of_ties, 0)
    u_lo, n_ge = search_digit(digit_ref, n_above, n_ge)
    thr = ((u_hi - 0x8000) << 16) | u_lo

    needs_cut = (n_ge > n_sel) & (thr > KEY_NEG_INF)
    any_ties = jnp.max(jnp.where(needs_cut, 1.0, 0.0)) > 0.0
    cut_ref[...] = jnp.full((1, QB), S, I32)

    @pl.when(any_ties)
    def _():
        thr_lo = signed_digit(u_lo)
        tie_take = n_sel - (n_above + count_packed(digit_ref, lambda x: x > thr_lo))
        far = jnp.full((SUB, QB), 0x7FFF, I16)

        def tied_positions(c2, carry):
            off = pl.multiple_of(c2 * (2 * KC), 2 * KC)
            for sub in range(2 * n_sub):
                r0 = off + sub * SUB
                rows = pl.ds(r0, SUB)
                pos = (r0 + lax.broadcasted_iota(I32, (SUB, QB), 0)).astype(I16)
                tied = (hi_ref[rows, :] == thr_hi) & (lo_ref[rows, :] == thr_lo)
                digit_ref[rows, :] = jnp.where(tied, pos, far)
            return carry

        lax.fori_loop(0, n_pairs, tied_positions, 0)

        def cut_step(t, cpos):
            cand = cpos | jnp.left_shift(jnp.int32(1), (S - 1).bit_length() - 1 - t)
            cand_s = cand.astype(I16)
            before = count_packed(digit_ref, lambda x: x < cand_s)
            return jnp.where(before < tie_take, cand, cpos)
        cpos = lax.fori_loop(0, (S - 1).bit_length(), cut_step, jnp.zeros((1, QB), I32))
        cut_ref[...] = jnp.where(needs_cut, cpos, S)

    cut = cut_ref[...]

    row_head = lax.broadcasted_iota(I32, (DSA_W, 1), 0) // HEAD_DIM
    dsaqT = dsaqT_ref[0] * (HEAD_DIM ** -0.5)
    for h in range(DSA_HEADS):
        qh_ref[h] = jnp.where(row_head == h, dsaqT, 0.0).astype(BF16)
    acc_ref[...] = jnp.zeros_like(acc_ref)
    p_ref[...] = jnp.zeros_like(p_ref)

    thr_floor = jnp.maximum(thr, KEY_NEG_INF + 1)

    tie_cut = jnp.where(thr > KEY_NEG_INF, cut, -1)

    def write_bias(slot, c, ties):
        off = pl.multiple_of(c * KC, KC)
        for sub in range(n_sub):
            r0 = off + sub * SUB
            key = key_ref[pl.ds(r0, SUB), :]
            if ties:
                tied = jnp.where(r0 + krow <= tie_cut, 0.0, MASKED_SCORE)
                bias = jnp.where(key > thr, 0.0, jnp.where(key == thr, tied, MASKED_SCORE))
            else:
                bias = jnp.where(key >= thr_floor, 0.0, MASKED_SCORE)
            bias_ref[slot, sub * SUB:(sub + 1) * SUB, :] = bias

    def issue_scores(slot, c):
        kch = k_ref[pl.ds(pl.multiple_of(c * KC, KC), KC), :]
        for h in range(DSA_HEADS):
            s_ref[slot, h] = _dot(kch, qh_ref[h])

    def pv_update(slot, c, alphas):
        vch = vT_ref[c]
        pvs = [_dot(vch[h * HEAD_DIM:(h + 1) * HEAD_DIM, :], p_ref[slot, h]) for h in range(DSA_HEADS)]
        for h in range(DSA_HEADS):
            acc_ref[h] = alphas[h] * acc_ref[h] + pvs[h]

    def softmax_chunk(slot, ms, ls):
        new_m, new_l, alphas = [], [], []
        for h in range(DSA_HEADS):
            mx = jnp.full((SUB, QB), MASKED_SCORE, F32)
            for sub in range(n_sub):
                rows = slice(sub * SUB, (sub + 1) * SUB)
                t = s_ref[slot, h, rows, :] + bias_ref[slot, rows, :]
                s_ref[slot, h, rows, :] = t
                mx = jnp.maximum(mx, t)
            m_new = jnp.maximum(ms[h], jnp.max(mx, axis=0, keepdims=True))
            alpha = jnp.exp(ms[h] - m_new)
            psum = jnp.zeros((SUB, QB), F32)
            for sub in range(n_sub):
                rows = slice(sub * SUB, (sub + 1) * SUB)
                p = jnp.exp(s_ref[slot, h, rows, :] - m_new)
                psum = psum + p
                p_ref[slot, h, rows, :] = p.astype(BF16)
            new_m.append(m_new)
            new_l.append(alpha * ls[h] + jnp.sum(psum, axis=0, keepdims=True))
            alphas.append(alpha)
        return tuple(new_m), tuple(new_l), tuple(alphas)

    def attend(ties):
        def attend_pair(c2, carry):
            ms, ls, prev_alphas = carry
            ca, cb = 2 * c2, 2 * c2 + 1
            write_bias(0, ca, ties)
            write_bias(1, cb, ties)
            pv_update(1, jnp.maximum(ca - 1, 0), prev_alphas)
            issue_scores(0, ca)
            issue_scores(1, cb)
            ms, ls, alphas_a = softmax_chunk(0, ms, ls)
            pv_update(0, ca, alphas_a)
            ms, ls, alphas_b = softmax_chunk(1, ms, ls)
            return ms, ls, alphas_b

        init = (tuple(jnp.full((1, QB), MASKED_SCORE, F32) for _ in range(DSA_HEADS)),
                tuple(jnp.zeros((1, QB), F32) for _ in range(DSA_HEADS)),
                tuple(jnp.ones((1, QB), F32) for _ in range(DSA_HEADS)))
        _, ls, last_alphas = lax.fori_loop(0, n_pairs, attend_pair, init)
        pv_update(1, 2 * n_pairs - 1, last_alphas)
        for h in range(DSA_HEADS):
            acc_ref[h] = acc_ref[h] / ls[h]

    pl.when(any_ties)(lambda: attend(True))
    pl.when(jnp.logical_not(any_ties))(lambda: attend(False))
    o_ref[...] = acc_ref[...].reshape(DSA_W, QB).T.astype(o_ref.dtype)


def _dsa(qidxT, dsaqT, widxT, kidx, k, vT, B, S):
    QB, KC = DSA_QB, DSA_KC
    assert QB == KC and S % (2 * KC) == 0
    n_sel = min(TOPK_MAX, S // 4)
    assert n_sel <= KC, "the first causal chunk must be able to hold every selected key"
    assert S // DSA_SUB <= 256, "packed counts are exact only up to 256 adds per accumulator lane"
    assert S <= 0x7FFF, "key positions are held as int16 when ties are cut"
    nq = S // QB
    nkc = S // KC
    n_sel = min(TOPK_MAX, S // 4)
    T = B * S
    kern = functools.partial(_dsa_kernel, n_sel=n_sel, S=S)
    return pl.pallas_call(
        kern,
        grid=(B, nq),
        in_specs=[
            pl.BlockSpec((1, IDX_HEADS * IDX_DIM, QB), lambda b, i: (b, 0, i)),
            pl.BlockSpec((1, DSA_W, QB), lambda b, i: (b, 0, i)),
            pl.BlockSpec((1, widxT.shape[1], QB), lambda b, i: (b, 0, i)),
            _resident((S, IDX_DIM), lambda b, i: (b, 0)),
            _resident((S, DSA_W), lambda b, i: (b, 0)),
            _resident((nkc, DSA_W, KC), lambda b, i: (b, 0, 0)),
        ],
        out_specs=pl.BlockSpec((QB, DSA_W), lambda b, i: (b * nq + i, 0)),
        out_shape=jax.ShapeDtypeStruct((T, DSA_W), BF16),
        scratch_shapes=[
            pltpu.VMEM((S, QB), I32),
            pltpu.VMEM((S, QB), I16),
            pltpu.VMEM((S, QB), I16),
            pltpu.VMEM((S, QB), I16),
            pltpu.VMEM((1, QB), I32),
            pltpu.VMEM((DSA_HEADS, DSA_W, QB), BF16),
            pltpu.VMEM((2, DSA_HEADS, KC, QB), F32),
            pltpu.VMEM((2, KC, QB), F32),
            pltpu.VMEM((2, DSA_HEADS, KC, QB), BF16),
            pltpu.VMEM((DSA_HEADS, HEAD_DIM, QB), F32),
        ],
        compiler_params=_params("parallel", "arbitrary"),
        name="dsa",
    )(qidxT, dsaqT, widxT, kidx, k, vT)


def _dil_units(dilation):
    blocks = DIL_SPAN // (DIL_BLOCK * dilation)
    return [(res, jb) for res in range(dilation) for jb in range(blocks)]


def _dilated_fused_kernel(q_ref, kp_ref, kc_ref, vp_ref, vc_ref, out_ref,
                          qf_ref, kf_ref, vf_ref, s_ref, p_ref, o_ref, lse_ref):
    L, SPAN = DIL_BLOCK, DIL_SPAN
    first_span = pl.program_id(1) == 0
    qf_ref[...] = q_ref[...].astype(F32) * (HEAD_DIM ** -0.5)
    kf_ref[0:SPAN, :] = kp_ref[...].astype(F32)
    kf_ref[SPAN:, :] = kc_ref[...].astype(F32)
    vf_ref[0:SPAN, :] = vp_ref[...].astype(F32)
    vf_ref[SPAN:, :] = vc_ref[...].astype(F32)

    lane = lax.broadcasted_iota(I32, (1, 2 * HEAD_DIM), 1)
    low = lane < HEAD_DIM
    qi = lax.broadcasted_iota(I32, (L, 2 * L), 0)
    kj = lax.broadcasted_iota(I32, (L, 2 * L), 1)
    dist = L + qi - kj
    band = (dist >= 0) & (dist <= L)
    bias_band = jnp.where(band, 0.0, -jnp.inf)
    bias_head = jnp.where(band & ((kj >= L) | jnp.logical_not(first_span)), 0.0, -jnp.inf)

    def rows(start, count, stride):
        return pl.ds(start, count, stride=stride) if stride > 1 else pl.ds(start, count)

    for branch, (window, r) in enumerate(DIL_PATTERNS):
        units = _dil_units(r)
        for g0 in range(0, len(units), DIL_GROUP):
            group = units[g0:g0 + DIL_GROUP]
            vvs = []
            for u, (res, jb) in enumerate(group):
                q = qf_ref[rows(res + r * L * jb, L, r), :].astype(BF16)
                k0 = SPAN + res + r * L * (jb - 1)
                kk = kf_ref[rows(k0, 2 * L, r), :].astype(BF16)
                vv = vf_ref[rows(k0, 2 * L, r), :]
                for sub in range(2):
                    mine = low if sub == 0 else jnp.logical_not(low)
                    s_ref[2 * u + sub] = _dot_nt(jnp.where(mine, q, jnp.zeros_like(q)), kk)
                    vvs.append(jnp.where(mine, vv, 1.0).astype(BF16))
            ms = []
            for u, (res, jb) in enumerate(group):
                bias = bias_head if jb == 0 else bias_band
                for sub in range(2):
                    t = s_ref[2 * u + sub] + bias
                    m = jnp.max(t, axis=-1, keepdims=True)
                    p_ref[2 * u + sub] = jnp.exp(t - m).astype(BF16)
                    ms.append(m)
            for u, (res, jb) in enumerate(group):
                halves = []
                for sub in range(2):
                    nd = _dot(p_ref[2 * u + sub], vvs[2 * u + sub])
                    halves.append((nd, pltpu.roll(nd, HEAD_DIM, 1), ms[2 * u + sub]))
                (n0, d0, m0), (n1, d1, m1) = halves
                dst = rows(res + r * L * jb, L, r)
                o_ref[branch, dst, :] = jnp.where(low, n0 / d0, n1 / d1)
                lse_ref[branch, dst, :] = jnp.where(low, m0 + jnp.log(d0), m1 + jnp.log(d1))

    for piece in range(SPAN // DIL_MERGE_ROWS):
        rs = slice(piece * DIL_MERGE_ROWS, (piece + 1) * DIL_MERGE_ROWS)
        a, b, c = lse_ref[0, rs, :], lse_ref[1, rs, :], lse_ref[2, rs, :]
        m = jnp.maximum(jnp.maximum(a, b), c)
        ea, eb, ec = jnp.exp(a - m), jnp.exp(b - m), jnp.exp(c - m)
        mix = (ea * o_ref[0, rs, :] + eb * o_ref[1, rs, :] + ec * o_ref[2, rs, :]) / (ea + eb + ec)
        out_ref[rs, :] = mix.astype(out_ref.dtype)


def _dilated_fused(q, k, v, B, S):
    T, W = q.shape
    SPAN, L = DIL_SPAN, DIL_BLOCK
    for window, r in DIL_PATTERNS:
        assert window // r == L and SPAN % (L * r) == 0 and L * r <= SPAN
    assert S % SPAN == 0
    ns = S // SPAN
    PW = 2 * HEAD_DIM
    own = lambda b, t, hp: (b * ns + t, hp)
    prev = lambda b, t, hp: (b * ns + jnp.maximum(t - 1, 0), hp)
    blk = (SPAN, PW)
    return pl.pallas_call(
        _dilated_fused_kernel,
        grid=(B, ns, W // PW),
        in_specs=[pl.BlockSpec(blk, own), pl.BlockSpec(blk, prev), pl.BlockSpec(blk, own),
                  pl.BlockSpec(blk, prev), pl.BlockSpec(blk, own)],
        out_specs=pl.BlockSpec(blk, own),
        out_shape=jax.ShapeDtypeStruct((T, W), BF16),
        scratch_shapes=[
            pltpu.VMEM((SPAN, PW), F32),
            pltpu.VMEM((2 * SPAN, PW), F32),
            pltpu.VMEM((2 * SPAN, PW), F32),
            pltpu.VMEM((2 * DIL_GROUP, L, 2 * L), F32),
            pltpu.VMEM((2 * DIL_GROUP, L, 2 * L), BF16),
            pltpu.VMEM((len(DIL_PATTERNS), SPAN, PW), F32),
            pltpu.VMEM((len(DIL_PATTERNS), SPAN, PW), F32),
        ],
        compiler_params=_params("parallel", "parallel", "parallel"),
        name="dilated",
    )(q, k, k, v, v)


def _mlp_kernel(x_ref, ret_ref, dsa_ref, dil_ref, wout_ref, nw_ref, wup_ref, wdown_ref, o_ref):
    mixed = (_dot(ret_ref[...], wout_ref[0:RET_W, :])
             + _dot(dsa_ref[...], wout_ref[RET_W:RET_W + DSA_W, :])
             + _dot(dil_ref[...], wout_ref[RET_W + DSA_W:, :]))
    x = x_ref[...] + mixed
    h = _rms(x, nw_ref[...]).astype(BF16)
    ff = None
    for c in range(wup_ref.shape[1] // FF_CHUNK):
        cs = slice(c * FF_CHUNK, (c + 1) * FF_CHUNK)
        u = jnp.maximum(_dot(h, wup_ref[:, cs]), 0.0)
        part = _dot((u * u).astype(BF16), wdown_ref[cs, :])
        ff = part if ff is None else ff + part
    o_ref[...] = x + ff


def _mlp(x2, ret_o, dsa_o, dil_o, w_out, nw, w_up, w_down):
    T, D = x2.shape
    tm = ROW_TILE
    row = lambda i: (i, 0)
    const = lambda i: (0, 0)
    return pl.pallas_call(
        _mlp_kernel,
        grid=(T // tm,),
        in_specs=[
            pl.BlockSpec((tm, D), row),
            pl.BlockSpec((tm, RET_W), row),
            pl.BlockSpec((tm, DSA_W), row),
            pl.BlockSpec((tm, DIL_W), row),
            _resident(w_out.shape, const),
            _resident((1, D), const),
            _resident(w_up.shape, const),
            _resident(w_down.shape, const),
        ],
        out_specs=pl.BlockSpec((tm, D), row),
        out_shape=jax.ShapeDtypeStruct((T, D), F32),
        compiler_params=_params("parallel"),
        name="outproj_mlp",
    )(x2, ret_o, dsa_o, dil_o, w_out, nw, w_up, w_down)


def _final_norm_kernel(x_ref, w_ref, o_ref):
    o_ref[...] = _rms(x_ref[...], w_ref[...])


def _final_norm(x2, w):
    T, D = x2.shape
    tm = ROW_TILE
    return pl.pallas_call(
        _final_norm_kernel,
        grid=(T // tm,),
        in_specs=[pl.BlockSpec((tm, D), lambda i: (i, 0)), _resident((1, D), lambda i: (0, 0))],
        out_specs=pl.BlockSpec((tm, D), lambda i: (i, 0)),
        out_shape=jax.ShapeDtypeStruct((T, D), F32),
        compiler_params=_params("parallel"),
        name="final_norm",
    )(x2, w)


def _split_w_in(w_in):
    ret_cols = 4 * RET_W
    o = ret_cols
    dsa_q = w_in[:, o:o + DSA_W]; o += DSA_W
    c_kv = w_in[:, o:o + KV_LATENT]; o += KV_LATENT
    q_idx = w_in[:, o:o + IDX_HEADS * IDX_DIM]; o += IDX_HEADS * IDX_DIM
    k_idx = w_in[:, o:o + IDX_DIM]; o += IDX_DIM
    w_idx = w_in[:, o:o + IDX_HEADS]; o += IDX_HEADS
    dil = w_in[:, o:]
    w_main = jnp.concatenate([w_in[:, :ret_cols], dil, c_kv, k_idx], axis=1).astype(BF16)
    pad = jnp.zeros((w_in.shape[0], WIDX_ROWS - IDX_HEADS), w_in.dtype)
    w_t = jnp.concatenate([q_idx, dsa_q, w_idx, pad], axis=1).T.astype(BF16)
    return w_main, w_t


def kernel(x, attn_norm_w, w_in, ret_norm_w, dsa_kv_norm_w, dsa_w_uk, dsa_w_uv, w_out, mlp_norm_w, w_up, w_down, final_norm_w):
    B, S, D = x.shape
    depth = w_in.shape[0]
    assert S % ROW_TILE == 0 and S % DSA_KC == 0 and S % RET_CHUNK == 0
    tables = _retention_tables(S)
    x2 = x.reshape(B * S, D)
    for layer in range(depth):
        w_main, w_t = _split_w_in(w_in[layer])
        ret_cols, dq, dk, dv, ckv, kidx, qidxT, dsaqT, widxT = _inproj(
            x2, attn_norm_w[layer][None, :], w_main, w_t, B, S)
        ret_o = _retention(ret_cols, tables, ret_norm_w[layer].reshape(1, RET_W), B, S)
        k, vT = _dsa_kv(ckv, dsa_kv_norm_w[layer][None, :], dsa_w_uk[layer].astype(BF16),
                        dsa_w_uv[layer].T.astype(BF16))
        dsa_o = _dsa(qidxT, dsaqT, widxT, kidx, k, vT, B, S)
        dil_o = _dilated_fused(dq, dk, dv, B, S)
        x2 = _mlp(x2, ret_o, dsa_o, dil_o, w_out[layer].astype(BF16), mlp_norm_w[layer][None, :],
                  w_up[layer].astype(BF16), w_down[layer].astype(BF16))
    return _final_norm(x2, final_norm_w[None, :]).reshape(B, S, D)
```

```python
import functools
import math

import numpy as np
import jax
import jax.numpy as jnp
from jax import lax
from jax.experimental import pallas as pl
from jax.experimental.pallas import tpu as pltpu

F32 = jnp.float32
BF16 = jnp.bfloat16
I32 = jnp.int32
I16 = jnp.int16

HEAD_DIM = 64
RET_HEADS = 4
DSA_HEADS = 4
DIL_HEADS = 8
RET_W = RET_HEADS * HEAD_DIM
DSA_W = DSA_HEADS * HEAD_DIM
DIL_W = DIL_HEADS * HEAD_DIM
RET_CHUNK = 128
RET_STEP_CHUNKS = 8
RET_ROPE_BASE = 10000.0
KV_LATENT = 128
IDX_HEADS = 8
IDX_DIM = 64
TOPK_MAX = 256
DIL_PATTERNS = ((128, 1), (512, 4), (2048, 16))
DIL_BLOCK = 128
DIL_SPAN = 2048
DIL_GROUP = 4
DIL_MERGE_ROWS = 256
NORM_EPS = 1e-6

V7X_VMEM_LIMIT_BYTES = 56 * 1024 * 1024

ROW_TILE = 512
DSA_QB = 256
DSA_KC = 256
DSA_SUB = 64
DSA_SCORE_ROWS = 128
DSA_IDX_GROUP = 4
FF_CHUNK = 1024

WIDX_ROWS = 16

INT_MIN = -(2 ** 31)
KEY_NEG_INF = int(np.int32(np.uint32(0xFF800000) ^ np.uint32(0x7FFFFFFF)))
MASKED_SCORE = -1e30


def _params(*semantics):
    return pltpu.CompilerParams(dimension_semantics=semantics, vmem_limit_bytes=V7X_VMEM_LIMIT_BYTES)


def _resident(shape, index_map):
    return pl.BlockSpec(shape, index_map, pipeline_mode=pl.Buffered(1))


def _dot(a, b):
    return jnp.dot(a, b, preferred_element_type=F32)


def _dot_nt(a, b):
    return lax.dot_general(a, b, (((1,), (1,)), ((), ())), preferred_element_type=F32)


def _rms(x, w):
    return x * lax.rsqrt(jnp.mean(x * x, axis=-1, keepdims=True) + NORM_EPS) * w


def _inproj_kernel(x_ref, nw_ref, w_ref, wt_ref, kvn_ref, wuk_ref, wuvT_ref,
                   ret_ref, dq_ref, dk_ref, dv_ref, kidx_ref, qidxT_ref, dsaqT_ref, widxT_ref, k_ref, vT_ref):
    h = _rms(x_ref[...], nw_ref[...]).astype(BF16)
    o = 0
    for ref in (ret_ref, dq_ref, dk_ref, dv_ref, kidx_ref):
        n = ref.shape[-1]
        ref[...] = _dot(h, w_ref[:, o:o + n]).astype(ref.dtype)
        o += n
    c = _rms(_dot(h, w_ref[:, o:o + KV_LATENT]), kvn_ref[...]).astype(BF16)
    k_ref[...] = _dot(c, wuk_ref[...]).astype(k_ref.dtype)
    for j in range(vT_ref.shape[0]):
        vT_ref[j] = _dot_nt(wuvT_ref[...], c[j * DSA_KC:(j + 1) * DSA_KC, :]).astype(vT_ref.dtype)
    o = 0
    for ref in (qidxT_ref, dsaqT_ref, widxT_ref):
        n = ref.shape[-2]
        ref[0] = _dot_nt(wt_ref[o:o + n, :], h).astype(ref.dtype)
        o += n


def _inproj(x2, nw, w_main, w_t, kvn, w_uk, w_uvT, B, S):
    T, D = x2.shape
    tm = ROW_TILE
    nt = S // tm
    assert tm % DSA_KC == 0
    widx_rows = w_t.shape[0] - IDX_HEADS * IDX_DIM - DSA_W
    row = lambda i: (i, 0)
    tr = lambda i: (i // nt, 0, i % nt)
    const = lambda i: (0, 0)
    out_shape = (
        jax.ShapeDtypeStruct((T, 4 * RET_W), F32),
        jax.ShapeDtypeStruct((T, DIL_W), BF16),
        jax.ShapeDtypeStruct((T, DIL_W), BF16),
        jax.ShapeDtypeStruct((T, DIL_W), BF16),
        jax.ShapeDtypeStruct((T, IDX_DIM), BF16),
        jax.ShapeDtypeStruct((B, IDX_HEADS * IDX_DIM, S), BF16),
        jax.ShapeDtypeStruct((B, DSA_W, S), BF16),
        jax.ShapeDtypeStruct((B, widx_rows, S), F32),
        jax.ShapeDtypeStruct((T, DSA_W), BF16),
        jax.ShapeDtypeStruct((T // DSA_KC, DSA_W, DSA_KC), BF16),
    )
    out_specs = (
        pl.BlockSpec((tm, 4 * RET_W), row),
        pl.BlockSpec((tm, DIL_W), row),
        pl.BlockSpec((tm, DIL_W), row),
        pl.BlockSpec((tm, DIL_W), row),
        pl.BlockSpec((tm, IDX_DIM), row),
        pl.BlockSpec((1, IDX_HEADS * IDX_DIM, tm), tr),
        pl.BlockSpec((1, DSA_W, tm), tr),
        pl.BlockSpec((1, widx_rows, tm), tr),
        pl.BlockSpec((tm, DSA_W), row),
        pl.BlockSpec((tm // DSA_KC, DSA_W, DSA_KC), lambda i: (i, 0, 0)),
    )
    return pl.pallas_call(
        _inproj_kernel,
        grid=(T // tm,),
        in_specs=[
            pl.BlockSpec((tm, D), row),
            _resident((1, D), const),
            _resident(w_main.shape, const),
            _resident(w_t.shape, const),
            _resident((1, KV_LATENT), const),
            _resident(w_uk.shape, const),
            _resident(w_uvT.shape, const),
        ],
        out_specs=out_specs,
        out_shape=out_shape,
        compiler_params=_params("parallel"),
        name="inproj",
    )(x2, nw, w_main, w_t, kvn, w_uk, w_uvT)


def _retention_tables(S):
    half = HEAD_DIM // 2
    inv = RET_ROPE_BASE ** (-jnp.arange(half, dtype=F32) / half)
    ang = jnp.arange(S, dtype=F32)[:, None] * inv[None, :]
    cos, sin = jnp.cos(ang), jnp.sin(ang)
    cos_h = jnp.concatenate([cos, cos], axis=-1)
    sin_h = jnp.concatenate([-sin, sin], axis=-1)
    cosf = jnp.tile(cos_h, (1, RET_HEADS))
    sinf = jnp.tile(sin_h, (1, RET_HEADS))
    C = RET_CHUNK
    log_g = np.log(1.0 - 2.0 ** (-5.0 - np.arange(RET_HEADS, dtype=np.float64)))
    i = np.arange(C, dtype=np.float64)
    diff = i[:, None] - i[None, :]
    decay = np.exp(np.maximum(diff, 0.0)[None] * log_g[:, None, None]) * (diff >= 0)[None]
    zeta = np.exp((C - 1.0 - i)[None, :] * log_g[:, None])
    xi = np.exp((i + 1.0)[None, :] * log_g[:, None])
    g_chunk = np.exp(C * log_g)
    rep = lambda t: np.repeat(t.T, HEAD_DIM, axis=1)
    g_rows = np.repeat(g_chunk, HEAD_DIM)[:, None] * np.ones((1, RET_W))
    return (cosf, sinf, jnp.asarray(decay, F32), jnp.asarray(rep(zeta), F32), jnp.asarray(rep(xi), F32),
            jnp.asarray(g_rows, F32))


def _retention_kernel(cols_ref, cos_ref, sin_ref, decay_ref, zeta_ref, xi_ref, grow_ref, nw_ref, o_ref, state_ref):
    W = RET_W

    @pl.when(pl.program_id(1) == 0)
    def _():
        state_ref[...] = jnp.zeros_like(state_ref)

    C = RET_CHUNK
    lane = lax.broadcasted_iota(I32, (1, W), 1)
    first_half = (lane % HEAD_DIM) < (HEAD_DIM // 2)
    head_of_lane = lane // HEAD_DIM
    r_head = lax.broadcasted_iota(I32, (W, W), 0) // HEAD_DIM
    c_head = lax.broadcasted_iota(I32, (W, W), 1) // HEAD_DIM
    same_head = r_head == c_head

    def rot(t, rows):
        partner = jnp.where(first_half, pltpu.roll(t, W - HEAD_DIM // 2, 1), pltpu.roll(t, HEAD_DIM // 2, 1))
        return t * cos_ref[rows, :] + partner * sin_ref[rows, :]

    qs, inners, kvs = [], [], []
    for j in range(RET_STEP_CHUNKS):
        rows = slice(j * C, (j + 1) * C)
        q = rot(cols_ref[rows, 0:W], rows)
        k = rot(cols_ref[rows, W:2 * W], rows) * (HEAD_DIM ** -0.5)
        v = cols_ref[rows, 2 * W:3 * W].astype(BF16)
        kb = k.astype(BF16)
        inner = jnp.zeros((C, W), F32)
        for h in range(RET_HEADS):
            sel = head_of_lane == h
            qh = jnp.where(sel, q, 0.0).astype(BF16)
            a = (_dot_nt(qh, kb) * decay_ref[h]).astype(BF16)
            inner = inner + jnp.where(sel, _dot(a, v), 0.0)
        kzT = (k * zeta_ref[...]).T.astype(BF16)
        kv = _dot(kzT, v)
        qs.append(q.astype(BF16))
        inners.append(inner)
        kvs.append(jnp.where(same_head, kv, 0.0))

    state = state_ref[...]
    for j in range(RET_STEP_CHUNKS):
        rows = slice(j * C, (j + 1) * C)
        o = inners[j] + _dot(qs[j], state.astype(BF16)) * xi_ref[...]
        state = grow_ref[...] * state + kvs[j]

        mu = jnp.zeros_like(o)
        for h in range(RET_HEADS):
            sel = head_of_lane == h
            mu = mu + jnp.where(sel, jnp.sum(jnp.where(sel, o, 0.0), axis=-1, keepdims=True), 0.0)
        mu = mu * (1.0 / HEAD_DIM)
        d = o - mu
        var = jnp.zeros_like(o)
        for h in range(RET_HEADS):
            sel = head_of_lane == h
            var = var + jnp.where(sel, jnp.sum(jnp.where(sel, d * d, 0.0), axis=-1, keepdims=True), 0.0)
        var = var * (1.0 / HEAD_DIM)
        y = d * lax.rsqrt(var + NORM_EPS) * nw_ref[...]
        g = cols_ref[rows, 3 * W:4 * W]
        o_ref[rows, :] = (jax.nn.silu(g) * y).astype(o_ref.dtype)
    state_ref[...] = state


def _retention(ret_cols, tables, norm_w, B, S):
    T = ret_cols.shape[0]
    C, W = RET_CHUNK, RET_W
    rows = RET_STEP_CHUNKS * C
    assert S % rows == 0
    n = S // rows
    cosf, sinf, decay, zeta, xi, g_rows = tables
    tok = lambda b, j: (b * n + j, 0)
    pos = lambda b, j: (j, 0)
    const2 = lambda b, j: (0, 0)
    return pl.pallas_call(
        _retention_kernel,
        grid=(B, n),
        in_specs=[
            pl.BlockSpec((rows, 4 * W), tok),
            pl.BlockSpec((rows, W), pos),
            pl.BlockSpec((rows, W), pos),
            _resident(decay.shape, lambda b, j: (0, 0, 0)),
            _resident((C, W), const2),
            _resident((C, W), const2),
            _resident((W, W), const2),
            _resident((1, W), const2),
        ],
        out_specs=pl.BlockSpec((rows, W), tok),
        out_shape=jax.ShapeDtypeStruct((T, W), BF16),
        scratch_shapes=[pltpu.VMEM((W, W), F32)],
        compiler_params=_params("parallel", "arbitrary"),
        name="retention",
    )(ret_cols, cosf, sinf, decay, zeta, xi, g_rows, norm_w)


def _sortable(x):
    b = pltpu.bitcast(x, I32)
    return b ^ ((b >> 31) & 0x7FFFFFFF)


def _dsa_kernel(qidxT_ref, dsaqT_ref, widxT_ref, kidx_ref, k_ref, vT_ref, o_ref,
                key_ref, hi_ref, lo_ref, digit_ref, cut_ref, qh_ref, s_ref, bias_ref, p_ref, acc_ref,
                *, n_sel, S):
    QB, KC, SUB = DSA_QB, DSA_KC, DSA_SUB
    n_sub = KC // SUB
    i = pl.program_id(1)
    n_chunks = i + 1
    n_pairs = (n_chunks + 1) // 2
    qpos = i * QB + lax.broadcasted_iota(I32, (1, QB), 1)
    krow = lax.broadcasted_iota(I32, (SUB, 1), 0)
    w_scale = IDX_HEADS ** -0.5 * IDX_DIM ** -0.5
    widxT = widxT_ref[0] * w_scale

    def score_chunk(c, carry):
        off = pl.multiple_of(c * (2 * KC), 2 * KC)
        for sub in range(2 * KC // DSA_SCORE_ROWS):
            r0 = off + sub * DSA_SCORE_ROWS
            kc = kidx_ref[pl.ds(r0, DSA_SCORE_ROWS), :]
            acc = jnp.zeros((DSA_SCORE_ROWS, QB), F32)
            for g in range(0, IDX_HEADS, DSA_IDX_GROUP):
                logits = [_dot(kc, qidxT_ref[0, h * IDX_DIM:(h + 1) * IDX_DIM, :])
                          for h in range(g, g + DSA_IDX_GROUP)]
                for h, logit in zip(range(g, g + DSA_IDX_GROUP), logits):
                    acc = acc + jnp.maximum(logit, 0.0) * widxT[h:h + 1, :]
            for part in range(DSA_SCORE_ROWS // SUB):
                a = acc[part * SUB:(part + 1) * SUB, :]
                a = jnp.where(a == 0.0, 0.0, a)
                score = jnp.where(r0 + part * SUB + krow <= qpos, a, -jnp.inf)
                key = _sortable(score)
                rows = pl.ds(r0 + part * SUB, SUB)
                key_ref[rows, :] = key
                hi_ref[rows, :] = (key >> 16).astype(I16)
                lo_ref[rows, :] = ((key & 0xFFFF) - 0x8000).astype(I16)
        return carry

    lax.fori_loop(0, n_pairs, score_chunk, 0)

    one_b, zero_b = jnp.ones((SUB, QB), BF16), jnp.zeros((SUB, QB), BF16)

    def count_packed(ref, pred):
        def body(c2, cnt):
            off = pl.multiple_of(c2 * (2 * KC), 2 * KC)
            for sub in range(2 * n_sub):
                cnt = cnt + jnp.where(pred(ref[pl.ds(off + sub * SUB, SUB), :]), one_b, zero_b)
            return cnt
        cnt = lax.fori_loop(0, n_pairs, body, zero_b)
        return jnp.sum(cnt.astype(F32), axis=0, keepdims=True)

    def signed_digit(u):
        return (u - 0x8000).astype(I16)

    def search_digit(ref, n_above, n_ge_zero):
        def step(t, carry):
            d, n_ge = carry
            cand = d | jnp.left_shift(jnp.int32(1), 15 - t)
            cand_s = signed_digit(cand)
            cnt = n_above + count_packed(ref, lambda x: x >= cand_s)
            keep = cnt >= n_sel
            return jnp.where(keep, cand, d), jnp.where(keep, cnt, n_ge)
        return lax.fori_loop(0, 16, step, (jnp.zeros((1, QB), I32), n_ge_zero))

    stored = jnp.full((1, QB), (n_pairs * (2 * KC)).astype(F32), F32)
    u_hi, n_ge = search_digit(hi_ref, 0.0, stored)
    thr_hi = signed_digit(u_hi)
    n_above = count_packed(hi_ref, lambda x: x > thr_hi)
    lowest = jnp.full((SUB, QB), -0x8000, I16)

    def low_digits_of_ties(c2, carry):
        off = pl.multiple_of(c2 * (2 * KC), 2 * KC)
        for sub in range(2 * n_sub):
            rows = pl.ds(off + sub * SUB, SUB)
            digit_ref[rows, :] = jnp.where(hi_ref[rows, :] == thr_hi, lo_ref[rows, :], lowest)
        return carry

    lax.fori_loop(0, n_pairs, low_digits_of_ties, 0)
    u_lo, n_ge = search_digit(digit_ref, n_above, n_ge)
    thr = ((u_hi - 0x8000) << 16) | u_lo

    needs_cut = (n_ge > n_sel) & (thr > KEY_NEG_INF)
    any_ties = jnp.max(jnp.where(needs_cut, 1.0, 0.0)) > 0.0
    cut_ref[...] = jnp.full((1, QB), S, I32)

    @pl.when(any_ties)
    def _():
        thr_lo = signed_digit(u_lo)
        tie_take = n_sel - (n_above + count_packed(digit_ref, lambda x: x > thr_lo))
        far = jnp.full((SUB, QB), 0x7FFF, I16)

        def tied_positions(c2, carry):
            off = pl.multiple_of(c2 * (2 * KC), 2 * KC)
            for sub in range(2 * n_sub):
                r0 = off + sub * SUB
                rows = pl.ds(r0, SUB)
                pos = (r0 + lax.broadcasted_iota(I32, (SUB, QB), 0)).astype(I16)
                tied = (hi_ref[rows, :] == thr_hi) & (lo_ref[rows, :] == thr_lo)
                digit_ref[rows, :] = jnp.where(tied, pos, far)
            return carry

        lax.fori_loop(0, n_pairs, tied_positions, 0)

        def cut_step(t, cpos):
            cand = cpos | jnp.left_shift(jnp.int32(1), (S - 1).bit_length() - 1 - t)
            cand_s = cand.astype(I16)
            before = count_packed(digit_ref, lambda x: x < cand_s)
            return jnp.where(before < tie_take, cand, cpos)
        cpos = lax.fori_loop(0, (S - 1).bit_length(), cut_step, jnp.zeros((1, QB), I32))
        cut_ref[...] = jnp.where(needs_cut, cpos, S)

    cut = cut_ref[...]

    row_head = lax.broadcasted_iota(I32, (DSA_W, 1), 0) // HEAD_DIM
    dsaqT = dsaqT_ref[0] * (HEAD_DIM ** -0.5)
    for h in range(DSA_HEADS):
        qh_ref[h] = jnp.where(row_head == h, dsaqT, 0.0).astype(BF16)
    acc_ref[...] = jnp.zeros_like(acc_ref)
    p_ref[...] = jnp.zeros_like(p_ref)

    thr_floor = jnp.maximum(thr, KEY_NEG_INF + 1)

    tie_cut = jnp.where(thr > KEY_NEG_INF, cut, -1)

    def write_bias(slot, c, ties):
        off = pl.multiple_of(c * KC, KC)
        for sub in range(n_sub):
            r0 = off + sub * SUB
            key = key_ref[pl.ds(r0, SUB), :]
            if ties:
                tied = jnp.where(r0 + krow <= tie_cut, 0.0, MASKED_SCORE)
                bias = jnp.where(key > thr, 0.0, jnp.where(key == thr, tied, MASKED_SCORE))
            else:
                bias = jnp.where(key >= thr_floor, 0.0, MASKED_SCORE)
            bias_ref[slot, sub * SUB:(sub + 1) * SUB, :] = bias

    def issue_scores(slot, c):
        kch = k_ref[pl.ds(pl.multiple_of(c * KC, KC), KC), :]
        for h in range(DSA_HEADS):
            s_ref[slot, h] = _dot(kch, qh_ref[h])

    def pv_update(slot, c, alphas):
        vch = vT_ref[c]
        pvs = [_dot(vch[h * HEAD_DIM:(h + 1) * HEAD_DIM, :], p_ref[slot, h]) for h in range(DSA_HEADS)]
        for h in range(DSA_HEADS):
            acc_ref[h] = alphas[h] * acc_ref[h] + pvs[h]

    def softmax_chunk(slot, ms, ls):
        new_m, new_l, alphas = [], [], []
        for h in range(DSA_HEADS):
            mx = jnp.full((SUB, QB), MASKED_SCORE, F32)
            for sub in range(n_sub):
                rows = slice(sub * SUB, (sub + 1) * SUB)
                t = s_ref[slot, h, rows, :] + bias_ref[slot, rows, :]
                s_ref[slot, h, rows, :] = t
                mx = jnp.maximum(mx, t)
            m_new = jnp.maximum(ms[h], jnp.max(mx, axis=0, keepdims=True))
            alpha = jnp.exp(ms[h] - m_new)
            psum = jnp.zeros((SUB, QB), F32)
            for sub in range(n_sub):
                rows = slice(sub * SUB, (sub + 1) * SUB)
                p = jnp.exp(s_ref[slot, h, rows, :] - m_new)
                psum = psum + p
                p_ref[slot, h, rows, :] = p.astype(BF16)
            new_m.append(m_new)
            new_l.append(alpha * ls[h] + jnp.sum(psum, axis=0, keepdims=True))
            alphas.append(alpha)
        return tuple(new_m), tuple(new_l), tuple(alphas)

    def attend(ties):
        def attend_pair(c2, carry):
            ms, ls, prev_alphas = carry
            ca, cb = 2 * c2, 2 * c2 + 1
            write_bias(0, ca, ties)
            write_bias(1, cb, ties)
            pv_update(1, jnp.maximum(ca - 1, 0), prev_alphas)
            issue_scores(0, ca)
            issue_scores(1, cb)
            ms, ls, alphas_a = softmax_chunk(0, ms, ls)
            pv_update(0, ca, alphas_a)
            ms, ls, alphas_b = softmax_chunk(1, ms, ls)
            return ms, ls, alphas_b

        init = (tuple(jnp.full((1, QB), MASKED_SCORE, F32) for _ in range(DSA_HEADS)),
                tuple(jnp.zeros((1, QB), F32) for _ in range(DSA_HEADS)),
                tuple(jnp.ones((1, QB), F32) for _ in range(DSA_HEADS)))
        _, ls, last_alphas = lax.fori_loop(0, n_pairs, attend_pair, init)
        pv_update(1, 2 * n_pairs - 1, last_alphas)
        for h in range(DSA_HEADS):
            acc_ref[h] = acc_ref[h] / ls[h]

    pl.when(any_ties)(lambda: attend(True))
    pl.when(jnp.logical_not(any_ties))(lambda: attend(False))
    o_ref[...] = acc_ref[...].reshape(DSA_W, QB).T.astype(o_ref.dtype)


def _dsa(qidxT, dsaqT, widxT, kidx, k, vT, B, S):
    QB, KC = DSA_QB, DSA_KC
    assert QB == KC and S % (2 * KC) == 0
    n_sel = min(TOPK_MAX, S // 4)
    assert n_sel <= KC, "the first causal chunk must be able to hold every selected key"
    assert S // DSA_SUB <= 256, "packed counts are exact only up to 256 adds per accumulator lane"
    assert S <= 0x7FFF, "key positions are held as int16 when ties are cut"
    nq = S // QB
    nkc = S // KC
    n_sel = min(TOPK_MAX, S // 4)
    T = B * S
    kern = functools.partial(_dsa_kernel, n_sel=n_sel, S=S)
    return pl.pallas_call(
        kern,
        grid=(B, nq),
        in_specs=[
            pl.BlockSpec((1, IDX_HEADS * IDX_DIM, QB), lambda b, i: (b, 0, i)),
            pl.BlockSpec((1, DSA_W, QB), lambda b, i: (b, 0, i)),
            pl.BlockSpec((1, widxT.shape[1], QB), lambda b, i: (b, 0, i)),
            _resident((S, IDX_DIM), lambda b, i: (b, 0)),
            _resident((S, DSA_W), lambda b, i: (b, 0)),
            _resident((nkc, DSA_W, KC), lambda b, i: (b, 0, 0)),
        ],
        out_specs=pl.BlockSpec((QB, DSA_W), lambda b, i: (b * nq + i, 0)),
        out_shape=jax.ShapeDtypeStruct((T, DSA_W), BF16),
        scratch_shapes=[
            pltpu.VMEM((S, QB), I32),
            pltpu.VMEM((S, QB), I16),
            pltpu.VMEM((S, QB), I16),
            pltpu.VMEM((S, QB), I16),
            pltpu.VMEM((1, QB), I32),
            pltpu.VMEM((DSA_HEADS, DSA_W, QB), BF16),
            pltpu.VMEM((2, DSA_HEADS, KC, QB), F32),
            pltpu.VMEM((2, KC, QB), F32),
            pltpu.VMEM((2, DSA_HEADS, KC, QB), BF16),
            pltpu.VMEM((DSA_HEADS, HEAD_DIM, QB), F32),
        ],
        compiler_params=_params("parallel", "arbitrary"),
        name="dsa",
    )(qidxT, dsaqT, widxT, kidx, k, vT)


def _dil_units(dilation):
    blocks = DIL_SPAN // (DIL_BLOCK * dilation)
    return [(res, jb) for res in range(dilation) for jb in range(blocks)]


def _dilated_fused_kernel(q_ref, kp_ref, kc_ref, vp_ref, vc_ref, out_ref,
                          qf_ref, kf_ref, vf_ref, s_ref, p_ref, o_ref, lse_ref):
    L, SPAN = DIL_BLOCK, DIL_SPAN
    first_span = pl.program_id(1) == 0
    qf_ref[...] = q_ref[...].astype(F32) * (HEAD_DIM ** -0.5)
    kf_ref[0:SPAN, :] = kp_ref[...].astype(F32)
    kf_ref[SPAN:, :] = kc_ref[...].astype(F32)
    vf_ref[0:SPAN, :] = vp_ref[...].astype(F32)
    vf_ref[SPAN:, :] = vc_ref[...].astype(F32)

    lane = lax.broadcasted_iota(I32, (1, 2 * HEAD_DIM), 1)
    low = lane < HEAD_DIM
    qi = lax.broadcasted_iota(I32, (L, 2 * L), 0)
    kj = lax.broadcasted_iota(I32, (L, 2 * L), 1)
    dist = L + qi - kj
    band = (dist >= 0) & (dist <= L)
    bias_band = jnp.where(band, 0.0, -jnp.inf)
    bias_head = jnp.where(band & ((kj >= L) | jnp.logical_not(first_span)), 0.0, -jnp.inf)

    def rows(start, count, stride):
        return pl.ds(start, count, stride=stride) if stride > 1 else pl.ds(start, count)

    for branch, (window, r) in enumerate(DIL_PATTERNS):
        units = _dil_units(r)
        for g0 in range(0, len(units), DIL_GROUP):
            group = units[g0:g0 + DIL_GROUP]
            vvs = []
            for u, (res, jb) in enumerate(group):
                q = qf_ref[rows(res + r * L * jb, L, r), :].astype(BF16)
                k0 = SPAN + res + r * L * (jb - 1)
                kk = kf_ref[rows(k0, 2 * L, r), :].astype(BF16)
                vv = vf_ref[rows(k0, 2 * L, r), :]
                for sub in range(2):
                    mine = low if sub == 0 else jnp.logical_not(low)
                    s_ref[2 * u + sub] = _dot_nt(jnp.where(mine, q, jnp.zeros_like(q)), kk)
                    vvs.append(jnp.where(mine, vv, 1.0).astype(BF16))
            ms = []
            for u, (res, jb) in enumerate(group):
                bias = bias_head if jb == 0 else bias_band
                for sub in range(2):
                    t = s_ref[2 * u + sub] + bias
                    m = jnp.max(t, axis=-1, keepdims=True)
                    p_ref[2 * u + sub] = jnp.exp(t - m).astype(BF16)
                    ms.append(m)
            for u, (res, jb) in enumerate(group):
                halves = []
                for sub in range(2):
                    nd = _dot(p_ref[2 * u + sub], vvs[2 * u + sub])
                    halves.append((nd, pltpu.roll(nd, HEAD_DIM, 1), ms[2 * u + sub]))
                (n0, d0, m0), (n1, d1, m1) = halves
                dst = rows(res + r * L * jb, L, r)
                o_ref[branch, dst, :] = jnp.where(low, n0 / d0, n1 / d1)
                lse_ref[branch, dst, :] = jnp.where(low, m0 + jnp.log(d0), m1 + jnp.log(d1))

    for piece in range(SPAN // DIL_MERGE_ROWS):
        rs = slice(piece * DIL_MERGE_ROWS, (piece + 1) * DIL_MERGE_ROWS)
        a, b, c = lse_ref[0, rs, :], lse_ref[1, rs, :], lse_ref[2, rs, :]
        m = jnp.maximum(jnp.maximum(a, b), c)
        ea, eb, ec = jnp.exp(a - m), jnp.exp(b - m), jnp.exp(c - m)
        mix = (ea * o_ref[0, rs, :] + eb * o_ref[1, rs, :] + ec * o_ref[2, rs, :]) / (ea + eb + ec)
        out_ref[rs, :] = mix.astype(out_ref.dtype)


def _dilated_fused(q, k, v, B, S):
    T, W = q.shape
    SPAN, L = DIL_SPAN, DIL_BLOCK
    for window, r in DIL_PATTERNS:
        assert window // r == L and SPAN % (L * r) == 0 and L * r <= SPAN
    assert S % SPAN == 0
    ns = S // SPAN
    PW = 2 * HEAD_DIM
    own = lambda b, t, hp: (b * ns + t, hp)
    prev = lambda b, t, hp: (b * ns + jnp.maximum(t - 1, 0), hp)
    blk = (SPAN, PW)
    return pl.pallas_call(
        _dilated_fused_kernel,
        grid=(B, ns, W // PW),
        in_specs=[pl.BlockSpec(blk, own), pl.BlockSpec(blk, prev), pl.BlockSpec(blk, own),
                  pl.BlockSpec(blk, prev), pl.BlockSpec(blk, own)],
        out_specs=pl.BlockSpec(blk, own),
        out_shape=jax.ShapeDtypeStruct((T, W), BF16),
        scratch_shapes=[
            pltpu.VMEM((SPAN, PW), F32),
            pltpu.VMEM((2 * SPAN, PW), F32),
            pltpu.VMEM((2 * SPAN, PW), F32),
            pltpu.VMEM((2 * DIL_GROUP, L, 2 * L), F32),
            pltpu.VMEM((2 * DIL_GROUP, L, 2 * L), BF16),
            pltpu.VMEM((len(DIL_PATTERNS), SPAN, PW), F32),
            pltpu.VMEM((len(DIL_PATTERNS), SPAN, PW), F32),
        ],
        compiler_params=_params("parallel", "parallel", "parallel"),
        name="dilated",
    )(q, k, k, v, v)


def _mlp_kernel(x_ref, ret_ref, dsa_ref, dil_ref, wout_ref, nw_ref, wup_ref, wdown_ref, fnw_ref, o_ref, *,
                apply_final_norm):
    mixed = (_dot(ret_ref[...], wout_ref[0:RET_W, :])
             + _dot(dsa_ref[...], wout_ref[RET_W:RET_W + DSA_W, :])
             + _dot(dil_ref[...], wout_ref[RET_W + DSA_W:, :]))
    x = x_ref[...] + mixed
    h = _rms(x, nw_ref[...]).astype(BF16)
    ff = None
    for c in range(wup_ref.shape[1] // FF_CHUNK):
        cs = slice(c * FF_CHUNK, (c + 1) * FF_CHUNK)
        u = jnp.maximum(_dot(h, wup_ref[:, cs]), 0.0)
        part = _dot((u * u).astype(BF16), wdown_ref[cs, :])
        ff = part if ff is None else ff + part
    out = x + ff
    o_ref[...] = _rms(out, fnw_ref[...]) if apply_final_norm else out


def _mlp(x2, ret_o, dsa_o, dil_o, w_out, nw, w_up, w_down, final_nw, *, apply_final_norm):
    T, D = x2.shape
    tm = ROW_TILE
    row = lambda i: (i, 0)
    const = lambda i: (0, 0)
    return pl.pallas_call(
        functools.partial(_mlp_kernel, apply_final_norm=apply_final_norm),
        grid=(T // tm,),
        in_specs=[
            pl.BlockSpec((tm, D), row),
            pl.BlockSpec((tm, RET_W), row),
            pl.BlockSpec((tm, DSA_W), row),
            pl.BlockSpec((tm, DIL_W), row),
            _resident(w_out.shape, const),
            _resident((1, D), const),
            _resident(w_up.shape, const),
            _resident(w_down.shape, const),
            _resident((1, D), const),
        ],
        out_specs=pl.BlockSpec((tm, D), row),
        out_shape=jax.ShapeDtypeStruct((T, D), F32),
        compiler_params=_params("parallel"),
        name="outproj_mlp",
    )(x2, ret_o, dsa_o, dil_o, w_out, nw, w_up, w_down, final_nw)


def _split_w_in(w_in):
    ret_cols = 4 * RET_W
    o = ret_cols
    dsa_q = w_in[:, o:o + DSA_W]; o += DSA_W
    c_kv = w_in[:, o:o + KV_LATENT]; o += KV_LATENT
    q_idx = w_in[:, o:o + IDX_HEADS * IDX_DIM]; o += IDX_HEADS * IDX_DIM
    k_idx = w_in[:, o:o + IDX_DIM]; o += IDX_DIM
    w_idx = w_in[:, o:o + IDX_HEADS]; o += IDX_HEADS
    dil = w_in[:, o:]
    w_main = jnp.concatenate([w_in[:, :ret_cols], dil, k_idx, c_kv], axis=1).astype(BF16)
    pad = jnp.zeros((w_in.shape[0], WIDX_ROWS - IDX_HEADS), w_in.dtype)
    w_t = jnp.concatenate([q_idx, dsa_q, w_idx, pad], axis=1).T.astype(BF16)
    return w_main, w_t


def kernel(x, attn_norm_w, w_in, ret_norm_w, dsa_kv_norm_w, dsa_w_uk, dsa_w_uv, w_out, mlp_norm_w, w_up, w_down, final_norm_w):
    B, S, D = x.shape
    depth = w_in.shape[0]
    assert S % ROW_TILE == 0 and S % DSA_KC == 0 and S % RET_CHUNK == 0
    tables = _retention_tables(S)
    x2 = x.reshape(B * S, D)
    for layer in range(depth):
        w_main, w_t = _split_w_in(w_in[layer])
        ret_cols, dq, dk, dv, kidx, qidxT, dsaqT, widxT, k, vT = _inproj(
            x2, attn_norm_w[layer][None, :], w_main, w_t, dsa_kv_norm_w[layer][None, :],
            dsa_w_uk[layer].astype(BF16), dsa_w_uv[layer].T.astype(BF16), B, S)
        ret_o = _retention(ret_cols, tables, ret_norm_w[layer].reshape(1, RET_W), B, S)
        dsa_o = _dsa(qidxT, dsaqT, widxT, kidx, k, vT, B, S)
        dil_o = _dilated_fused(dq, dk, dv, B, S)
        x2 = _mlp(x2, ret_o, dsa_o, dil_o, w_out[layer].astype(BF16), mlp_norm_w[layer][None, :],
                  w_up[layer].astype(BF16), w_down[layer].astype(BF16), final_norm_w[None, :],
                  apply_final_norm=layer == depth - 1)
    return x2.reshape(B, S, D)
```

```python
import functools
import math

import numpy as np
import jax
import jax.numpy as jnp
from jax import lax
from jax.experimental import pallas as pl
from jax.experimental.pallas import tpu as pltpu

F32 = jnp.float32
BF16 = jnp.bfloat16
I32 = jnp.int32
I16 = jnp.int16

HEAD_DIM = 64
RET_HEADS = 4
DSA_HEADS = 4
DIL_HEADS = 8
RET_W = RET_HEADS * HEAD_DIM
DSA_W = DSA_HEADS * HEAD_DIM
DIL_W = DIL_HEADS * HEAD_DIM
RET_CHUNK = 128
RET_STEP_CHUNKS = 8
RET_ROPE_BASE = 10000.0
KV_LATENT = 128
IDX_HEADS = 8
IDX_DIM = 64
TOPK_MAX = 256
DIL_PATTERNS = ((128, 1), (512, 4), (2048, 16))
DIL_BLOCK = 128
DIL_SPAN = 2048
DIL_GROUP = 4
DIL_MERGE_ROWS = 256
NORM_EPS = 1e-6

V7X_VMEM_LIMIT_BYTES = 56 * 1024 * 1024

ROW_TILE = 512
DSA_QB = 256
DSA_KC = 256
DSA_SUB = 64
DSA_SCORE_ROWS = 128
DSA_IDX_GROUP = 4
FF_CHUNK = 1024

WIDX_ROWS = 16

INT_MIN = -(2 ** 31)
KEY_NEG_INF = int(np.int32(np.uint32(0xFF800000) ^ np.uint32(0x7FFFFFFF)))
MASKED_SCORE = -1e30


def _params(*semantics):
    return pltpu.CompilerParams(dimension_semantics=semantics, vmem_limit_bytes=V7X_VMEM_LIMIT_BYTES)


def _resident(shape, index_map):
    return pl.BlockSpec(shape, index_map, pipeline_mode=pl.Buffered(1))


def _dot(a, b):
    return jnp.dot(a, b, preferred_element_type=F32)


def _dot_nt(a, b):
    return lax.dot_general(a, b, (((1,), (1,)), ((), ())), preferred_element_type=F32)


def _rms(x, w):
    return x * lax.rsqrt(jnp.mean(x * x, axis=-1, keepdims=True) + NORM_EPS) * w


def _inproj_kernel(x_ref, nw_ref, w_ref, wt_ref, kvn_ref, wuk_ref, wuvT_ref,
                   ret_ref, dq_ref, dk_ref, dv_ref, kidx_ref, qidxT_ref, dsaqT_ref, widxT_ref, k_ref, vT_ref):
    h = _rms(x_ref[...], nw_ref[...]).astype(BF16)
    o = 0
    for ref in (ret_ref, dq_ref, dk_ref, dv_ref, kidx_ref):
        n = ref.shape[-1]
        ref[...] = _dot(h, w_ref[:, o:o + n]).astype(ref.dtype)
        o += n
    c = _rms(_dot(h, w_ref[:, o:o + KV_LATENT]), kvn_ref[...]).astype(BF16)
    k_ref[...] = _dot(c, wuk_ref[...]).astype(k_ref.dtype)
    for j in range(vT_ref.shape[0]):
        vT_ref[j] = _dot_nt(wuvT_ref[...], c[j * DSA_KC:(j + 1) * DSA_KC, :]).astype(vT_ref.dtype)
    o = 0
    for ref in (qidxT_ref, dsaqT_ref, widxT_ref):
        n = ref.shape[-2]
        ref[0] = _dot_nt(wt_ref[o:o + n, :], h).astype(ref.dtype)
        o += n


def _inproj(x2, nw, w_main, w_t, kvn, w_uk, w_uvT, B, S):
    T, D = x2.shape
    tm = ROW_TILE
    nt = S // tm
    assert tm % DSA_KC == 0
    widx_rows = w_t.shape[0] - IDX_HEADS * IDX_DIM - DSA_W
    row = lambda i: (i, 0)
    tr = lambda i: (i // nt, 0, i % nt)
    const = lambda i: (0, 0)
    out_shape = (
        jax.ShapeDtypeStruct((T, 4 * RET_W), F32),
        jax.ShapeDtypeStruct((T, DIL_W), BF16),
        jax.ShapeDtypeStruct((T, DIL_W), BF16),
        jax.ShapeDtypeStruct((T, DIL_W), BF16),
        jax.ShapeDtypeStruct((T, IDX_DIM), BF16),
        jax.ShapeDtypeStruct((B, IDX_HEADS * IDX_DIM, S), BF16),
        jax.ShapeDtypeStruct((B, DSA_W, S), BF16),
        jax.ShapeDtypeStruct((B, widx_rows, S), F32),
        jax.ShapeDtypeStruct((T, DSA_W), BF16),
        jax.ShapeDtypeStruct((T // DSA_KC, DSA_W, DSA_KC), BF16),
    )
    out_specs = (
        pl.BlockSpec((tm, 4 * RET_W), row),
        pl.BlockSpec((tm, DIL_W), row),
        pl.BlockSpec((tm, DIL_W), row),
        pl.BlockSpec((tm, DIL_W), row),
        pl.BlockSpec((tm, IDX_DIM), row),
        pl.BlockSpec((1, IDX_HEADS * IDX_DIM, tm), tr),
        pl.BlockSpec((1, DSA_W, tm), tr),
        pl.BlockSpec((1, widx_rows, tm), tr),
        pl.BlockSpec((tm, DSA_W), row),
        pl.BlockSpec((tm // DSA_KC, DSA_W, DSA_KC), lambda i: (i, 0, 0)),
    )
    return pl.pallas_call(
        _inproj_kernel,
        grid=(T // tm,),
        in_specs=[
            pl.BlockSpec((tm, D), row),
            _resident((1, D), const),
            _resident(w_main.shape, const),
            _resident(w_t.shape, const),
            _resident((1, KV_LATENT), const),
            _resident(w_uk.shape, const),
            _resident(w_uvT.shape, const),
        ],
        out_specs=out_specs,
        out_shape=out_shape,
        compiler_params=_params("parallel"),
        name="inproj",
    )(x2, nw, w_main, w_t, kvn, w_uk, w_uvT)


def _retention_tables(S):
    half = HEAD_DIM // 2
    inv = RET_ROPE_BASE ** (-jnp.arange(half, dtype=F32) / half)
    ang = jnp.arange(S, dtype=F32)[:, None] * inv[None, :]
    cos, sin = jnp.cos(ang), jnp.sin(ang)
    cos_h = jnp.concatenate([cos, cos], axis=-1)
    sin_h = jnp.concatenate([-sin, sin], axis=-1)
    cosf = jnp.tile(cos_h, (1, RET_HEADS))
    sinf = jnp.tile(sin_h, (1, RET_HEADS))
    C = RET_CHUNK
    log_g = np.log(1.0 - 2.0 ** (-5.0 - np.arange(RET_HEADS, dtype=np.float64)))
    i = np.arange(C, dtype=np.float64)
    diff = i[:, None] - i[None, :]
    decay = np.exp(np.maximum(diff, 0.0)[None] * log_g[:, None, None]) * (diff >= 0)[None]
    zeta = np.exp((C - 1.0 - i)[None, :] * log_g[:, None])
    xi = np.exp((i + 1.0)[None, :] * log_g[:, None])
    g_chunk = np.exp(C * log_g)
    rep = lambda t: np.repeat(t.T, HEAD_DIM, axis=1)
    g_rows = np.repeat(g_chunk, HEAD_DIM)[:, None] * np.ones((1, RET_W))
    return (cosf, sinf, jnp.asarray(decay, F32), jnp.asarray(rep(zeta), F32), jnp.asarray(rep(xi), F32),
            jnp.asarray(g_rows, F32))


def _retention_kernel(cols_ref, cos_ref, sin_ref, decay_ref, zeta_ref, xi_ref, grow_ref, nw_ref, o_ref, state_ref):
    W = RET_W

    @pl.when(pl.program_id(1) == 0)
    def _():
        state_ref[...] = jnp.zeros_like(state_ref)

    C = RET_CHUNK
    lane = lax.broadcasted_iota(I32, (1, W), 1)
    first_half = (lane % HEAD_DIM) < (HEAD_DIM // 2)
    head_of_lane = lane // HEAD_DIM
    r_head = lax.broadcasted_iota(I32, (W, W), 0) // HEAD_DIM
    c_head = lax.broadcasted_iota(I32, (W, W), 1) // HEAD_DIM
    same_head = r_head == c_head

    def rot(t, rows):
        partner = jnp.where(first_half, pltpu.roll(t, W - HEAD_DIM // 2, 1), pltpu.roll(t, HEAD_DIM // 2, 1))
        return t * cos_ref[rows, :] + partner * sin_ref[rows, :]

    qs, inners, kvs = [], [], []
    for j in range(RET_STEP_CHUNKS):
        rows = slice(j * C, (j + 1) * C)
        q = rot(cols_ref[rows, 0:W], rows)
        k = rot(cols_ref[rows, W:2 * W], rows) * (HEAD_DIM ** -0.5)
        v = cols_ref[rows, 2 * W:3 * W].astype(BF16)
        kb = k.astype(BF16)
        inner = jnp.zeros((C, W), F32)
        for h in range(RET_HEADS):
            sel = head_of_lane == h
            qh = jnp.where(sel, q, 0.0).astype(BF16)
            a = (_dot_nt(qh, kb) * decay_ref[h]).astype(BF16)
            inner = inner + jnp.where(sel, _dot(a, v), 0.0)
        kzT = (k * zeta_ref[...]).T.astype(BF16)
        kv = _dot(kzT, v)
        qs.append(q.astype(BF16))
        inners.append(inner)
        kvs.append(jnp.where(same_head, kv, 0.0))

    state = state_ref[...]
    for j in range(RET_STEP_CHUNKS):
        rows = slice(j * C, (j + 1) * C)
        o = inners[j] + _dot(qs[j], state.astype(BF16)) * xi_ref[...]
        state = grow_ref[...] * state + kvs[j]

        mu = jnp.zeros_like(o)
        for h in range(RET_HEADS):
            sel = head_of_lane == h
            mu = mu + jnp.where(sel, jnp.sum(jnp.where(sel, o, 0.0), axis=-1, keepdims=True), 0.0)
        mu = mu * (1.0 / HEAD_DIM)
        d = o - mu
        var = jnp.zeros_like(o)
        for h in range(RET_HEADS):
            sel = head_of_lane == h
            var = var + jnp.where(sel, jnp.sum(jnp.where(sel, d * d, 0.0), axis=-1, keepdims=True), 0.0)
        var = var * (1.0 / HEAD_DIM)
        y = d * lax.rsqrt(var + NORM_EPS) * nw_ref[...]
        g = cols_ref[rows, 3 * W:4 * W]
        o_ref[rows, :] = (jax.nn.silu(g) * y).astype(o_ref.dtype)
    state_ref[...] = state


def _retention(ret_cols, tables, norm_w, B, S):
    T = ret_cols.shape[0]
    C, W = RET_CHUNK, RET_W
    rows = RET_STEP_CHUNKS * C
    assert S % rows == 0
    n = S // rows
    cosf, sinf, decay, zeta, xi, g_rows = tables
    tok = lambda b, j: (b * n + j, 0)
    pos = lambda b, j: (j, 0)
    const2 = lambda b, j: (0, 0)
    return pl.pallas_call(
        _retention_kernel,
        grid=(B, n),
        in_specs=[
            pl.BlockSpec((rows, 4 * W), tok),
            pl.BlockSpec((rows, W), pos),
            pl.BlockSpec((rows, W), pos),
            _resident(decay.shape, lambda b, j: (0, 0, 0)),
            _resident((C, W), const2),
            _resident((C, W), const2),
            _resident((W, W), const2),
            _resident((1, W), const2),
        ],
        out_specs=pl.BlockSpec((rows, W), tok),
        out_shape=jax.ShapeDtypeStruct((T, W), BF16),
        scratch_shapes=[pltpu.VMEM((W, W), F32)],
        compiler_params=_params("parallel", "arbitrary"),
        name="retention",
    )(ret_cols, cosf, sinf, decay, zeta, xi, g_rows, norm_w)


def _sortable(x):
    b = pltpu.bitcast(x, I32)
    return b ^ ((b >> 31) & 0x7FFFFFFF)


def _dsa_kernel(qidxT_ref, dsaqT_ref, widxT_ref, kidx_ref, k_ref, vT_ref, o_ref,
                key_ref, hi_ref, lo_ref, digit_ref, tied_ref, tri_ref, qh_ref, s_ref, bias_ref, p_ref, acc_ref,
                *, n_sel, S):
    QB, KC, SUB = DSA_QB, DSA_KC, DSA_SUB
    n_sub = KC // SUB
    i = pl.program_id(1)
    n_chunks = i + 1
    n_pairs = (n_chunks + 1) // 2
    qpos = i * QB + lax.broadcasted_iota(I32, (1, QB), 1)
    krow = lax.broadcasted_iota(I32, (SUB, 1), 0)
    w_scale = IDX_HEADS ** -0.5 * IDX_DIM ** -0.5
    widxT = widxT_ref[0] * w_scale

    def score_chunk(c, carry):
        off = pl.multiple_of(c * (2 * KC), 2 * KC)
        for sub in range(2 * KC // DSA_SCORE_ROWS):
            r0 = off + sub * DSA_SCORE_ROWS
            kc = kidx_ref[pl.ds(r0, DSA_SCORE_ROWS), :]
            acc = jnp.zeros((DSA_SCORE_ROWS, QB), F32)
            for g in range(0, IDX_HEADS, DSA_IDX_GROUP):
                logits = [_dot(kc, qidxT_ref[0, h * IDX_DIM:(h + 1) * IDX_DIM, :])
                          for h in range(g, g + DSA_IDX_GROUP)]
                for h, logit in zip(range(g, g + DSA_IDX_GROUP), logits):
                    acc = acc + jnp.maximum(logit, 0.0) * widxT[h:h + 1, :]
            for part in range(DSA_SCORE_ROWS // SUB):
                a = acc[part * SUB:(part + 1) * SUB, :]
                a = jnp.where(a == 0.0, 0.0, a)
                score = jnp.where(r0 + part * SUB + krow <= qpos, a, -jnp.inf)
                key = _sortable(score)
                rows = pl.ds(r0 + part * SUB, SUB)
                key_ref[rows, :] = key
                hi_ref[rows, :] = (key >> 16).astype(I16)
                lo_ref[rows, :] = ((key & 0xFFFF) - 0x8000).astype(I16)
        return carry

    lax.fori_loop(0, n_pairs, score_chunk, 0)

    one_b, zero_b = jnp.ones((SUB, QB), BF16), jnp.zeros((SUB, QB), BF16)

    def count_packed(ref, pred):
        def body(c2, cnt):
            off = pl.multiple_of(c2 * (2 * KC), 2 * KC)
            for sub in range(2 * n_sub):
                cnt = cnt + jnp.where(pred(ref[pl.ds(off + sub * SUB, SUB), :]), one_b, zero_b)
            return cnt
        cnt = lax.fori_loop(0, n_pairs, body, zero_b)
        return jnp.sum(cnt.astype(F32), axis=0, keepdims=True)

    def signed_digit(u):
        return (u - 0x8000).astype(I16)

    def search_digit(ref, n_above, n_ge_zero):
        def step(t, carry):
            d, n_ge = carry
            cand = d | jnp.left_shift(jnp.int32(1), 15 - t)
            cand_s = signed_digit(cand)
            cnt = n_above + count_packed(ref, lambda x: x >= cand_s)
            keep = cnt >= n_sel
            return jnp.where(keep, cand, d), jnp.where(keep, cnt, n_ge)
        return lax.fori_loop(0, 16, step, (jnp.zeros((1, QB), I32), n_ge_zero))

    stored = jnp.full((1, QB), (n_pairs * (2 * KC)).astype(F32), F32)
    u_hi, n_ge = search_digit(hi_ref, 0.0, stored)
    thr_hi = signed_digit(u_hi)
    n_above = count_packed(hi_ref, lambda x: x > thr_hi)
    lowest = jnp.full((SUB, QB), -0x8000, I16)

    def low_digits_of_ties(c2, carry):
        off = pl.multiple_of(c2 * (2 * KC), 2 * KC)
        for sub in range(2 * n_sub):
            rows = pl.ds(off + sub * SUB, SUB)
            digit_ref[rows, :] = jnp.where(hi_ref[rows, :] == thr_hi, lo_ref[rows, :], lowest)
        return carry

    lax.fori_loop(0, n_pairs, low_digits_of_ties, 0)
    u_lo, n_ge = search_digit(digit_ref, n_above, n_ge)
    thr = ((u_hi - 0x8000) << 16) | u_lo

    needs_cut = (n_ge > n_sel) & (thr > KEY_NEG_INF)
    any_ties = jnp.max(jnp.where(needs_cut, 1.0, 0.0)) > 0.0

    row_head = lax.broadcasted_iota(I32, (DSA_W, 1), 0) // HEAD_DIM
    dsaqT = dsaqT_ref[0] * (HEAD_DIM ** -0.5)
    for h in range(DSA_HEADS):
        qh_ref[h] = jnp.where(row_head == h, dsaqT, 0.0).astype(BF16)
    acc_ref[...] = jnp.zeros_like(acc_ref)
    p_ref[...] = jnp.zeros_like(p_ref)

    thr_floor = jnp.maximum(thr, KEY_NEG_INF + 1)

    def write_bias(slot, c):
        off = pl.multiple_of(c * KC, KC)
        for sub in range(n_sub):
            key = key_ref[pl.ds(off + sub * SUB, SUB), :]
            bias_ref[slot, sub * SUB:(sub + 1) * SUB, :] = jnp.where(key >= thr_floor, 0.0, MASKED_SCORE)

    def write_bias_ties(slot, c, tie_take, taken):
        off = pl.multiple_of(c * KC, KC)
        for sub in range(n_sub):
            key = key_ref[pl.ds(off + sub * SUB, SUB), :]
            tied_ref[sub * SUB:(sub + 1) * SUB, :] = jnp.where(key == thr, 1.0, 0.0).astype(BF16)
        s_ref[slot, 0] = _dot(tri_ref[...], tied_ref[...])
        room = tie_take - taken
        for sub in range(n_sub):
            rows = slice(sub * SUB, (sub + 1) * SUB)
            key = key_ref[pl.ds(off + sub * SUB, SUB), :]
            before = s_ref[slot, 0, rows, :] - tied_ref[rows, :].astype(F32)
            tied_bias = jnp.where(before < room, 0.0, MASKED_SCORE)
            bias_ref[slot, rows, :] = jnp.where(key > thr, 0.0, jnp.where(key == thr, tied_bias, MASKED_SCORE))
        return taken + s_ref[slot, 0, KC - 1:KC, :]

    def issue_scores(slot, c):
        kch = k_ref[pl.ds(pl.multiple_of(c * KC, KC), KC), :]
        for h in range(DSA_HEADS):
            s_ref[slot, h] = _dot(kch, qh_ref[h])

    def pv_update(slot, c, alphas):
        vch = vT_ref[c]
        pvs = [_dot(vch[h * HEAD_DIM:(h + 1) * HEAD_DIM, :], p_ref[slot, h]) for h in range(DSA_HEADS)]
        for h in range(DSA_HEADS):
            acc_ref[h] = alphas[h] * acc_ref[h] + pvs[h]

    def softmax_chunk(slot, ms, ls):
        new_m, new_l, alphas = [], [], []
        for h in range(DSA_HEADS):
            mx = jnp.full((SUB, QB), MASKED_SCORE, F32)
            for sub in range(n_sub):
                rows = slice(sub * SUB, (sub + 1) * SUB)
                t = s_ref[slot, h, rows, :] + bias_ref[slot, rows, :]
                s_ref[slot, h, rows, :] = t
                mx = jnp.maximum(mx, t)
            m_new = jnp.maximum(ms[h], jnp.max(mx, axis=0, keepdims=True))
            alpha = jnp.exp(ms[h] - m_new)
            psum = jnp.zeros((SUB, QB), F32)
            for sub in range(n_sub):
                rows = slice(sub * SUB, (sub + 1) * SUB)
                p = jnp.exp(s_ref[slot, h, rows, :] - m_new)
                psum = psum + p
                p_ref[slot, h, rows, :] = p.astype(BF16)
            new_m.append(m_new)
            new_l.append(alpha * ls[h] + jnp.sum(psum, axis=0, keepdims=True))
            alphas.append(alpha)
        return tuple(new_m), tuple(new_l), tuple(alphas)

    def attend(ties):
        if ties:
            thr_lo = signed_digit(u_lo)
            n_gt = n_above + count_packed(digit_ref, lambda x: x > thr_lo)
            tie_take = jnp.where(thr > KEY_NEG_INF, n_sel - n_gt, 0.0)
            ri = lax.broadcasted_iota(I32, (KC, KC), 0)
            ci = lax.broadcasted_iota(I32, (KC, KC), 1)
            tri_ref[...] = jnp.where(ci <= ri, 1.0, 0.0).astype(BF16)

        def attend_pair(c2, carry):
            ms, ls, prev_alphas, taken = carry
            ca, cb = 2 * c2, 2 * c2 + 1
            if ties:
                taken = write_bias_ties(0, ca, tie_take, taken)
                taken = write_bias_ties(1, cb, tie_take, taken)
            else:
                write_bias(0, ca)
                write_bias(1, cb)
            pv_update(1, jnp.maximum(ca - 1, 0), prev_alphas)
            issue_scores(0, ca)
            issue_scores(1, cb)
            ms, ls, alphas_a = softmax_chunk(0, ms, ls)
            pv_update(0, ca, alphas_a)
            ms, ls, alphas_b = softmax_chunk(1, ms, ls)
            return ms, ls, alphas_b, taken

        init = (tuple(jnp.full((1, QB), MASKED_SCORE, F32) for _ in range(DSA_HEADS)),
                tuple(jnp.zeros((1, QB), F32) for _ in range(DSA_HEADS)),
                tuple(jnp.ones((1, QB), F32) for _ in range(DSA_HEADS)),
                jnp.zeros((1, QB), F32))
        _, ls, last_alphas, _ = lax.fori_loop(0, n_pairs, attend_pair, init)
        pv_update(1, 2 * n_pairs - 1, last_alphas)
        for h in range(DSA_HEADS):
            acc_ref[h] = acc_ref[h] / ls[h]

    pl.when(any_ties)(lambda: attend(True))
    pl.when(jnp.logical_not(any_ties))(lambda: attend(False))
    o_ref[...] = acc_ref[...].reshape(DSA_W, QB).T.astype(o_ref.dtype)


def _dsa(qidxT, dsaqT, widxT, kidx, k, vT, B, S):
    QB, KC = DSA_QB, DSA_KC
    assert QB == KC and S % (2 * KC) == 0
    n_sel = min(TOPK_MAX, S // 4)
    assert n_sel <= KC, "the first causal chunk must be able to hold every selected key"
    assert S // DSA_SUB <= 256, "packed counts are exact only up to 256 adds per accumulator lane"
    nq = S // QB
    nkc = S // KC
    n_sel = min(TOPK_MAX, S // 4)
    T = B * S
    kern = functools.partial(_dsa_kernel, n_sel=n_sel, S=S)
    return pl.pallas_call(
        kern,
        grid=(B, nq),
        in_specs=[
            pl.BlockSpec((1, IDX_HEADS * IDX_DIM, QB), lambda b, i: (b, 0, i)),
            pl.BlockSpec((1, DSA_W, QB), lambda b, i: (b, 0, i)),
            pl.BlockSpec((1, widxT.shape[1], QB), lambda b, i: (b, 0, i)),
            _resident((S, IDX_DIM), lambda b, i: (b, 0)),
            _resident((S, DSA_W), lambda b, i: (b, 0)),
            _resident((nkc, DSA_W, KC), lambda b, i: (b, 0, 0)),
        ],
        out_specs=pl.BlockSpec((QB, DSA_W), lambda b, i: (b * nq + i, 0)),
        out_shape=jax.ShapeDtypeStruct((T, DSA_W), BF16),
        scratch_shapes=[
            pltpu.VMEM((S, QB), I32),
            pltpu.VMEM((S, QB), I16),
            pltpu.VMEM((S, QB), I16),
            pltpu.VMEM((S, QB), I16),
            pltpu.VMEM((KC, QB), BF16),
            pltpu.VMEM((KC, KC), BF16),
            pltpu.VMEM((DSA_HEADS, DSA_W, QB), BF16),
            pltpu.VMEM((2, DSA_HEADS, KC, QB), F32),
            pltpu.VMEM((2, KC, QB), F32),
            pltpu.VMEM((2, DSA_HEADS, KC, QB), BF16),
            pltpu.VMEM((DSA_HEADS, HEAD_DIM, QB), F32),
        ],
        compiler_params=_params("parallel", "arbitrary"),
        name="dsa",
    )(qidxT, dsaqT, widxT, kidx, k, vT)


def _dil_units(dilation):
    blocks = DIL_SPAN // (DIL_BLOCK * dilation)
    return [(res, jb) for res in range(dilation) for jb in range(blocks)]


def _dilated_fused_kernel(q_ref, kp_ref, kc_ref, vp_ref, vc_ref, out_ref,
                          qf_ref, kf_ref, vf_ref, s_ref, p_ref, o_ref, lse_ref):
    L, SPAN = DIL_BLOCK, DIL_SPAN
    first_span = pl.program_id(1) == 0
    qf_ref[...] = q_ref[...].astype(F32) * (HEAD_DIM ** -0.5)
    kf_ref[0:SPAN, :] = kp_ref[...].astype(F32)
    kf_ref[SPAN:, :] = kc_ref[...].astype(F32)
    vf_ref[0:SPAN, :] = vp_ref[...].astype(F32)
    vf_ref[SPAN:, :] = vc_ref[...].astype(F32)

    lane = lax.broadcasted_iota(I32, (1, 2 * HEAD_DIM), 1)
    low = lane < HEAD_DIM
    qi = lax.broadcasted_iota(I32, (L, 2 * L), 0)
    kj = lax.broadcasted_iota(I32, (L, 2 * L), 1)
    dist = L + qi - kj
    band = (dist >= 0) & (dist <= L)
    bias_band = jnp.where(band, 0.0, -jnp.inf)
    bias_head = jnp.where(band & ((kj >= L) | jnp.logical_not(first_span)), 0.0, -jnp.inf)

    def rows(start, count, stride):
        return pl.ds(start, count, stride=stride) if stride > 1 else pl.ds(start, count)

    for branch, (window, r) in enumerate(DIL_PATTERNS):
        units = _dil_units(r)
        for g0 in range(0, len(units), DIL_GROUP):
            group = units[g0:g0 + DIL_GROUP]
            vvs = []
            for u, (res, jb) in enumerate(group):
                q = qf_ref[rows(res + r * L * jb, L, r), :].astype(BF16)
                k0 = SPAN + res + r * L * (jb - 1)
                kk = kf_ref[rows(k0, 2 * L, r), :].astype(BF16)
                vv = vf_ref[rows(k0, 2 * L, r), :]
                for sub in range(2):
                    mine = low if sub == 0 else jnp.logical_not(low)
                    s_ref[2 * u + sub] = _dot_nt(jnp.where(mine, q, jnp.zeros_like(q)), kk)
                    vvs.append(jnp.where(mine, vv, 1.0).astype(BF16))
            ms = []
            for u, (res, jb) in enumerate(group):
                bias = bias_head if jb == 0 else bias_band
                for sub in range(2):
                    t = s_ref[2 * u + sub] + bias
                    m = jnp.max(t, axis=-1, keepdims=True)
                    p_ref[2 * u + sub] = jnp.exp(t - m).astype(BF16)
                    ms.append(m)
            for u, (res, jb) in enumerate(group):
                halves = []
                for sub in range(2):
                    nd = _dot(p_ref[2 * u + sub], vvs[2 * u + sub])
                    halves.append((nd, pltpu.roll(nd, HEAD_DIM, 1), ms[2 * u + sub]))
                (n0, d0, m0), (n1, d1, m1) = halves
                dst = rows(res + r * L * jb, L, r)
                o_ref[branch, dst, :] = jnp.where(low, n0 / d0, n1 / d1)
                lse_ref[branch, dst, :] = jnp.where(low, m0 + jnp.log(d0), m1 + jnp.log(d1))

    for piece in range(SPAN // DIL_MERGE_ROWS):
        rs = slice(piece * DIL_MERGE_ROWS, (piece + 1) * DIL_MERGE_ROWS)
        a, b, c = lse_ref[0, rs, :], lse_ref[1, rs, :], lse_ref[2, rs, :]
        m = jnp.maximum(jnp.maximum(a, b), c)
        ea, eb, ec = jnp.exp(a - m), jnp.exp(b - m), jnp.exp(c - m)
        mix = (ea * o_ref[0, rs, :] + eb * o_ref[1, rs, :] + ec * o_ref[2, rs, :]) / (ea + eb + ec)
        out_ref[rs, :] = mix.astype(out_ref.dtype)


def _dilated_fused(q, k, v, B, S):
    T, W = q.shape
    SPAN, L = DIL_SPAN, DIL_BLOCK
    for window, r in DIL_PATTERNS:
        assert window // r == L and SPAN % (L * r) == 0 and L * r <= SPAN
    assert S % SPAN == 0
    ns = S // SPAN
    PW = 2 * HEAD_DIM
    own = lambda b, t, hp: (b * ns + t, hp)
    prev = lambda b, t, hp: (b * ns + jnp.maximum(t - 1, 0), hp)
    blk = (SPAN, PW)
    return pl.pallas_call(
        _dilated_fused_kernel,
        grid=(B, ns, W // PW),
        in_specs=[pl.BlockSpec(blk, own), pl.BlockSpec(blk, prev), pl.BlockSpec(blk, own),
                  pl.BlockSpec(blk, prev), pl.BlockSpec(blk, own)],
        out_specs=pl.BlockSpec(blk, own),
        out_shape=jax.ShapeDtypeStruct((T, W), BF16),
        scratch_shapes=[
            pltpu.VMEM((SPAN, PW), F32),
            pltpu.VMEM((2 * SPAN, PW), F32),
            pltpu.VMEM((2 * SPAN, PW), F32),
            pltpu.VMEM((2 * DIL_GROUP, L, 2 * L), F32),
            pltpu.VMEM((2 * DIL_GROUP, L, 2 * L), BF16),
            pltpu.VMEM((len(DIL_PATTERNS), SPAN, PW), F32),
            pltpu.VMEM((len(DIL_PATTERNS), SPAN, PW), F32),
        ],
        compiler_params=_params("parallel", "parallel", "parallel"),
        name="dilated",
    )(q, k, k, v, v)


def _mlp_kernel(x_ref, ret_ref, dsa_ref, dil_ref, wout_ref, nw_ref, wup_ref, wdown_ref, fnw_ref, o_ref, *,
                apply_final_norm):
    mixed = (_dot(ret_ref[...], wout_ref[0:RET_W, :])
             + _dot(dsa_ref[...], wout_ref[RET_W:RET_W + DSA_W, :])
             + _dot(dil_ref[...], wout_ref[RET_W + DSA_W:, :]))
    x = x_ref[...] + mixed
    h = _rms(x, nw_ref[...]).astype(BF16)
    ff = None
    for c in range(wup_ref.shape[1] // FF_CHUNK):
        cs = slice(c * FF_CHUNK, (c + 1) * FF_CHUNK)
        u = jnp.maximum(_dot(h, wup_ref[:, cs]), 0.0)
        part = _dot((u * u).astype(BF16), wdown_ref[cs, :])
        ff = part if ff is None else ff + part
    out = x + ff
    o_ref[...] = _rms(out, fnw_ref[...]) if apply_final_norm else out


def _mlp(x2, ret_o, dsa_o, dil_o, w_out, nw, w_up, w_down, final_nw, *, apply_final_norm):
    T, D = x2.shape
    tm = ROW_TILE
    row = lambda i: (i, 0)
    const = lambda i: (0, 0)
    return pl.pallas_call(
        functools.partial(_mlp_kernel, apply_final_norm=apply_final_norm),
        grid=(T // tm,),
        in_specs=[
            pl.BlockSpec((tm, D), row),
            pl.BlockSpec((tm, RET_W), row),
            pl.BlockSpec((tm, DSA_W), row),
            pl.BlockSpec((tm, DIL_W), row),
            _resident(w_out.shape, const),
            _resident((1, D), const),
            _resident(w_up.shape, const),
            _resident(w_down.shape, const),
            _resident((1, D), const),
        ],
        out_specs=pl.BlockSpec((tm, D), row),
        out_shape=jax.ShapeDtypeStruct((T, D), F32),
        compiler_params=_params("parallel"),
        name="outproj_mlp",
    )(x2, ret_o, dsa_o, dil_o, w_out, nw, w_up, w_down, final_nw)


def _split_w_in(w_in):
    ret_cols = 4 * RET_W
    o = ret_cols
    dsa_q = w_in[:, o:o + DSA_W]; o += DSA_W
    c_kv = w_in[:, o:o + KV_LATENT]; o += KV_LATENT
    q_idx = w_in[:, o:o + IDX_HEADS * IDX_DIM]; o += IDX_HEADS * IDX_DIM
    k_idx = w_in[:, o:o + IDX_DIM]; o += IDX_DIM
    w_idx = w_in[:, o:o + IDX_HEADS]; o += IDX_HEADS
    dil = w_in[:, o:]
    w_main = jnp.concatenate([w_in[:, :ret_cols], dil, k_idx, c_kv], axis=1).astype(BF16)
    pad = jnp.zeros((w_in.shape[0], WIDX_ROWS - IDX_HEADS), w_in.dtype)
    w_t = jnp.concatenate([q_idx, dsa_q, w_idx, pad], axis=1).T.astype(BF16)
    return w_main, w_t


def kernel(x, attn_norm_w, w_in, ret_norm_w, dsa_kv_norm_w, dsa_w_uk, dsa_w_uv, w_out, mlp_norm_w, w_up, w_down, final_norm_w):
    B, S, D = x.shape
    depth = w_in.shape[0]
    assert S % ROW_TILE == 0 and S % DSA_KC == 0 and S % RET_CHUNK == 0
    tables = _retention_tables(S)
    x2 = x.reshape(B * S, D)
    for layer in range(depth):
        w_main, w_t = _split_w_in(w_in[layer])
        ret_cols, dq, dk, dv, kidx, qidxT, dsaqT, widxT, k, vT = _inproj(
            x2, attn_norm_w[layer][None, :], w_main, w_t, dsa_kv_norm_w[layer][None, :],
            dsa_w_uk[layer].astype(BF16), dsa_w_uv[layer].T.astype(BF16), B, S)
        ret_o = _retention(ret_cols, tables, ret_norm_w[layer].reshape(1, RET_W), B, S)
        dsa_o = _dsa(qidxT, dsaqT, widxT, kidx, k, vT, B, S)
        dil_o = _dilated_fused(dq, dk, dv, B, S)
        x2 = _mlp(x2, ret_o, dsa_o, dil_o, w_out[layer].astype(BF16), mlp_norm_w[layer][None, :],
                  w_up[layer].astype(BF16), w_down[layer].astype(BF16), final_norm_w[None, :],
                  apply_final_norm=layer == depth - 1)
    return x2.reshape(B, S, D)
```

```python
import functools
import math

import numpy as np
import jax
import jax.numpy as jnp
from jax import lax
from jax.experimental import pallas as pl
from jax.experimental.pallas import tpu as pltpu

F32 = jnp.float32
BF16 = jnp.bfloat16
I32 = jnp.int32
I16 = jnp.int16

HEAD_DIM = 64
RET_HEADS = 4
DSA_HEADS = 4
DIL_HEADS = 8
RET_W = RET_HEADS * HEAD_DIM
DSA_W = DSA_HEADS * HEAD_DIM
DIL_W = DIL_HEADS * HEAD_DIM
RET_CHUNK = 128
RET_STEP_CHUNKS = 8
RET_ROPE_BASE = 10000.0
KV_LATENT = 128
IDX_HEADS = 8
IDX_DIM = 64
TOPK_MAX = 256
DIL_PATTERNS = ((128, 1), (512, 4), (2048, 16))
DIL_BLOCK = 128
DIL_SPAN = 2048
DIL_GROUP = 4
DIL_MERGE_ROWS = 256
NORM_EPS = 1e-6

V7X_VMEM_LIMIT_BYTES = 56 * 1024 * 1024

ROW_TILE = 512
DSA_QB = 256
DSA_KC = 256
DSA_SUB = 64
DSA_SCORE_ROWS = 128
DSA_IDX_GROUP = 4
FF_CHUNK = 1024

WIDX_ROWS = 16

INT_MIN = -(2 ** 31)
KEY_NEG_INF = int(np.int32(np.uint32(0xFF800000) ^ np.uint32(0x7FFFFFFF)))
MASKED_SCORE = -1e30


def _params(*semantics):
    return pltpu.CompilerParams(dimension_semantics=semantics, vmem_limit_bytes=V7X_VMEM_LIMIT_BYTES)


def _resident(shape, index_map):
    return pl.BlockSpec(shape, index_map, pipeline_mode=pl.Buffered(1))


def _dot(a, b):
    return jnp.dot(a, b, preferred_element_type=F32)


def _dot_nt(a, b):
    return lax.dot_general(a, b, (((1,), (1,)), ((), ())), preferred_element_type=F32)


def _rms(x, w):
    return x * lax.rsqrt(jnp.mean(x * x, axis=-1, keepdims=True) + NORM_EPS) * w


def _inproj_kernel(x_ref, nw_ref, w_ref, wt_ref, kvn_ref, wuk_ref, wuvT_ref,
                   ret_ref, dq_ref, dk_ref, dv_ref, kidx_ref, qidxT_ref, dsaqT_ref, widxT_ref, k_ref, vT_ref):
    h = _rms(x_ref[...], nw_ref[...]).astype(BF16)
    o = 0
    for ref in (ret_ref, dq_ref, dk_ref, dv_ref, kidx_ref):
        n = ref.shape[-1]
        ref[...] = _dot(h, w_ref[:, o:o + n]).astype(ref.dtype)
        o += n
    c = _rms(_dot(h, w_ref[:, o:o + KV_LATENT]), kvn_ref[...]).astype(BF16)
    for hd in range(DSA_HEADS):
        k_ref[hd] = _dot(c, wuk_ref[:, hd * HEAD_DIM:(hd + 1) * HEAD_DIM]).astype(k_ref.dtype)
    for j in range(vT_ref.shape[0]):
        vT_ref[j] = _dot_nt(wuvT_ref[...], c[j * DSA_KC:(j + 1) * DSA_KC, :]).astype(vT_ref.dtype)
    o = 0
    for ref in (qidxT_ref, dsaqT_ref, widxT_ref):
        n = ref.shape[-2]
        ref[0] = _dot_nt(wt_ref[o:o + n, :], h).astype(ref.dtype)
        o += n


def _inproj(x2, nw, w_main, w_t, kvn, w_uk, w_uvT, B, S):
    T, D = x2.shape
    tm = ROW_TILE
    nt = S // tm
    assert tm % DSA_KC == 0
    widx_rows = w_t.shape[0] - IDX_HEADS * IDX_DIM - DSA_W
    row = lambda i: (i, 0)
    tr = lambda i: (i // nt, 0, i % nt)
    const = lambda i: (0, 0)
    out_shape = (
        jax.ShapeDtypeStruct((T, 4 * RET_W), F32),
        jax.ShapeDtypeStruct((T, DIL_W), BF16),
        jax.ShapeDtypeStruct((T, DIL_W), BF16),
        jax.ShapeDtypeStruct((T, DIL_W), BF16),
        jax.ShapeDtypeStruct((T, IDX_DIM), BF16),
        jax.ShapeDtypeStruct((B, IDX_HEADS * IDX_DIM, S), BF16),
        jax.ShapeDtypeStruct((B, DSA_W, S), BF16),
        jax.ShapeDtypeStruct((B, widx_rows, S), F32),
        jax.ShapeDtypeStruct((DSA_HEADS, T, HEAD_DIM), BF16),
        jax.ShapeDtypeStruct((T // DSA_KC, DSA_W, DSA_KC), BF16),
    )
    out_specs = (
        pl.BlockSpec((tm, 4 * RET_W), row),
        pl.BlockSpec((tm, DIL_W), row),
        pl.BlockSpec((tm, DIL_W), row),
        pl.BlockSpec((tm, DIL_W), row),
        pl.BlockSpec((tm, IDX_DIM), row),
        pl.BlockSpec((1, IDX_HEADS * IDX_DIM, tm), tr),
        pl.BlockSpec((1, DSA_W, tm), tr),
        pl.BlockSpec((1, widx_rows, tm), tr),
        pl.BlockSpec((DSA_HEADS, tm, HEAD_DIM), lambda i: (0, i, 0)),
        pl.BlockSpec((tm // DSA_KC, DSA_W, DSA_KC), lambda i: (i, 0, 0)),
    )
    return pl.pallas_call(
        _inproj_kernel,
        grid=(T // tm,),
        in_specs=[
            pl.BlockSpec((tm, D), row),
            _resident((1, D), const),
            _resident(w_main.shape, const),
            _resident(w_t.shape, const),
            _resident((1, KV_LATENT), const),
            _resident(w_uk.shape, const),
            _resident(w_uvT.shape, const),
        ],
        out_specs=out_specs,
        out_shape=out_shape,
        compiler_params=_params("parallel"),
        name="inproj",
    )(x2, nw, w_main, w_t, kvn, w_uk, w_uvT)


def _retention_tables(S):
    half = HEAD_DIM // 2
    inv = RET_ROPE_BASE ** (-jnp.arange(half, dtype=F32) / half)
    ang = jnp.arange(S, dtype=F32)[:, None] * inv[None, :]
    cos, sin = jnp.cos(ang), jnp.sin(ang)
    cos_h = jnp.concatenate([cos, cos], axis=-1)
    sin_h = jnp.concatenate([-sin, sin], axis=-1)
    cosf = jnp.tile(cos_h, (1, RET_HEADS))
    sinf = jnp.tile(sin_h, (1, RET_HEADS))
    C = RET_CHUNK
    log_g = np.log(1.0 - 2.0 ** (-5.0 - np.arange(RET_HEADS, dtype=np.float64)))
    i = np.arange(C, dtype=np.float64)
    diff = i[:, None] - i[None, :]
    decay = np.exp(np.maximum(diff, 0.0)[None] * log_g[:, None, None]) * (diff >= 0)[None]
    zeta = np.exp((C - 1.0 - i)[None, :] * log_g[:, None])
    xi = np.exp((i + 1.0)[None, :] * log_g[:, None])
    g_chunk = np.exp(C * log_g)
    rep = lambda t: np.repeat(t.T, HEAD_DIM, axis=1)
    g_rows = np.repeat(g_chunk, HEAD_DIM)[:, None] * np.ones((1, RET_W))
    return (cosf, sinf, jnp.asarray(decay, F32), jnp.asarray(rep(zeta), F32), jnp.asarray(rep(xi), F32),
            jnp.asarray(g_rows, F32))


def _retention_kernel(cols_ref, cos_ref, sin_ref, decay_ref, zeta_ref, xi_ref, grow_ref, nw_ref, o_ref, state_ref):
    W = RET_W

    @pl.when(pl.program_id(1) == 0)
    def _():
        state_ref[...] = jnp.zeros_like(state_ref)

    C = RET_CHUNK
    lane = lax.broadcasted_iota(I32, (1, W), 1)
    first_half = (lane % HEAD_DIM) < (HEAD_DIM // 2)
    head_of_lane = lane // HEAD_DIM
    r_head = lax.broadcasted_iota(I32, (W, W), 0) // HEAD_DIM
    c_head = lax.broadcasted_iota(I32, (W, W), 1) // HEAD_DIM
    same_head = r_head == c_head

    def rot(t, rows):
        partner = jnp.where(first_half, pltpu.roll(t, W - HEAD_DIM // 2, 1), pltpu.roll(t, HEAD_DIM // 2, 1))
        return t * cos_ref[rows, :] + partner * sin_ref[rows, :]

    qs, inners, kvs = [], [], []
    for j in range(RET_STEP_CHUNKS):
        rows = slice(j * C, (j + 1) * C)
        q = rot(cols_ref[rows, 0:W], rows)
        k = rot(cols_ref[rows, W:2 * W], rows) * (HEAD_DIM ** -0.5)
        v = cols_ref[rows, 2 * W:3 * W].astype(BF16)
        kb = k.astype(BF16)
        inner = jnp.zeros((C, W), F32)
        for h in range(RET_HEADS):
            sel = head_of_lane == h
            qh = jnp.where(sel, q, 0.0).astype(BF16)
            a = (_dot_nt(qh, kb) * decay_ref[h]).astype(BF16)
            inner = inner + jnp.where(sel, _dot(a, v), 0.0)
        kzT = (k * zeta_ref[...]).T.astype(BF16)
        kv = _dot(kzT, v)
        qs.append(q.astype(BF16))
        inners.append(inner)
        kvs.append(jnp.where(same_head, kv, 0.0))

    state = state_ref[...]
    for j in range(RET_STEP_CHUNKS):
        rows = slice(j * C, (j + 1) * C)
        o = inners[j] + _dot(qs[j], state.astype(BF16)) * xi_ref[...]
        state = grow_ref[...] * state + kvs[j]

        mu = jnp.zeros_like(o)
        for h in range(RET_HEADS):
            sel = head_of_lane == h
            mu = mu + jnp.where(sel, jnp.sum(jnp.where(sel, o, 0.0), axis=-1, keepdims=True), 0.0)
        mu = mu * (1.0 / HEAD_DIM)
        d = o - mu
        var = jnp.zeros_like(o)
        for h in range(RET_HEADS):
            sel = head_of_lane == h
            var = var + jnp.where(sel, jnp.sum(jnp.where(sel, d * d, 0.0), axis=-1, keepdims=True), 0.0)
        var = var * (1.0 / HEAD_DIM)
        y = d * lax.rsqrt(var + NORM_EPS) * nw_ref[...]
        g = cols_ref[rows, 3 * W:4 * W]
        o_ref[rows, :] = (jax.nn.silu(g) * y).astype(o_ref.dtype)
    state_ref[...] = state


def _retention(ret_cols, tables, norm_w, B, S):
    T = ret_cols.shape[0]
    C, W = RET_CHUNK, RET_W
    rows = RET_STEP_CHUNKS * C
    assert S % rows == 0
    n = S // rows
    cosf, sinf, decay, zeta, xi, g_rows = tables
    tok = lambda b, j: (b * n + j, 0)
    pos = lambda b, j: (j, 0)
    const2 = lambda b, j: (0, 0)
    return pl.pallas_call(
        _retention_kernel,
        grid=(B, n),
        in_specs=[
            pl.BlockSpec((rows, 4 * W), tok),
            pl.BlockSpec((rows, W), pos),
            pl.BlockSpec((rows, W), pos),
            _resident(decay.shape, lambda b, j: (0, 0, 0)),
            _resident((C, W), const2),
            _resident((C, W), const2),
            _resident((W, W), const2),
            _resident((1, W), const2),
        ],
        out_specs=pl.BlockSpec((rows, W), tok),
        out_shape=jax.ShapeDtypeStruct((T, W), BF16),
        scratch_shapes=[pltpu.VMEM((W, W), F32)],
        compiler_params=_params("parallel", "arbitrary"),
        name="retention",
    )(ret_cols, cosf, sinf, decay, zeta, xi, g_rows, norm_w)


def _sortable(x):
    b = pltpu.bitcast(x, I32)
    return b ^ ((b >> 31) & 0x7FFFFFFF)


def _dsa_kernel(qidxT_ref, dsaqT_ref, widxT_ref, kidx_ref, k_ref, vT_ref, o_ref,
                key_ref, hi_ref, lo_ref, digit_ref, tied_ref, tri_ref, qh_ref, s_ref, bias_ref, p_ref, acc_ref,
                *, n_sel, S):
    QB, KC, SUB = DSA_QB, DSA_KC, DSA_SUB
    n_sub = KC // SUB
    i = pl.program_id(1)
    n_chunks = i + 1
    n_pairs = (n_chunks + 1) // 2
    qpos = i * QB + lax.broadcasted_iota(I32, (1, QB), 1)
    krow = lax.broadcasted_iota(I32, (SUB, 1), 0)
    w_scale = IDX_HEADS ** -0.5 * IDX_DIM ** -0.5
    widxT = widxT_ref[0] * w_scale

    def score_chunk(c, carry):
        off = pl.multiple_of(c * (2 * KC), 2 * KC)
        for sub in range(2 * KC // DSA_SCORE_ROWS):
            r0 = off + sub * DSA_SCORE_ROWS
            kc = kidx_ref[pl.ds(r0, DSA_SCORE_ROWS), :]
            acc = jnp.zeros((DSA_SCORE_ROWS, QB), F32)
            for g in range(0, IDX_HEADS, DSA_IDX_GROUP):
                logits = [_dot(kc, qidxT_ref[0, h * IDX_DIM:(h + 1) * IDX_DIM, :])
                          for h in range(g, g + DSA_IDX_GROUP)]
                for h, logit in zip(range(g, g + DSA_IDX_GROUP), logits):
                    acc = acc + jnp.maximum(logit, 0.0) * widxT[h:h + 1, :]
            for part in range(DSA_SCORE_ROWS // SUB):
                a = acc[part * SUB:(part + 1) * SUB, :]
                a = jnp.where(a == 0.0, 0.0, a)
                score = jnp.where(r0 + part * SUB + krow <= qpos, a, -jnp.inf)
                key = _sortable(score)
                rows = pl.ds(r0 + part * SUB, SUB)
                key_ref[rows, :] = key
                hi_ref[rows, :] = (key >> 16).astype(I16)
                lo_ref[rows, :] = ((key & 0xFFFF) - 0x8000).astype(I16)
        return carry

    lax.fori_loop(0, n_pairs, score_chunk, 0)

    one_b, zero_b = jnp.ones((SUB, QB), BF16), jnp.zeros((SUB, QB), BF16)

    def count_packed(ref, pred):
        def body(c2, cnt):
            off = pl.multiple_of(c2 * (2 * KC), 2 * KC)
            for sub in range(2 * n_sub):
                cnt = cnt + jnp.where(pred(ref[pl.ds(off + sub * SUB, SUB), :]), one_b, zero_b)
            return cnt
        cnt = lax.fori_loop(0, n_pairs, body, zero_b)
        return jnp.sum(cnt.astype(F32), axis=0, keepdims=True)

    def signed_digit(u):
        return (u - 0x8000).astype(I16)

    def search_digit(ref, n_above, n_ge_zero):
        def step(t, carry):
            d, n_ge = carry
            cand = d | jnp.left_shift(jnp.int32(1), 15 - t)
            cand_s = signed_digit(cand)
            cnt = n_above + count_packed(ref, lambda x: x >= cand_s)
            keep = cnt >= n_sel
            return jnp.where(keep, cand, d), jnp.where(keep, cnt, n_ge)
        return lax.fori_loop(0, 16, step, (jnp.zeros((1, QB), I32), n_ge_zero))

    stored = jnp.full((1, QB), (n_pairs * (2 * KC)).astype(F32), F32)
    u_hi, n_ge = search_digit(hi_ref, 0.0, stored)
    thr_hi = signed_digit(u_hi)
    n_above = count_packed(hi_ref, lambda x: x > thr_hi)
    lowest = jnp.full((SUB, QB), -0x8000, I16)

    def low_digits_of_ties(c2, carry):
        off = pl.multiple_of(c2 * (2 * KC), 2 * KC)
        for sub in range(2 * n_sub):
            rows = pl.ds(off + sub * SUB, SUB)
            digit_ref[rows, :] = jnp.where(hi_ref[rows, :] == thr_hi, lo_ref[rows, :], lowest)
        return carry

    lax.fori_loop(0, n_pairs, low_digits_of_ties, 0)
    u_lo, n_ge = search_digit(digit_ref, n_above, n_ge)
    thr = ((u_hi - 0x8000) << 16) | u_lo

    needs_cut = (n_ge > n_sel) & (thr > KEY_NEG_INF)
    any_ties = jnp.max(jnp.where(needs_cut, 1.0, 0.0)) > 0.0

    qh_ref[...] = (dsaqT_ref[0] * (HEAD_DIM ** -0.5)).astype(BF16)
    acc_ref[...] = jnp.zeros_like(acc_ref)
    p_ref[...] = jnp.zeros_like(p_ref)

    thr_floor = jnp.maximum(thr, KEY_NEG_INF + 1)

    def write_bias(slot, c):
        off = pl.multiple_of(c * KC, KC)
        for sub in range(n_sub):
            key = key_ref[pl.ds(off + sub * SUB, SUB), :]
            bias_ref[slot, sub * SUB:(sub + 1) * SUB, :] = jnp.where(key >= thr_floor, 0.0, MASKED_SCORE)

    def write_bias_ties(slot, c, tie_take, taken):
        off = pl.multiple_of(c * KC, KC)
        for sub in range(n_sub):
            key = key_ref[pl.ds(off + sub * SUB, SUB), :]
            tied_ref[sub * SUB:(sub + 1) * SUB, :] = jnp.where(key == thr, 1.0, 0.0).astype(BF16)
        s_ref[slot, 0] = _dot(tri_ref[...], tied_ref[...])
        room = tie_take - taken
        for sub in range(n_sub):
            rows = slice(sub * SUB, (sub + 1) * SUB)
            key = key_ref[pl.ds(off + sub * SUB, SUB), :]
            before = s_ref[slot, 0, rows, :] - tied_ref[rows, :].astype(F32)
            tied_bias = jnp.where(before < room, 0.0, MASKED_SCORE)
            bias_ref[slot, rows, :] = jnp.where(key > thr, 0.0, jnp.where(key == thr, tied_bias, MASKED_SCORE))
        return taken + s_ref[slot, 0, KC - 1:KC, :]

    def issue_scores(slot, c):
        rows = pl.ds(pl.multiple_of(c * KC, KC), KC)
        for h in range(DSA_HEADS):
            s_ref[slot, h] = _dot(k_ref[h, rows, :], qh_ref[h * HEAD_DIM:(h + 1) * HEAD_DIM, :])

    def pv_update(slot, c, alphas):
        vch = vT_ref[c]
        pvs = [_dot(vch[h * HEAD_DIM:(h + 1) * HEAD_DIM, :], p_ref[slot, h]) for h in range(DSA_HEADS)]
        for h in range(DSA_HEADS):
            acc_ref[h] = alphas[h] * acc_ref[h] + pvs[h]

    def softmax_chunk(slot, ms, ls):
        new_m, new_l, alphas = [], [], []
        for h in range(DSA_HEADS):
            mx = jnp.full((SUB, QB), MASKED_SCORE, F32)
            for sub in range(n_sub):
                rows = slice(sub * SUB, (sub + 1) * SUB)
                t = s_ref[slot, h, rows, :] + bias_ref[slot, rows, :]
                s_ref[slot, h, rows, :] = t
                mx = jnp.maximum(mx, t)
            m_new = jnp.maximum(ms[h], jnp.max(mx, axis=0, keepdims=True))
            alpha = jnp.exp(ms[h] - m_new)
            psum = jnp.zeros((SUB, QB), F32)
            for sub in range(n_sub):
                rows = slice(sub * SUB, (sub + 1) * SUB)
                p = jnp.exp(s_ref[slot, h, rows, :] - m_new)
                psum = psum + p
                p_ref[slot, h, rows, :] = p.astype(BF16)
            new_m.append(m_new)
            new_l.append(alpha * ls[h] + jnp.sum(psum, axis=0, keepdims=True))
            alphas.append(alpha)
        return tuple(new_m), tuple(new_l), tuple(alphas)

    def attend(ties):
        if ties:
            thr_lo = signed_digit(u_lo)
            n_gt = n_above + count_packed(digit_ref, lambda x: x > thr_lo)
            tie_take = jnp.where(thr > KEY_NEG_INF, n_sel - n_gt, 0.0)
            ri = lax.broadcasted_iota(I32, (KC, KC), 0)
            ci = lax.broadcasted_iota(I32, (KC, KC), 1)
            tri_ref[...] = jnp.where(ci <= ri, 1.0, 0.0).astype(BF16)

        def attend_pair(c2, carry):
            ms, ls, prev_alphas, taken = carry
            ca, cb = 2 * c2, 2 * c2 + 1
            if ties:
                taken = write_bias_ties(0, ca, tie_take, taken)
                taken = write_bias_ties(1, cb, tie_take, taken)
            else:
                write_bias(0, ca)
                write_bias(1, cb)
            issue_scores(0, ca)
            pv_update(1, jnp.maximum(ca - 1, 0), prev_alphas)
            issue_scores(1, cb)
            ms, ls, alphas_a = softmax_chunk(0, ms, ls)
            pv_update(0, ca, alphas_a)
            ms, ls, alphas_b = softmax_chunk(1, ms, ls)
            return ms, ls, alphas_b, taken

        init = (tuple(jnp.full((1, QB), MASKED_SCORE, F32) for _ in range(DSA_HEADS)),
                tuple(jnp.zeros((1, QB), F32) for _ in range(DSA_HEADS)),
                tuple(jnp.ones((1, QB), F32) for _ in range(DSA_HEADS)),
                jnp.zeros((1, QB), F32))
        _, ls, last_alphas, _ = lax.fori_loop(0, n_pairs, attend_pair, init)
        pv_update(1, 2 * n_pairs - 1, last_alphas)
        for h in range(DSA_HEADS):
            acc_ref[h] = acc_ref[h] / ls[h]

    pl.when(any_ties)(lambda: attend(True))
    pl.when(jnp.logical_not(any_ties))(lambda: attend(False))
    o_ref[...] = acc_ref[...].reshape(DSA_W, QB).T.astype(o_ref.dtype)


def _dsa(qidxT, dsaqT, widxT, kidx, k, vT, B, S):
    QB, KC = DSA_QB, DSA_KC
    assert QB == KC and S % (2 * KC) == 0
    n_sel = min(TOPK_MAX, S // 4)
    assert n_sel <= KC, "the first causal chunk must be able to hold every selected key"
    assert S // DSA_SUB <= 256, "packed counts are exact only up to 256 adds per accumulator lane"
    nq = S // QB
    nkc = S // KC
    n_sel = min(TOPK_MAX, S // 4)
    T = B * S
    kern = functools.partial(_dsa_kernel, n_sel=n_sel, S=S)
    return pl.pallas_call(
        kern,
        grid=(B, nq),
        in_specs=[
            pl.BlockSpec((1, IDX_HEADS * IDX_DIM, QB), lambda b, i: (b, 0, i)),
            pl.BlockSpec((1, DSA_W, QB), lambda b, i: (b, 0, i)),
            pl.BlockSpec((1, widxT.shape[1], QB), lambda b, i: (b, 0, i)),
            _resident((S, IDX_DIM), lambda b, i: (b, 0)),
            _resident((DSA_HEADS, S, HEAD_DIM), lambda b, i: (0, b, 0)),
            _resident((nkc, DSA_W, KC), lambda b, i: (b, 0, 0)),
        ],
        out_specs=pl.BlockSpec((QB, DSA_W), lambda b, i: (b * nq + i, 0)),
        out_shape=jax.ShapeDtypeStruct((T, DSA_W), BF16),
        scratch_shapes=[
            pltpu.VMEM((S, QB), I32),
            pltpu.VMEM((S, QB), I16),
            pltpu.VMEM((S, QB), I16),
            pltpu.VMEM((S, QB), I16),
            pltpu.VMEM((KC, QB), BF16),
            pltpu.VMEM((KC, KC), BF16),
            pltpu.VMEM((DSA_W, QB), BF16),
            pltpu.VMEM((2, DSA_HEADS, KC, QB), F32),
            pltpu.VMEM((2, KC, QB), F32),
            pltpu.VMEM((2, DSA_HEADS, KC, QB), BF16),
            pltpu.VMEM((DSA_HEADS, HEAD_DIM, QB), F32),
        ],
        compiler_params=_params("parallel", "arbitrary"),
        name="dsa",
    )(qidxT, dsaqT, widxT, kidx, k, vT)


def _dil_units(dilation):
    blocks = DIL_SPAN // (DIL_BLOCK * dilation)
    return [(res, jb) for res in range(dilation) for jb in range(blocks)]


def _dilated_fused_kernel(q_ref, kp_ref, kc_ref, vp_ref, vc_ref, out_ref,
                          qf_ref, kf_ref, vf_ref, s_ref, p_ref, o_ref, lse_ref):
    L, SPAN = DIL_BLOCK, DIL_SPAN
    first_span = pl.program_id(1) == 0
    qf_ref[...] = q_ref[...].astype(F32) * (HEAD_DIM ** -0.5)
    kf_ref[0:SPAN, :] = kp_ref[...].astype(F32)
    kf_ref[SPAN:, :] = kc_ref[...].astype(F32)
    vf_ref[0:SPAN, :] = vp_ref[...].astype(F32)
    vf_ref[SPAN:, :] = vc_ref[...].astype(F32)

    lane = lax.broadcasted_iota(I32, (1, 2 * HEAD_DIM), 1)
    low = lane < HEAD_DIM
    qi = lax.broadcasted_iota(I32, (L, 2 * L), 0)
    kj = lax.broadcasted_iota(I32, (L, 2 * L), 1)
    dist = L + qi - kj
    band = (dist >= 0) & (dist <= L)
    bias_band = jnp.where(band, 0.0, -jnp.inf)
    bias_head = jnp.where(band & ((kj >= L) | jnp.logical_not(first_span)), 0.0, -jnp.inf)

    def rows(start, count, stride):
        return pl.ds(start, count, stride=stride) if stride > 1 else pl.ds(start, count)

    for branch, (window, r) in enumerate(DIL_PATTERNS):
        units = _dil_units(r)
        for g0 in range(0, len(units), DIL_GROUP):
            group = units[g0:g0 + DIL_GROUP]
            vvs = []
            for u, (res, jb) in enumerate(group):
                q = qf_ref[rows(res + r * L * jb, L, r), :].astype(BF16)
                k0 = SPAN + res + r * L * (jb - 1)
                kk = kf_ref[rows(k0, 2 * L, r), :].astype(BF16)
                vv = vf_ref[rows(k0, 2 * L, r), :]
                for sub in range(2):
                    mine = low if sub == 0 else jnp.logical_not(low)
                    s_ref[2 * u + sub] = _dot_nt(jnp.where(mine, q, jnp.zeros_like(q)), kk)
                    vvs.append(jnp.where(mine, vv, 1.0).astype(BF16))
            ms = []
            for u, (res, jb) in enumerate(group):
                bias = bias_head if jb == 0 else bias_band
                for sub in range(2):
                    t = s_ref[2 * u + sub] + bias
                    m = jnp.max(t, axis=-1, keepdims=True)
                    p_ref[2 * u + sub] = jnp.exp(t - m).astype(BF16)
                    ms.append(m)
            for u, (res, jb) in enumerate(group):
                halves = []
                for sub in range(2):
                    nd = _dot(p_ref[2 * u + sub], vvs[2 * u + sub])
                    halves.append((nd, pltpu.roll(nd, HEAD_DIM, 1), ms[2 * u + sub]))
                (n0, d0, m0), (n1, d1, m1) = halves
                dst = rows(res + r * L * jb, L, r)
                o_ref[branch, dst, :] = jnp.where(low, n0 / d0, n1 / d1)
                lse_ref[branch, dst, :] = jnp.where(low, m0 + jnp.log(d0), m1 + jnp.log(d1))

    for piece in range(SPAN // DIL_MERGE_ROWS):
        rs = slice(piece * DIL_MERGE_ROWS, (piece + 1) * DIL_MERGE_ROWS)
        a, b, c = lse_ref[0, rs, :], lse_ref[1, rs, :], lse_ref[2, rs, :]
        m = jnp.maximum(jnp.maximum(a, b), c)
        ea, eb, ec = jnp.exp(a - m), jnp.exp(b - m), jnp.exp(c - m)
        mix = (ea * o_ref[0, rs, :] + eb * o_ref[1, rs, :] + ec * o_ref[2, rs, :]) / (ea + eb + ec)
        out_ref[rs, :] = mix.astype(out_ref.dtype)


def _dilated_fused(q, k, v, B, S):
    T, W = q.shape
    SPAN, L = DIL_SPAN, DIL_BLOCK
    for window, r in DIL_PATTERNS:
        assert window // r == L and SPAN % (L * r) == 0 and L * r <= SPAN
    assert S % SPAN == 0
    ns = S // SPAN
    PW = 2 * HEAD_DIM
    own = lambda b, t, hp: (b * ns + t, hp)
    prev = lambda b, t, hp: (b * ns + jnp.maximum(t - 1, 0), hp)
    blk = (SPAN, PW)
    return pl.pallas_call(
        _dilated_fused_kernel,
        grid=(B, ns, W // PW),
        in_specs=[pl.BlockSpec(blk, own), pl.BlockSpec(blk, prev), pl.BlockSpec(blk, own),
                  pl.BlockSpec(blk, prev), pl.BlockSpec(blk, own)],
        out_specs=pl.BlockSpec(blk, own),
        out_shape=jax.ShapeDtypeStruct((T, W), BF16),
        scratch_shapes=[
            pltpu.VMEM((SPAN, PW), F32),
            pltpu.VMEM((2 * SPAN, PW), F32),
            pltpu.VMEM((2 * SPAN, PW), F32),
            pltpu.VMEM((2 * DIL_GROUP, L, 2 * L), F32),
            pltpu.VMEM((2 * DIL_GROUP, L, 2 * L), BF16),
            pltpu.VMEM((len(DIL_PATTERNS), SPAN, PW), F32),
            pltpu.VMEM((len(DIL_PATTERNS), SPAN, PW), F32),
        ],
        compiler_params=_params("parallel", "parallel", "parallel"),
        name="dilated",
    )(q, k, k, v, v)


def _mlp_kernel(x_ref, ret_ref, dsa_ref, dil_ref, wout_ref, nw_ref, wup_ref, wdown_ref, fnw_ref, o_ref, *,
                apply_final_norm):
    mixed = (_dot(ret_ref[...], wout_ref[0:RET_W, :])
             + _dot(dsa_ref[...], wout_ref[RET_W:RET_W + DSA_W, :])
             + _dot(dil_ref[...], wout_ref[RET_W + DSA_W:, :]))
    x = x_ref[...] + mixed
    h = _rms(x, nw_ref[...]).astype(BF16)
    ff = None
    for c in range(wup_ref.shape[1] // FF_CHUNK):
        cs = slice(c * FF_CHUNK, (c + 1) * FF_CHUNK)
        u = jnp.maximum(_dot(h, wup_ref[:, cs]), 0.0)
        part = _dot((u * u).astype(BF16), wdown_ref[cs, :])
        ff = part if ff is None else ff + part
    out = x + ff
    o_ref[...] = _rms(out, fnw_ref[...]) if apply_final_norm else out


def _mlp(x2, ret_o, dsa_o, dil_o, w_out, nw, w_up, w_down, final_nw, *, apply_final_norm):
    T, D = x2.shape
    tm = ROW_TILE
    row = lambda i: (i, 0)
    const = lambda i: (0, 0)
    return pl.pallas_call(
        functools.partial(_mlp_kernel, apply_final_norm=apply_final_norm),
        grid=(T // tm,),
        in_specs=[
            pl.BlockSpec((tm, D), row),
            pl.BlockSpec((tm, RET_W), row),
            pl.BlockSpec((tm, DSA_W), row),
            pl.BlockSpec((tm, DIL_W), row),
            _resident(w_out.shape, const),
            _resident((1, D), const),
            _resident(w_up.shape, const),
            _resident(w_down.shape, const),
            _resident((1, D), const),
        ],
        out_specs=pl.BlockSpec((tm, D), row),
        out_shape=jax.ShapeDtypeStruct((T, D), F32),
        compiler_params=_params("parallel"),
        name="outproj_mlp",
    )(x2, ret_o, dsa_o, dil_o, w_out, nw, w_up, w_down, final_nw)


def _split_w_in(w_in):
    ret_cols = 4 * RET_W
    o = ret_cols
    dsa_q = w_in[:, o:o + DSA_W]; o += DSA_W
    c_kv = w_in[:, o:o + KV_LATENT]; o += KV_LATENT
    q_idx = w_in[:, o:o + IDX_HEADS * IDX_DIM]; o += IDX_HEADS * IDX_DIM
    k_idx = w_in[:, o:o + IDX_DIM]; o += IDX_DIM
    w_idx = w_in[:, o:o + IDX_HEADS]; o += IDX_HEADS
    dil = w_in[:, o:]
    w_main = jnp.concatenate([w_in[:, :ret_cols], dil, k_idx, c_kv], axis=1).astype(BF16)
    pad = jnp.zeros((w_in.shape[0], WIDX_ROWS - IDX_HEADS), w_in.dtype)
    w_t = jnp.concatenate([q_idx, dsa_q, w_idx, pad], axis=1).T.astype(BF16)
    return w_main, w_t


def kernel(x, attn_norm_w, w_in, ret_norm_w, dsa_kv_norm_w, dsa_w_uk, dsa_w_uv, w_out, mlp_norm_w, w_up, w_down, final_norm_w):
    B, S, D = x.shape
    depth = w_in.shape[0]
    assert S % ROW_TILE == 0 and S % DSA_KC == 0 and S % RET_CHUNK == 0
    tables = _retention_tables(S)
    x2 = x.reshape(B * S, D)
    for layer in range(depth):
        w_main, w_t = _split_w_in(w_in[layer])
        ret_cols, dq, dk, dv, kidx, qidxT, dsaqT, widxT, k, vT = _inproj(
            x2, attn_norm_w[layer][None, :], w_main, w_t, dsa_kv_norm_w[layer][None, :],
            dsa_w_uk[layer].astype(BF16), dsa_w_uv[layer].T.astype(BF16), B, S)
        ret_o = _retention(ret_cols, tables, ret_norm_w[layer].reshape(1, RET_W), B, S)
        dsa_o = _dsa(qidxT, dsaqT, widxT, kidx, k, vT, B, S)
        dil_o = _dilated_fused(dq, dk, dv, B, S)
        x2 = _mlp(x2, ret_o, dsa_o, dil_o, w_out[layer].astype(BF16), mlp_norm_w[layer][None, :],
                  w_up[layer].astype(BF16), w_down[layer].astype(BF16), final_norm_w[None, :],
                  apply_final_norm=layer == depth - 1)
    return x2.reshape(B, S, D)
```

```python
import functools
import math

import numpy as np
import jax
import jax.numpy as jnp
from jax import lax
from jax.experimental import pallas as pl
from jax.experimental.pallas import tpu as pltpu

F32 = jnp.float32
BF16 = jnp.bfloat16
I32 = jnp.int32
I16 = jnp.int16

HEAD_DIM = 64
RET_HEADS = 4
DSA_HEADS = 4
DIL_HEADS = 8
RET_W = RET_HEADS * HEAD_DIM
DSA_W = DSA_HEADS * HEAD_DIM
DIL_W = DIL_HEADS * HEAD_DIM
RET_CHUNK = 128
RET_STEP_CHUNKS = 8
RET_ROPE_BASE = 10000.0
KV_LATENT = 128
IDX_HEADS = 8
IDX_DIM = 64
TOPK_MAX = 256
DIL_PATTERNS = ((128, 1), (512, 4), (2048, 16))
DIL_BLOCK = 128
DIL_SPAN = 2048
DIL_GROUP = 4
DIL_MERGE_ROWS = 256
NORM_EPS = 1e-6

V7X_VMEM_LIMIT_BYTES = 56 * 1024 * 1024

ROW_TILE = 512
DSA_QB = 256
DSA_KC = 256
DSA_SUB = 64
DSA_SCORE_ROWS = 128
DSA_IDX_GROUP = 4
FF_CHUNK = 1024

WIDX_ROWS = 16

INT_MIN = -(2 ** 31)
KEY_NEG_INF = int(np.int32(np.uint32(0xFF800000) ^ np.uint32(0x7FFFFFFF)))
MASKED_SCORE = -1e30


def _params(*semantics):
    return pltpu.CompilerParams(dimension_semantics=semantics, vmem_limit_bytes=V7X_VMEM_LIMIT_BYTES)


def _resident(shape, index_map):
    return pl.BlockSpec(shape, index_map, pipeline_mode=pl.Buffered(1))


def _dot(a, b):
    return jnp.dot(a, b, preferred_element_type=F32)


def _dot_nt(a, b):
    return lax.dot_general(a, b, (((1,), (1,)), ((), ())), preferred_element_type=F32)


def _rms(x, w):
    return x * lax.rsqrt(jnp.mean(x * x, axis=-1, keepdims=True) + NORM_EPS) * w


def _inproj_kernel(x_ref, nw_ref, w_ref, wt_ref, kvn_ref, wuk_ref, wuvT_ref,
                   ret_ref, dq_ref, dk_ref, dv_ref, kidx_ref, qidxT_ref, dsaqT_ref, widxT_ref, k_ref, vT_ref):
    h = _rms(x_ref[...], nw_ref[...]).astype(BF16)
    o = 0
    for ref in (ret_ref, dq_ref, dk_ref, dv_ref, kidx_ref):
        n = ref.shape[-1]
        ref[...] = _dot(h, w_ref[:, o:o + n]).astype(ref.dtype)
        o += n
    c = _rms(_dot(h, w_ref[:, o:o + KV_LATENT]), kvn_ref[...]).astype(BF16)
    for hd in range(DSA_HEADS):
        k_ref[hd] = _dot(c, wuk_ref[:, hd * HEAD_DIM:(hd + 1) * HEAD_DIM]).astype(k_ref.dtype)
    for j in range(vT_ref.shape[0]):
        vT_ref[j] = _dot_nt(wuvT_ref[...], c[j * DSA_KC:(j + 1) * DSA_KC, :]).astype(vT_ref.dtype)
    o = 0
    for ref in (qidxT_ref, dsaqT_ref, widxT_ref):
        n = ref.shape[-2]
        ref[0] = _dot_nt(wt_ref[o:o + n, :], h).astype(ref.dtype)
        o += n


def _inproj(x2, nw, w_main, w_t, kvn, w_uk, w_uvT, B, S):
    T, D = x2.shape
    tm = ROW_TILE
    nt = S // tm
    assert tm % DSA_KC == 0
    widx_rows = w_t.shape[0] - IDX_HEADS * IDX_DIM - DSA_W
    row = lambda i: (i, 0)
    tr = lambda i: (i // nt, 0, i % nt)
    const = lambda i: (0, 0)
    out_shape = (
        jax.ShapeDtypeStruct((T, 4 * RET_W), F32),
        jax.ShapeDtypeStruct((T, DIL_W), BF16),
        jax.ShapeDtypeStruct((T, DIL_W), BF16),
        jax.ShapeDtypeStruct((T, DIL_W), BF16),
        jax.ShapeDtypeStruct((T, IDX_DIM), BF16),
        jax.ShapeDtypeStruct((B, IDX_HEADS * IDX_DIM, S), BF16),
        jax.ShapeDtypeStruct((B, DSA_W, S), BF16),
        jax.ShapeDtypeStruct((B, widx_rows, S), F32),
        jax.ShapeDtypeStruct((DSA_HEADS, T, HEAD_DIM), BF16),
        jax.ShapeDtypeStruct((T // DSA_KC, DSA_W, DSA_KC), BF16),
    )
    out_specs = (
        pl.BlockSpec((tm, 4 * RET_W), row),
        pl.BlockSpec((tm, DIL_W), row),
        pl.BlockSpec((tm, DIL_W), row),
        pl.BlockSpec((tm, DIL_W), row),
        pl.BlockSpec((tm, IDX_DIM), row),
        pl.BlockSpec((1, IDX_HEADS * IDX_DIM, tm), tr),
        pl.BlockSpec((1, DSA_W, tm), tr),
        pl.BlockSpec((1, widx_rows, tm), tr),
        pl.BlockSpec((DSA_HEADS, tm, HEAD_DIM), lambda i: (0, i, 0)),
        pl.BlockSpec((tm // DSA_KC, DSA_W, DSA_KC), lambda i: (i, 0, 0)),
    )
    return pl.pallas_call(
        _inproj_kernel,
        grid=(T // tm,),
        in_specs=[
            pl.BlockSpec((tm, D), row),
            _resident((1, D), const),
            _resident(w_main.shape, const),
            _resident(w_t.shape, const),
            _resident((1, KV_LATENT), const),
            _resident(w_uk.shape, const),
            _resident(w_uvT.shape, const),
        ],
        out_specs=out_specs,
        out_shape=out_shape,
        compiler_params=_params("parallel"),
        name="inproj",
    )(x2, nw, w_main, w_t, kvn, w_uk, w_uvT)


def _retention_tables(S):
    half = HEAD_DIM // 2
    inv = RET_ROPE_BASE ** (-jnp.arange(half, dtype=F32) / half)
    ang = jnp.arange(S, dtype=F32)[:, None] * inv[None, :]
    cos, sin = jnp.cos(ang), jnp.sin(ang)
    cos_h = jnp.concatenate([cos, cos], axis=-1)
    sin_h = jnp.concatenate([-sin, sin], axis=-1)
    cosf = jnp.tile(cos_h, (1, RET_HEADS))
    sinf = jnp.tile(sin_h, (1, RET_HEADS))
    C = RET_CHUNK
    log_g = np.log(1.0 - 2.0 ** (-5.0 - np.arange(RET_HEADS, dtype=np.float64)))
    i = np.arange(C, dtype=np.float64)
    diff = i[:, None] - i[None, :]
    decay = np.exp(np.maximum(diff, 0.0)[None] * log_g[:, None, None]) * (diff >= 0)[None]
    zeta = np.exp((C - 1.0 - i)[None, :] * log_g[:, None])
    xi = np.exp((i + 1.0)[None, :] * log_g[:, None])
    g_chunk = np.exp(C * log_g)
    rep = lambda t: np.repeat(t.T, HEAD_DIM, axis=1)
    g_rows = np.repeat(g_chunk, HEAD_DIM)[:, None] * np.ones((1, RET_W))
    return (cosf, sinf, jnp.asarray(decay, F32), jnp.asarray(rep(zeta), F32), jnp.asarray(rep(xi), F32),
            jnp.asarray(g_rows, F32))


def _retention_kernel(cols_ref, cos_ref, sin_ref, decay_ref, zeta_ref, xi_ref, grow_ref, nw_ref, o_ref, state_ref):
    W = RET_W

    @pl.when(pl.program_id(1) == 0)
    def _():
        state_ref[...] = jnp.zeros_like(state_ref)

    C = RET_CHUNK
    lane = lax.broadcasted_iota(I32, (1, W), 1)
    first_half = (lane % HEAD_DIM) < (HEAD_DIM // 2)
    head_of_lane = lane // HEAD_DIM
    r_head = lax.broadcasted_iota(I32, (W, W), 0) // HEAD_DIM
    c_head = lax.broadcasted_iota(I32, (W, W), 1) // HEAD_DIM
    same_head = r_head == c_head

    def rot(t, rows):
        partner = jnp.where(first_half, pltpu.roll(t, W - HEAD_DIM // 2, 1), pltpu.roll(t, HEAD_DIM // 2, 1))
        return t * cos_ref[rows, :] + partner * sin_ref[rows, :]

    qs, inners, kvs = [], [], []
    for j in range(RET_STEP_CHUNKS):
        rows = slice(j * C, (j + 1) * C)
        q = rot(cols_ref[rows, 0:W], rows)
        k = rot(cols_ref[rows, W:2 * W], rows) * (HEAD_DIM ** -0.5)
        v = cols_ref[rows, 2 * W:3 * W].astype(BF16)
        kb = k.astype(BF16)
        inner = jnp.zeros((C, W), F32)
        for h in range(RET_HEADS):
            sel = head_of_lane == h
            qh = jnp.where(sel, q, 0.0).astype(BF16)
            a = (_dot_nt(qh, kb) * decay_ref[h]).astype(BF16)
            inner = inner + jnp.where(sel, _dot(a, v), 0.0)
        kzT = (k * zeta_ref[...]).T.astype(BF16)
        kv = _dot(kzT, v)
        qs.append(q.astype(BF16))
        inners.append(inner)
        kvs.append(jnp.where(same_head, kv, 0.0))

    state = state_ref[...]
    for j in range(RET_STEP_CHUNKS):
        rows = slice(j * C, (j + 1) * C)
        o = inners[j] + _dot(qs[j], state.astype(BF16)) * xi_ref[...]
        state = grow_ref[...] * state + kvs[j]

        mu = jnp.zeros_like(o)
        for h in range(RET_HEADS):
            sel = head_of_lane == h
            mu = mu + jnp.where(sel, jnp.sum(jnp.where(sel, o, 0.0), axis=-1, keepdims=True), 0.0)
        mu = mu * (1.0 / HEAD_DIM)
        d = o - mu
        var = jnp.zeros_like(o)
        for h in range(RET_HEADS):
            sel = head_of_lane == h
            var = var + jnp.where(sel, jnp.sum(jnp.where(sel, d * d, 0.0), axis=-1, keepdims=True), 0.0)
        var = var * (1.0 / HEAD_DIM)
        y = d * lax.rsqrt(var + NORM_EPS) * nw_ref[...]
        g = cols_ref[rows, 3 * W:4 * W]
        o_ref[rows, :] = (jax.nn.silu(g) * y).astype(o_ref.dtype)
    state_ref[...] = state


def _retention(ret_cols, tables, norm_w, B, S):
    T = ret_cols.shape[0]
    C, W = RET_CHUNK, RET_W
    rows = RET_STEP_CHUNKS * C
    assert S % rows == 0
    n = S // rows
    cosf, sinf, decay, zeta, xi, g_rows = tables
    tok = lambda b, j: (b * n + j, 0)
    pos = lambda b, j: (j, 0)
    const2 = lambda b, j: (0, 0)
    return pl.pallas_call(
        _retention_kernel,
        grid=(B, n),
        in_specs=[
            pl.BlockSpec((rows, 4 * W), tok),
            pl.BlockSpec((rows, W), pos),
            pl.BlockSpec((rows, W), pos),
            _resident(decay.shape, lambda b, j: (0, 0, 0)),
            _resident((C, W), const2),
            _resident((C, W), const2),
            _resident((W, W), const2),
            _resident((1, W), const2),
        ],
        out_specs=pl.BlockSpec((rows, W), tok),
        out_shape=jax.ShapeDtypeStruct((T, W), BF16),
        scratch_shapes=[pltpu.VMEM((W, W), F32)],
        compiler_params=_params("parallel", "arbitrary"),
        name="retention",
    )(ret_cols, cosf, sinf, decay, zeta, xi, g_rows, norm_w)


def _sortable(x):
    b = pltpu.bitcast(x, I32)
    return b ^ ((b >> 31) & 0x7FFFFFFF)


def _dsa_kernel(qidxT_ref, dsaqT_ref, widxT_ref, kidx_ref, k_ref, vT_ref, o_ref,
                key_ref, hi_ref, lo_ref, digit_ref, tied_ref, tri_ref, qh_ref, s_ref, bias_ref, p_ref, acc_ref,
                *, n_sel, S):
    QB, KC, SUB = DSA_QB, DSA_KC, DSA_SUB
    n_sub = KC // SUB
    i = pl.program_id(1)
    n_chunks = i + 1
    n_pairs = (n_chunks + 1) // 2
    qpos = i * QB + lax.broadcasted_iota(I32, (1, QB), 1)
    krow = lax.broadcasted_iota(I32, (SUB, 1), 0)
    w_scale = IDX_HEADS ** -0.5 * IDX_DIM ** -0.5
    widxT = widxT_ref[0] * w_scale

    def score_pair(c, carry, causal_mask):
        off = pl.multiple_of(c * (2 * KC), 2 * KC)
        for sub in range(2 * KC // DSA_SCORE_ROWS):
            r0 = off + sub * DSA_SCORE_ROWS
            kc = kidx_ref[pl.ds(r0, DSA_SCORE_ROWS), :]
            acc = jnp.zeros((DSA_SCORE_ROWS, QB), F32)
            for g in range(0, IDX_HEADS, DSA_IDX_GROUP):
                logits = [_dot(kc, qidxT_ref[0, h * IDX_DIM:(h + 1) * IDX_DIM, :])
                          for h in range(g, g + DSA_IDX_GROUP)]
                for h, logit in zip(range(g, g + DSA_IDX_GROUP), logits):
                    acc = acc + jnp.maximum(logit, 0.0) * widxT[h:h + 1, :]
            for part in range(DSA_SCORE_ROWS // SUB):
                a = acc[part * SUB:(part + 1) * SUB, :]
                a = jnp.where(a == 0.0, 0.0, a)
                if causal_mask:
                    a = jnp.where(r0 + part * SUB + krow <= qpos, a, -jnp.inf)
                key = _sortable(a)
                rows = pl.ds(r0 + part * SUB, SUB)
                key_ref[rows, :] = key
                hi_ref[rows, :] = (key >> 16).astype(I16)
                lo_ref[rows, :] = ((key & 0xFFFF) - 0x8000).astype(I16)
        return carry

    lax.fori_loop(0, n_pairs - 1, functools.partial(score_pair, causal_mask=False), 0)
    score_pair(n_pairs - 1, 0, causal_mask=True)

    one_b, zero_b = jnp.ones((SUB, QB), BF16), jnp.zeros((SUB, QB), BF16)

    def count_packed(ref, pred):
        def body(c2, cnt):
            off = pl.multiple_of(c2 * (2 * KC), 2 * KC)
            for sub in range(2 * n_sub):
                cnt = cnt + jnp.where(pred(ref[pl.ds(off + sub * SUB, SUB), :]), one_b, zero_b)
            return cnt
        cnt = lax.fori_loop(0, n_pairs, body, zero_b)
        return jnp.sum(cnt.astype(F32), axis=0, keepdims=True)

    def signed_digit(u):
        return (u - 0x8000).astype(I16)

    def search_digit(ref, n_above, n_ge_zero):
        def step(t, carry):
            d, n_ge = carry
            cand = d | jnp.left_shift(jnp.int32(1), 15 - t)
            cand_s = signed_digit(cand)
            cnt = n_above + count_packed(ref, lambda x: x >= cand_s)
            keep = cnt >= n_sel
            return jnp.where(keep, cand, d), jnp.where(keep, cnt, n_ge)
        return lax.fori_loop(0, 16, step, (jnp.zeros((1, QB), I32), n_ge_zero))

    stored = jnp.full((1, QB), (n_pairs * (2 * KC)).astype(F32), F32)
    u_hi, n_ge = search_digit(hi_ref, 0.0, stored)
    thr_hi = signed_digit(u_hi)
    n_above = count_packed(hi_ref, lambda x: x > thr_hi)
    lowest = jnp.full((SUB, QB), -0x8000, I16)

    def low_digits_of_ties(c2, carry):
        off = pl.multiple_of(c2 * (2 * KC), 2 * KC)
        for sub in range(2 * n_sub):
            rows = pl.ds(off + sub * SUB, SUB)
            digit_ref[rows, :] = jnp.where(hi_ref[rows, :] == thr_hi, lo_ref[rows, :], lowest)
        return carry

    lax.fori_loop(0, n_pairs, low_digits_of_ties, 0)
    u_lo, n_ge = search_digit(digit_ref, n_above, n_ge)
    thr = ((u_hi - 0x8000) << 16) | u_lo

    needs_cut = (n_ge > n_sel) & (thr > KEY_NEG_INF)
    any_ties = jnp.max(jnp.where(needs_cut, 1.0, 0.0)) > 0.0

    qh_ref[...] = (dsaqT_ref[0] * (HEAD_DIM ** -0.5)).astype(BF16)
    acc_ref[...] = jnp.zeros_like(acc_ref)
    p_ref[...] = jnp.zeros_like(p_ref)

    thr_floor = jnp.maximum(thr, KEY_NEG_INF + 1)

    def write_bias(slot, c):
        off = pl.multiple_of(c * KC, KC)
        for sub in range(n_sub):
            key = key_ref[pl.ds(off + sub * SUB, SUB), :]
            bias_ref[slot, sub * SUB:(sub + 1) * SUB, :] = jnp.where(key >= thr_floor, 0.0, MASKED_SCORE)

    def write_bias_ties(slot, c, tie_take, taken):
        off = pl.multiple_of(c * KC, KC)
        for sub in range(n_sub):
            key = key_ref[pl.ds(off + sub * SUB, SUB), :]
            tied_ref[sub * SUB:(sub + 1) * SUB, :] = jnp.where(key == thr, 1.0, 0.0).astype(BF16)
        s_ref[slot, 0] = _dot(tri_ref[...], tied_ref[...])
        room = tie_take - taken
        for sub in range(n_sub):
            rows = slice(sub * SUB, (sub + 1) * SUB)
            key = key_ref[pl.ds(off + sub * SUB, SUB), :]
            before = s_ref[slot, 0, rows, :] - tied_ref[rows, :].astype(F32)
            tied_bias = jnp.where(before < room, 0.0, MASKED_SCORE)
            bias_ref[slot, rows, :] = jnp.where(key > thr, 0.0, jnp.where(key == thr, tied_bias, MASKED_SCORE))
        return taken + s_ref[slot, 0, KC - 1:KC, :]

    def issue_scores(slot, c):
        rows = pl.ds(pl.multiple_of(c * KC, KC), KC)
        for h in range(DSA_HEADS):
            s_ref[slot, h] = _dot(k_ref[h, rows, :], qh_ref[h * HEAD_DIM:(h + 1) * HEAD_DIM, :])

    def pv_update(slot, c, alphas):
        vch = vT_ref[c]
        pvs = [_dot(vch[h * HEAD_DIM:(h + 1) * HEAD_DIM, :], p_ref[slot, h]) for h in range(DSA_HEADS)]
        for h in range(DSA_HEADS):
            acc_ref[h] = alphas[h] * acc_ref[h] + pvs[h]

    def softmax_chunk(slot, ms, ls):
        new_m, new_l, alphas = [], [], []
        for h in range(DSA_HEADS):
            mx = jnp.full((SUB, QB), MASKED_SCORE, F32)
            for sub in range(n_sub):
                rows = slice(sub * SUB, (sub + 1) * SUB)
                t = s_ref[slot, h, rows, :] + bias_ref[slot, rows, :]
                s_ref[slot, h, rows, :] = t
                mx = jnp.maximum(mx, t)
            m_new = jnp.maximum(ms[h], jnp.max(mx, axis=0, keepdims=True))
            alpha = jnp.exp(ms[h] - m_new)
            psum = jnp.zeros((SUB, QB), F32)
            for sub in range(n_sub):
                rows = slice(sub * SUB, (sub + 1) * SUB)
                p = jnp.exp(s_ref[slot, h, rows, :] - m_new)
                psum = psum + p
                p_ref[slot, h, rows, :] = p.astype(BF16)
            new_m.append(m_new)
            new_l.append(alpha * ls[h] + jnp.sum(psum, axis=0, keepdims=True))
            alphas.append(alpha)
        return tuple(new_m), tuple(new_l), tuple(alphas)

    def attend(ties):
        if ties:
            thr_lo = signed_digit(u_lo)
            n_gt = n_above + count_packed(digit_ref, lambda x: x > thr_lo)
            tie_take = jnp.where(thr > KEY_NEG_INF, n_sel - n_gt, 0.0)
            ri = lax.broadcasted_iota(I32, (KC, KC), 0)
            ci = lax.broadcasted_iota(I32, (KC, KC), 1)
            tri_ref[...] = jnp.where(ci <= ri, 1.0, 0.0).astype(BF16)

        def attend_pair(c2, carry):
            ms, ls, prev_alphas, taken = carry
            ca, cb = 2 * c2, 2 * c2 + 1
            if ties:
                taken = write_bias_ties(0, ca, tie_take, taken)
                taken = write_bias_ties(1, cb, tie_take, taken)
            else:
                write_bias(0, ca)
                write_bias(1, cb)
            issue_scores(0, ca)
            pv_update(1, jnp.maximum(ca - 1, 0), prev_alphas)
            issue_scores(1, cb)
            ms, ls, alphas_a = softmax_chunk(0, ms, ls)
            pv_update(0, ca, alphas_a)
            ms, ls, alphas_b = softmax_chunk(1, ms, ls)
            return ms, ls, alphas_b, taken

        init = (tuple(jnp.full((1, QB), MASKED_SCORE, F32) for _ in range(DSA_HEADS)),
                tuple(jnp.zeros((1, QB), F32) for _ in range(DSA_HEADS)),
                tuple(jnp.ones((1, QB), F32) for _ in range(DSA_HEADS)),
                jnp.zeros((1, QB), F32))
        _, ls, last_alphas, _ = lax.fori_loop(0, n_pairs, attend_pair, init)
        pv_update(1, 2 * n_pairs - 1, last_alphas)
        for h in range(DSA_HEADS):
            acc_ref[h] = acc_ref[h] / ls[h]

    pl.when(any_ties)(lambda: attend(True))
    pl.when(jnp.logical_not(any_ties))(lambda: attend(False))
    o_ref[...] = acc_ref[...].reshape(DSA_W, QB).T.astype(o_ref.dtype)


def _dsa(qidxT, dsaqT, widxT, kidx, k, vT, B, S):
    QB, KC = DSA_QB, DSA_KC
    assert QB == KC and S % (2 * KC) == 0
    n_sel = min(TOPK_MAX, S // 4)
    assert n_sel <= KC, "the first causal chunk must be able to hold every selected key"
    assert S // DSA_SUB <= 256, "packed counts are exact only up to 256 adds per accumulator lane"
    nq = S // QB
    nkc = S // KC
    n_sel = min(TOPK_MAX, S // 4)
    T = B * S
    kern = functools.partial(_dsa_kernel, n_sel=n_sel, S=S)
    return pl.pallas_call(
        kern,
        grid=(B, nq),
        in_specs=[
            pl.BlockSpec((1, IDX_HEADS * IDX_DIM, QB), lambda b, i: (b, 0, i)),
            pl.BlockSpec((1, DSA_W, QB), lambda b, i: (b, 0, i)),
            pl.BlockSpec((1, widxT.shape[1], QB), lambda b, i: (b, 0, i)),
            _resident((S, IDX_DIM), lambda b, i: (b, 0)),
            _resident((DSA_HEADS, S, HEAD_DIM), lambda b, i: (0, b, 0)),
            _resident((nkc, DSA_W, KC), lambda b, i: (b, 0, 0)),
        ],
        out_specs=pl.BlockSpec((QB, DSA_W), lambda b, i: (b * nq + i, 0)),
        out_shape=jax.ShapeDtypeStruct((T, DSA_W), BF16),
        scratch_shapes=[
            pltpu.VMEM((S, QB), I32),
            pltpu.VMEM((S, QB), I16),
            pltpu.VMEM((S, QB), I16),
            pltpu.VMEM((S, QB), I16),
            pltpu.VMEM((KC, QB), BF16),
            pltpu.VMEM((KC, KC), BF16),
            pltpu.VMEM((DSA_W, QB), BF16),
            pltpu.VMEM((2, DSA_HEADS, KC, QB), F32),
            pltpu.VMEM((2, KC, QB), F32),
            pltpu.VMEM((2, DSA_HEADS, KC, QB), BF16),
            pltpu.VMEM((DSA_HEADS, HEAD_DIM, QB), F32),
        ],
        compiler_params=_params("parallel", "arbitrary"),
        name="dsa",
    )(qidxT, dsaqT, widxT, kidx, k, vT)


def _dil_units(dilation):
    blocks = DIL_SPAN // (DIL_BLOCK * dilation)
    return [(res, jb) for res in range(dilation) for jb in range(blocks)]


def _dilated_fused_kernel(q_ref, kp_ref, kc_ref, vp_ref, vc_ref, out_ref,
                          qf_ref, kf_ref, vf_ref, s_ref, p_ref, num_ref, den_ref, max_ref):
    L, SPAN = DIL_BLOCK, DIL_SPAN
    first_span = pl.program_id(1) == 0
    qf_ref[...] = q_ref[...].astype(F32) * (HEAD_DIM ** -0.5)
    kf_ref[0:SPAN, :] = kp_ref[...].astype(F32)
    kf_ref[SPAN:, :] = kc_ref[...].astype(F32)
    vf_ref[0:SPAN, :] = vp_ref[...].astype(F32)
    vf_ref[SPAN:, :] = vc_ref[...].astype(F32)

    lane = lax.broadcasted_iota(I32, (1, 2 * HEAD_DIM), 1)
    low = lane < HEAD_DIM
    qi = lax.broadcasted_iota(I32, (L, 2 * L), 0)
    kj = lax.broadcasted_iota(I32, (L, 2 * L), 1)
    dist = L + qi - kj
    band = (dist >= 0) & (dist <= L)
    bias_band = jnp.where(band, 0.0, -jnp.inf)
    bias_head = jnp.where(band & ((kj >= L) | jnp.logical_not(first_span)), 0.0, -jnp.inf)

    def rows(start, count, stride):
        return pl.ds(start, count, stride=stride) if stride > 1 else pl.ds(start, count)

    for branch, (window, r) in enumerate(DIL_PATTERNS):
        units = _dil_units(r)
        for g0 in range(0, len(units), DIL_GROUP):
            group = units[g0:g0 + DIL_GROUP]
            vvs = []
            for u, (res, jb) in enumerate(group):
                q = qf_ref[rows(res + r * L * jb, L, r), :].astype(BF16)
                k0 = SPAN + res + r * L * (jb - 1)
                kk = kf_ref[rows(k0, 2 * L, r), :].astype(BF16)
                vv = vf_ref[rows(k0, 2 * L, r), :]
                for sub in range(2):
                    mine = low if sub == 0 else jnp.logical_not(low)
                    s_ref[2 * u + sub] = _dot_nt(jnp.where(mine, q, jnp.zeros_like(q)), kk)
                    vvs.append(jnp.where(mine, vv, 1.0).astype(BF16))
            ms = []
            for u, (res, jb) in enumerate(group):
                bias = bias_head if jb == 0 else bias_band
                for sub in range(2):
                    t = s_ref[2 * u + sub] + bias
                    m = jnp.max(t, axis=-1, keepdims=True)
                    p_ref[2 * u + sub] = jnp.exp(t - m).astype(BF16)
                    ms.append(m)
            for u, (res, jb) in enumerate(group):
                halves = []
                for sub in range(2):
                    nd = _dot(p_ref[2 * u + sub], vvs[2 * u + sub])
                    halves.append((nd, pltpu.roll(nd, HEAD_DIM, 1), ms[2 * u + sub]))
                (n0, d0, m0), (n1, d1, m1) = halves
                dst = rows(res + r * L * jb, L, r)
                num_ref[branch, dst, :] = jnp.where(low, n0, n1)
                den_ref[branch, dst, :] = jnp.where(low, d0, d1)
                max_ref[branch, dst, :] = jnp.where(low, m0, m1)

    for piece in range(SPAN // DIL_MERGE_ROWS):
        rs = slice(piece * DIL_MERGE_ROWS, (piece + 1) * DIL_MERGE_ROWS)
        a, b, c = max_ref[0, rs, :], max_ref[1, rs, :], max_ref[2, rs, :]
        m = jnp.maximum(jnp.maximum(a, b), c)
        ea, eb, ec = jnp.exp(a - m), jnp.exp(b - m), jnp.exp(c - m)
        num = ea * num_ref[0, rs, :] + eb * num_ref[1, rs, :] + ec * num_ref[2, rs, :]
        den = ea * den_ref[0, rs, :] + eb * den_ref[1, rs, :] + ec * den_ref[2, rs, :]
        out_ref[rs, :] = (num / den).astype(out_ref.dtype)


def _dilated_fused(q, k, v, B, S):
    T, W = q.shape
    SPAN, L = DIL_SPAN, DIL_BLOCK
    for window, r in DIL_PATTERNS:
        assert window // r == L and SPAN % (L * r) == 0 and L * r <= SPAN
    assert S % SPAN == 0
    ns = S // SPAN
    PW = 2 * HEAD_DIM
    own = lambda b, t, hp: (b * ns + t, hp)
    prev = lambda b, t, hp: (b * ns + jnp.maximum(t - 1, 0), hp)
    blk = (SPAN, PW)
    return pl.pallas_call(
        _dilated_fused_kernel,
        grid=(B, ns, W // PW),
        in_specs=[pl.BlockSpec(blk, own), pl.BlockSpec(blk, prev), pl.BlockSpec(blk, own),
                  pl.BlockSpec(blk, prev), pl.BlockSpec(blk, own)],
        out_specs=pl.BlockSpec(blk, own),
        out_shape=jax.ShapeDtypeStruct((T, W), BF16),
        scratch_shapes=[
            pltpu.VMEM((SPAN, PW), F32),
            pltpu.VMEM((2 * SPAN, PW), F32),
            pltpu.VMEM((2 * SPAN, PW), F32),
            pltpu.VMEM((2 * DIL_GROUP, L, 2 * L), F32),
            pltpu.VMEM((2 * DIL_GROUP, L, 2 * L), BF16),
            pltpu.VMEM((len(DIL_PATTERNS), SPAN, PW), F32),
            pltpu.VMEM((len(DIL_PATTERNS), SPAN, PW), F32),
            pltpu.VMEM((len(DIL_PATTERNS), SPAN, PW), F32),
        ],
        compiler_params=_params("parallel", "parallel", "parallel"),
        name="dilated",
    )(q, k, k, v, v)


def _mlp_kernel(x_ref, ret_ref, dsa_ref, dil_ref, wout_ref, nw_ref, wup_ref, wdown_ref, fnw_ref, o_ref, *,
                apply_final_norm):
    mixed = (_dot(ret_ref[...], wout_ref[0:RET_W, :])
             + _dot(dsa_ref[...], wout_ref[RET_W:RET_W + DSA_W, :])
             + _dot(dil_ref[...], wout_ref[RET_W + DSA_W:, :]))
    x = x_ref[...] + mixed
    h = _rms(x, nw_ref[...]).astype(BF16)
    ff = None
    for c in range(wup_ref.shape[1] // FF_CHUNK):
        cs = slice(c * FF_CHUNK, (c + 1) * FF_CHUNK)
        u = jnp.maximum(_dot(h, wup_ref[:, cs]), 0.0)
        part = _dot((u * u).astype(BF16), wdown_ref[cs, :])
        ff = part if ff is None else ff + part
    out = x + ff
    o_ref[...] = _rms(out, fnw_ref[...]) if apply_final_norm else out


def _mlp(x2, ret_o, dsa_o, dil_o, w_out, nw, w_up, w_down, final_nw, *, apply_final_norm):
    T, D = x2.shape
    tm = ROW_TILE
    row = lambda i: (i, 0)
    const = lambda i: (0, 0)
    return pl.pallas_call(
        functools.partial(_mlp_kernel, apply_final_norm=apply_final_norm),
        grid=(T // tm,),
        in_specs=[
            pl.BlockSpec((tm, D), row),
            pl.BlockSpec((tm, RET_W), row),
            pl.BlockSpec((tm, DSA_W), row),
            pl.BlockSpec((tm, DIL_W), row),
            _resident(w_out.shape, const),
            _resident((1, D), const),
            _resident(w_up.shape, const),
            _resident(w_down.shape, const),
            _resident((1, D), const),
        ],
        out_specs=pl.BlockSpec((tm, D), row),
        out_shape=jax.ShapeDtypeStruct((T, D), F32),
        compiler_params=_params("parallel"),
        name="outproj_mlp",
    )(x2, ret_o, dsa_o, dil_o, w_out, nw, w_up, w_down, final_nw)


def _split_w_in(w_in):
    ret_cols = 4 * RET_W
    o = ret_cols
    dsa_q = w_in[:, o:o + DSA_W]; o += DSA_W
    c_kv = w_in[:, o:o + KV_LATENT]; o += KV_LATENT
    q_idx = w_in[:, o:o + IDX_HEADS * IDX_DIM]; o += IDX_HEADS * IDX_DIM
    k_idx = w_in[:, o:o + IDX_DIM]; o += IDX_DIM
    w_idx = w_in[:, o:o + IDX_HEADS]; o += IDX_HEADS
    dil = w_in[:, o:]
    w_main = jnp.concatenate([w_in[:, :ret_cols], dil, k_idx, c_kv], axis=1).astype(BF16)
    pad = jnp.zeros((w_in.shape[0], WIDX_ROWS - IDX_HEADS), w_in.dtype)
    w_t = jnp.concatenate([q_idx, dsa_q, w_idx, pad], axis=1).T.astype(BF16)
    return w_main, w_t


def kernel(x, attn_norm_w, w_in, ret_norm_w, dsa_kv_norm_w, dsa_w_uk, dsa_w_uv, w_out, mlp_norm_w, w_up, w_down, final_norm_w):
    B, S, D = x.shape
    depth = w_in.shape[0]
    assert S % ROW_TILE == 0 and S % DSA_KC == 0 and S % RET_CHUNK == 0
    tables = _retention_tables(S)
    x2 = x.reshape(B * S, D)
    for layer in range(depth):
        w_main, w_t = _split_w_in(w_in[layer])
        ret_cols, dq, dk, dv, kidx, qidxT, dsaqT, widxT, k, vT = _inproj(
            x2, attn_norm_w[layer][None, :], w_main, w_t, dsa_kv_norm_w[layer][None, :],
            dsa_w_uk[layer].astype(BF16), dsa_w_uv[layer].T.astype(BF16), B, S)
        ret_o = _retention(ret_cols, tables, ret_norm_w[layer].reshape(1, RET_W), B, S)
        dsa_o = _dsa(qidxT, dsaqT, widxT, kidx, k, vT, B, S)
        dil_o = _dilated_fused(dq, dk, dv, B, S)
        x2 = _mlp(x2, ret_o, dsa_o, dil_o, w_out[layer].astype(BF16), mlp_norm_w[layer][None, :],
                  w_up[layer].astype(BF16), w_down[layer].astype(BF16), final_norm_w[None, :],
                  apply_final_norm=layer == depth - 1)
    return x2.reshape(B, S, D)
```

```python
import functools
import math

import numpy as np
import jax
import jax.numpy as jnp
from jax import lax
from jax.experimental import pallas as pl
from jax.experimental.pallas import tpu as pltpu

F32 = jnp.float32
BF16 = jnp.bfloat16
I32 = jnp.int32
I16 = jnp.int16

HEAD_DIM = 64
RET_HEADS = 4
DSA_HEADS = 4
DIL_HEADS = 8
RET_W = RET_HEADS * HEAD_DIM
DSA_W = DSA_HEADS * HEAD_DIM
DIL_W = DIL_HEADS * HEAD_DIM
RET_CHUNK = 128
RET_STEP_CHUNKS = 8
RET_ROPE_BASE = 10000.0
KV_LATENT = 128
IDX_HEADS = 8
IDX_DIM = 64
TOPK_MAX = 256
DIL_PATTERNS = ((128, 1), (512, 4), (2048, 16))
DIL_BLOCK = 128
DIL_SPAN = 2048
DIL_GROUP = 4
DIL_MERGE_ROWS = 256
NORM_EPS = 1e-6

V7X_VMEM_LIMIT_BYTES = 56 * 1024 * 1024

ROW_TILE = 512
DSA_QB = 256
DSA_KC = 256
DSA_SUB = 64
DSA_SCORE_ROWS = 128
DSA_IDX_GROUP = 4
FF_CHUNK = 1024

WIDX_ROWS = 16

INT_MIN = -(2 ** 31)
KEY_NEG_INF = int(np.int32(np.uint32(0xFF800000) ^ np.uint32(0x7FFFFFFF)))
MASKED_SCORE = -1e30
LOG2_E = math.log2(math.e)


def _params(*semantics):
    return pltpu.CompilerParams(dimension_semantics=semantics, vmem_limit_bytes=V7X_VMEM_LIMIT_BYTES)


def _resident(shape, index_map):
    return pl.BlockSpec(shape, index_map, pipeline_mode=pl.Buffered(1))


def _dot(a, b):
    return jnp.dot(a, b, preferred_element_type=F32)


def _dot_nt(a, b):
    return lax.dot_general(a, b, (((1,), (1,)), ((), ())), preferred_element_type=F32)


def _rms(x, w):
    return x * lax.rsqrt(jnp.mean(x * x, axis=-1, keepdims=True) + NORM_EPS) * w


def _inproj_kernel(x_ref, nw_ref, w_ref, wt_ref, kvn_ref, wuk_ref, wuvT_ref,
                   ret_ref, dq_ref, dk_ref, dv_ref, kidx_ref, qidxT_ref, dsaqT_ref, widxT_ref, k_ref, vT_ref):
    h = _rms(x_ref[...], nw_ref[...]).astype(BF16)
    o = 0
    for ref in (ret_ref, dq_ref, dk_ref, dv_ref, kidx_ref):
        n = ref.shape[-1]
        ref[...] = _dot(h, w_ref[:, o:o + n]).astype(ref.dtype)
        o += n
    c = _rms(_dot(h, w_ref[:, o:o + KV_LATENT]), kvn_ref[...]).astype(BF16)
    for hd in range(DSA_HEADS):
        k_ref[hd] = _dot(c, wuk_ref[:, hd * HEAD_DIM:(hd + 1) * HEAD_DIM]).astype(k_ref.dtype)
    for j in range(vT_ref.shape[0]):
        vT_ref[j] = _dot_nt(wuvT_ref[...], c[j * DSA_KC:(j + 1) * DSA_KC, :]).astype(vT_ref.dtype)
    o = 0
    for ref in (qidxT_ref, dsaqT_ref, widxT_ref):
        n = ref.shape[-2]
        ref[0] = _dot_nt(wt_ref[o:o + n, :], h).astype(ref.dtype)
        o += n


def _inproj(x2, nw, w_main, w_t, kvn, w_uk, w_uvT, B, S):
    T, D = x2.shape
    tm = ROW_TILE
    nt = S // tm
    assert tm % DSA_KC == 0
    widx_rows = w_t.shape[0] - IDX_HEADS * IDX_DIM - DSA_W
    row = lambda i: (i, 0)
    tr = lambda i: (i // nt, 0, i % nt)
    const = lambda i: (0, 0)
    out_shape = (
        jax.ShapeDtypeStruct((T, 4 * RET_W), F32),
        jax.ShapeDtypeStruct((T, DIL_W), BF16),
        jax.ShapeDtypeStruct((T, DIL_W), BF16),
        jax.ShapeDtypeStruct((T, DIL_W), BF16),
        jax.ShapeDtypeStruct((T, IDX_DIM), BF16),
        jax.ShapeDtypeStruct((B, IDX_HEADS * IDX_DIM, S), BF16),
        jax.ShapeDtypeStruct((B, DSA_W, S), F32),
        jax.ShapeDtypeStruct((B, widx_rows, S), F32),
        jax.ShapeDtypeStruct((DSA_HEADS, T, HEAD_DIM), BF16),
        jax.ShapeDtypeStruct((T // DSA_KC, DSA_W, DSA_KC), BF16),
    )
    out_specs = (
        pl.BlockSpec((tm, 4 * RET_W), row),
        pl.BlockSpec((tm, DIL_W), row),
        pl.BlockSpec((tm, DIL_W), row),
        pl.BlockSpec((tm, DIL_W), row),
        pl.BlockSpec((tm, IDX_DIM), row),
        pl.BlockSpec((1, IDX_HEADS * IDX_DIM, tm), tr),
        pl.BlockSpec((1, DSA_W, tm), tr),
        pl.BlockSpec((1, widx_rows, tm), tr),
        pl.BlockSpec((DSA_HEADS, tm, HEAD_DIM), lambda i: (0, i, 0)),
        pl.BlockSpec((tm // DSA_KC, DSA_W, DSA_KC), lambda i: (i, 0, 0)),
    )
    return pl.pallas_call(
        _inproj_kernel,
        grid=(T // tm,),
        in_specs=[
            pl.BlockSpec((tm, D), row),
            _resident((1, D), const),
            _resident(w_main.shape, const),
            _resident(w_t.shape, const),
            _resident((1, KV_LATENT), const),
            _resident(w_uk.shape, const),
            _resident(w_uvT.shape, const),
        ],
        out_specs=out_specs,
        out_shape=out_shape,
        compiler_params=_params("parallel"),
        name="inproj",
    )(x2, nw, w_main, w_t, kvn, w_uk, w_uvT)


def _retention_tables(S):
    half = HEAD_DIM // 2
    inv = RET_ROPE_BASE ** (-jnp.arange(half, dtype=F32) / half)
    ang = jnp.arange(S, dtype=F32)[:, None] * inv[None, :]
    cos, sin = jnp.cos(ang), jnp.sin(ang)
    cos_h = jnp.concatenate([cos, cos], axis=-1)
    sin_h = jnp.concatenate([-sin, sin], axis=-1)
    cosf = jnp.tile(cos_h, (1, RET_HEADS))
    sinf = jnp.tile(sin_h, (1, RET_HEADS))
    C = RET_CHUNK
    log_g = np.log(1.0 - 2.0 ** (-5.0 - np.arange(RET_HEADS, dtype=np.float64)))
    i = np.arange(C, dtype=np.float64)
    diff = i[:, None] - i[None, :]
    decay = np.exp(np.maximum(diff, 0.0)[None] * log_g[:, None, None]) * (diff >= 0)[None]
    zeta = np.exp((C - 1.0 - i)[None, :] * log_g[:, None])
    xi = np.exp((i + 1.0)[None, :] * log_g[:, None])
    g_chunk = np.exp(C * log_g)
    rep = lambda t: np.repeat(t.T, HEAD_DIM, axis=1)
    g_rows = np.repeat(g_chunk, HEAD_DIM)[:, None] * np.ones((1, RET_W))
    return (cosf, sinf, jnp.asarray(decay, F32), jnp.asarray(rep(zeta), F32), jnp.asarray(rep(xi), F32),
            jnp.asarray(g_rows, F32))


def _retention_kernel(cols_ref, cos_ref, sin_ref, decay_ref, zeta_ref, xi_ref, grow_ref, nw_ref, o_ref, state_ref):
    W = RET_W

    @pl.when(pl.program_id(1) == 0)
    def _():
        state_ref[...] = jnp.zeros_like(state_ref)

    C = RET_CHUNK
    lane = lax.broadcasted_iota(I32, (1, W), 1)
    first_half = (lane % HEAD_DIM) < (HEAD_DIM // 2)
    head_of_lane = lane // HEAD_DIM
    r_head = lax.broadcasted_iota(I32, (W, W), 0) // HEAD_DIM
    c_head = lax.broadcasted_iota(I32, (W, W), 1) // HEAD_DIM
    same_head = r_head == c_head

    def rot(t, rows):
        partner = jnp.where(first_half, pltpu.roll(t, W - HEAD_DIM // 2, 1), pltpu.roll(t, HEAD_DIM // 2, 1))
        return t * cos_ref[rows, :] + partner * sin_ref[rows, :]

    qs, inners, kvs = [], [], []
    for j in range(RET_STEP_CHUNKS):
        rows = slice(j * C, (j + 1) * C)
        q = rot(cols_ref[rows, 0:W], rows)
        k = rot(cols_ref[rows, W:2 * W], rows) * (HEAD_DIM ** -0.5)
        v = cols_ref[rows, 2 * W:3 * W].astype(BF16)
        kb = k.astype(BF16)
        inner = jnp.zeros((C, W), F32)
        for h in range(RET_HEADS):
            sel = head_of_lane == h
            qh = jnp.where(sel, q, 0.0).astype(BF16)
            a = (_dot_nt(qh, kb) * decay_ref[h]).astype(BF16)
            inner = inner + jnp.where(sel, _dot(a, v), 0.0)
        kzT = (k * zeta_ref[...]).T.astype(BF16)
        kv = _dot(kzT, v)
        qs.append(q.astype(BF16))
        inners.append(inner)
        kvs.append(jnp.where(same_head, kv, 0.0))

    state = state_ref[...]
    for j in range(RET_STEP_CHUNKS):
        rows = slice(j * C, (j + 1) * C)
        o = inners[j] + _dot(qs[j], state.astype(BF16)) * xi_ref[...]
        state = grow_ref[...] * state + kvs[j]

        mu = jnp.zeros_like(o)
        for h in range(RET_HEADS):
            sel = head_of_lane == h
            mu = mu + jnp.where(sel, jnp.sum(jnp.where(sel, o, 0.0), axis=-1, keepdims=True), 0.0)
        mu = mu * (1.0 / HEAD_DIM)
        d = o - mu
        var = jnp.zeros_like(o)
        for h in range(RET_HEADS):
            sel = head_of_lane == h
            var = var + jnp.where(sel, jnp.sum(jnp.where(sel, d * d, 0.0), axis=-1, keepdims=True), 0.0)
        var = var * (1.0 / HEAD_DIM)
        y = d * lax.rsqrt(var + NORM_EPS) * nw_ref[...]
        g = cols_ref[rows, 3 * W:4 * W]
        o_ref[rows, :] = (jax.nn.silu(g) * y).astype(o_ref.dtype)
    state_ref[...] = state


def _retention(ret_cols, tables, norm_w, B, S):
    T = ret_cols.shape[0]
    C, W = RET_CHUNK, RET_W
    rows = RET_STEP_CHUNKS * C
    assert S % rows == 0
    n = S // rows
    cosf, sinf, decay, zeta, xi, g_rows = tables
    tok = lambda b, j: (b * n + j, 0)
    pos = lambda b, j: (j, 0)
    const2 = lambda b, j: (0, 0)
    return pl.pallas_call(
        _retention_kernel,
        grid=(B, n),
        in_specs=[
            pl.BlockSpec((rows, 4 * W), tok),
            pl.BlockSpec((rows, W), pos),
            pl.BlockSpec((rows, W), pos),
            _resident(decay.shape, lambda b, j: (0, 0, 0)),
            _resident((C, W), const2),
            _resident((C, W), const2),
            _resident((W, W), const2),
            _resident((1, W), const2),
        ],
        out_specs=pl.BlockSpec((rows, W), tok),
        out_shape=jax.ShapeDtypeStruct((T, W), BF16),
        scratch_shapes=[pltpu.VMEM((W, W), F32)],
        compiler_params=_params("parallel", "arbitrary"),
        name="retention",
    )(ret_cols, cosf, sinf, decay, zeta, xi, g_rows, norm_w)


def _sortable(x):
    b = pltpu.bitcast(x, I32)
    return b ^ ((b >> 31) & 0x7FFFFFFF)


def _dsa_kernel(qidxT_ref, dsaqT_ref, widxT_ref, kidx_ref, k_ref, vT_ref, o_ref,
                key_ref, hi_ref, lo_ref, digit_ref, tied_ref, tri_ref, qh_ref, s_ref, bias_ref, p_ref, acc_ref,
                *, n_sel, S):
    QB, KC, SUB = DSA_QB, DSA_KC, DSA_SUB
    n_sub = KC // SUB
    i = pl.program_id(1)
    n_chunks = i + 1
    n_pairs = (n_chunks + 1) // 2
    qpos = i * QB + lax.broadcasted_iota(I32, (1, QB), 1)
    krow = lax.broadcasted_iota(I32, (SUB, 1), 0)
    w_scale = IDX_HEADS ** -0.5 * IDX_DIM ** -0.5
    widxT = widxT_ref[0] * w_scale

    def score_pair(c, carry, causal_mask):
        off = pl.multiple_of(c * (2 * KC), 2 * KC)
        for sub in range(2 * KC // DSA_SCORE_ROWS):
            r0 = off + sub * DSA_SCORE_ROWS
            kc = kidx_ref[pl.ds(r0, DSA_SCORE_ROWS), :]
            acc = jnp.zeros((DSA_SCORE_ROWS, QB), F32)
            for g in range(0, IDX_HEADS, DSA_IDX_GROUP):
                logits = [_dot(kc, qidxT_ref[0, h * IDX_DIM:(h + 1) * IDX_DIM, :])
                          for h in range(g, g + DSA_IDX_GROUP)]
                for h, logit in zip(range(g, g + DSA_IDX_GROUP), logits):
                    acc = acc + jnp.maximum(logit, 0.0) * widxT[h:h + 1, :]
            for part in range(DSA_SCORE_ROWS // SUB):
                a = acc[part * SUB:(part + 1) * SUB, :]
                a = jnp.where(a == 0.0, 0.0, a)
                if causal_mask:
                    a = jnp.where(r0 + part * SUB + krow <= qpos, a, -jnp.inf)
                key = _sortable(a)
                rows = pl.ds(r0 + part * SUB, SUB)
                key_ref[rows, :] = key
                hi_ref[rows, :] = (key >> 16).astype(I16)
                lo_ref[rows, :] = ((key & 0xFFFF) - 0x8000).astype(I16)
        return carry

    lax.fori_loop(0, n_pairs - 1, functools.partial(score_pair, causal_mask=False), 0)
    score_pair(n_pairs - 1, 0, causal_mask=True)

    one_b, zero_b = jnp.ones((SUB, QB), BF16), jnp.zeros((SUB, QB), BF16)

    def count_packed(ref, pred):
        def body(c2, cnt):
            off = pl.multiple_of(c2 * (2 * KC), 2 * KC)
            for sub in range(2 * n_sub):
                cnt = cnt + jnp.where(pred(ref[pl.ds(off + sub * SUB, SUB), :]), one_b, zero_b)
            return cnt
        cnt = lax.fori_loop(0, n_pairs, body, zero_b)
        return jnp.sum(cnt.astype(F32), axis=0, keepdims=True)

    def signed_digit(u):
        return (u - 0x8000).astype(I16)

    def search_digit(ref, n_above, n_ge_zero):
        def step(t, carry):
            d, n_ge = carry
            cand = d | jnp.left_shift(jnp.int32(1), 15 - t)
            cand_s = signed_digit(cand)
            cnt = n_above + count_packed(ref, lambda x: x >= cand_s)
            keep = cnt >= n_sel
            return jnp.where(keep, cand, d), jnp.where(keep, cnt, n_ge)
        return lax.fori_loop(0, 16, step, (jnp.zeros((1, QB), I32), n_ge_zero))

    stored = jnp.full((1, QB), (n_pairs * (2 * KC)).astype(F32), F32)
    u_hi, n_ge = search_digit(hi_ref, 0.0, stored)
    thr_hi = signed_digit(u_hi)
    n_above = count_packed(hi_ref, lambda x: x > thr_hi)
    lowest = jnp.full((SUB, QB), -0x8000, I16)

    def low_digits_of_ties(c2, carry):
        off = pl.multiple_of(c2 * (2 * KC), 2 * KC)
        for sub in range(2 * n_sub):
            rows = pl.ds(off + sub * SUB, SUB)
            digit_ref[rows, :] = jnp.where(hi_ref[rows, :] == thr_hi, lo_ref[rows, :], lowest)
        return carry

    lax.fori_loop(0, n_pairs, low_digits_of_ties, 0)
    u_lo, n_ge = search_digit(digit_ref, n_above, n_ge)
    thr = ((u_hi - 0x8000) << 16) | u_lo

    needs_cut = (n_ge > n_sel) & (thr > KEY_NEG_INF)
    any_ties = jnp.max(jnp.where(needs_cut, 1.0, 0.0)) > 0.0

    qh_ref[...] = (dsaqT_ref[0] * (HEAD_DIM ** -0.5 * LOG2_E)).astype(BF16)
    acc_ref[...] = jnp.zeros_like(acc_ref)
    p_ref[...] = jnp.zeros_like(p_ref)

    thr_floor = jnp.maximum(thr, KEY_NEG_INF + 1)

    def write_bias(slot, c):
        off = pl.multiple_of(c * KC, KC)
        for sub in range(n_sub):
            key = key_ref[pl.ds(off + sub * SUB, SUB), :]
            bias_ref[slot, sub * SUB:(sub + 1) * SUB, :] = jnp.where(key >= thr_floor, 0.0, MASKED_SCORE)

    def write_bias_ties(slot, c, tie_take, taken):
        off = pl.multiple_of(c * KC, KC)
        for sub in range(n_sub):
            key = key_ref[pl.ds(off + sub * SUB, SUB), :]
            tied_ref[sub * SUB:(sub + 1) * SUB, :] = jnp.where(key == thr, 1.0, 0.0).astype(BF16)
        s_ref[slot, 0] = _dot(tri_ref[...], tied_ref[...])
        room = tie_take - taken
        for sub in range(n_sub):
            rows = slice(sub * SUB, (sub + 1) * SUB)
            key = key_ref[pl.ds(off + sub * SUB, SUB), :]
            before = s_ref[slot, 0, rows, :] - tied_ref[rows, :].astype(F32)
            tied_bias = jnp.where(before < room, 0.0, MASKED_SCORE)
            bias_ref[slot, rows, :] = jnp.where(key > thr, 0.0, jnp.where(key == thr, tied_bias, MASKED_SCORE))
        return taken + s_ref[slot, 0, KC - 1:KC, :]

    def issue_scores(slot, c):
        rows = pl.ds(pl.multiple_of(c * KC, KC), KC)
        for h in range(DSA_HEADS):
            s_ref[slot, h] = _dot(k_ref[h, rows, :], qh_ref[h * HEAD_DIM:(h + 1) * HEAD_DIM, :])

    def pv_update(slot, c, alphas):
        vch = vT_ref[c]
        pvs = [_dot(vch[h * HEAD_DIM:(h + 1) * HEAD_DIM, :], p_ref[slot, h]) for h in range(DSA_HEADS)]
        for h in range(DSA_HEADS):
            acc_ref[h] = alphas[h] * acc_ref[h] + pvs[h]

    def softmax_chunk(slot, ms, ls):
        new_m, new_l, alphas = [], [], []
        for h in range(DSA_HEADS):
            mx = jnp.full((SUB, QB), MASKED_SCORE, F32)
            for sub in range(n_sub):
                rows = slice(sub * SUB, (sub + 1) * SUB)
                t = s_ref[slot, h, rows, :] + bias_ref[slot, rows, :]
                s_ref[slot, h, rows, :] = t
                mx = jnp.maximum(mx, t)
            m_new = jnp.maximum(ms[h], jnp.max(mx, axis=0, keepdims=True))
            alpha = jnp.exp2(ms[h] - m_new)
            psum = jnp.zeros((SUB, QB), F32)
            for sub in range(n_sub):
                rows = slice(sub * SUB, (sub + 1) * SUB)
                p = jnp.exp2(s_ref[slot, h, rows, :] - m_new)
                psum = psum + p
                p_ref[slot, h, rows, :] = p.astype(BF16)
            new_m.append(m_new)
            new_l.append(alpha * ls[h] + jnp.sum(psum, axis=0, keepdims=True))
            alphas.append(alpha)
        return tuple(new_m), tuple(new_l), tuple(alphas)

    def attend(ties):
        if ties:
            thr_lo = signed_digit(u_lo)
            n_gt = n_above + count_packed(digit_ref, lambda x: x > thr_lo)
            tie_take = jnp.where(thr > KEY_NEG_INF, n_sel - n_gt, 0.0)
            ri = lax.broadcasted_iota(I32, (KC, KC), 0)
            ci = lax.broadcasted_iota(I32, (KC, KC), 1)
            tri_ref[...] = jnp.where(ci <= ri, 1.0, 0.0).astype(BF16)

        def attend_pair(c2, carry):
            ms, ls, prev_alphas, taken = carry
            ca, cb = 2 * c2, 2 * c2 + 1
            if ties:
                taken = write_bias_ties(0, ca, tie_take, taken)
                taken = write_bias_ties(1, cb, tie_take, taken)
            else:
                write_bias(0, ca)
                write_bias(1, cb)
            issue_scores(0, ca)
            pv_update(1, jnp.maximum(ca - 1, 0), prev_alphas)
            issue_scores(1, cb)
            ms, ls, alphas_a = softmax_chunk(0, ms, ls)
            pv_update(0, ca, alphas_a)
            ms, ls, alphas_b = softmax_chunk(1, ms, ls)
            return ms, ls, alphas_b, taken

        init = (tuple(jnp.full((1, QB), MASKED_SCORE, F32) for _ in range(DSA_HEADS)),
                tuple(jnp.zeros((1, QB), F32) for _ in range(DSA_HEADS)),
                tuple(jnp.ones((1, QB), F32) for _ in range(DSA_HEADS)),
                jnp.zeros((1, QB), F32))
        _, ls, last_alphas, _ = lax.fori_loop(0, n_pairs, attend_pair, init)
        pv_update(1, 2 * n_pairs - 1, last_alphas)
        for h in range(DSA_HEADS):
            acc_ref[h] = acc_ref[h] / ls[h]

    pl.when(any_ties)(lambda: attend(True))
    pl.when(jnp.logical_not(any_ties))(lambda: attend(False))
    o_ref[...] = acc_ref[...].reshape(DSA_W, QB).T.astype(o_ref.dtype)


def _dsa(qidxT, dsaqT, widxT, kidx, k, vT, B, S):
    QB, KC = DSA_QB, DSA_KC
    assert QB == KC and S % (2 * KC) == 0
    n_sel = min(TOPK_MAX, S // 4)
    assert n_sel <= KC, "the first causal chunk must be able to hold every selected key"
    assert S // DSA_SUB <= 256, "packed counts are exact only up to 256 adds per accumulator lane"
    nq = S // QB
    nkc = S // KC
    n_sel = min(TOPK_MAX, S // 4)
    T = B * S
    kern = functools.partial(_dsa_kernel, n_sel=n_sel, S=S)
    return pl.pallas_call(
        kern,
        grid=(B, nq),
        in_specs=[
            pl.BlockSpec((1, IDX_HEADS * IDX_DIM, QB), lambda b, i: (b, 0, i)),
            pl.BlockSpec((1, DSA_W, QB), lambda b, i: (b, 0, i)),
            pl.BlockSpec((1, widxT.shape[1], QB), lambda b, i: (b, 0, i)),
            _resident((S, IDX_DIM), lambda b, i: (b, 0)),
            _resident((DSA_HEADS, S, HEAD_DIM), lambda b, i: (0, b, 0)),
            _resident((nkc, DSA_W, KC), lambda b, i: (b, 0, 0)),
        ],
        out_specs=pl.BlockSpec((QB, DSA_W), lambda b, i: (b * nq + i, 0)),
        out_shape=jax.ShapeDtypeStruct((T, DSA_W), BF16),
        scratch_shapes=[
            pltpu.VMEM((S, QB), I32),
            pltpu.VMEM((S, QB), I16),
            pltpu.VMEM((S, QB), I16),
            pltpu.VMEM((S, QB), I16),
            pltpu.VMEM((KC, QB), BF16),
            pltpu.VMEM((KC, KC), BF16),
            pltpu.VMEM((DSA_W, QB), BF16),
            pltpu.VMEM((2, DSA_HEADS, KC, QB), F32),
            pltpu.VMEM((2, KC, QB), F32),
            pltpu.VMEM((2, DSA_HEADS, KC, QB), BF16),
            pltpu.VMEM((DSA_HEADS, HEAD_DIM, QB), F32),
        ],
        compiler_params=_params("parallel", "arbitrary"),
        name="dsa",
    )(qidxT, dsaqT, widxT, kidx, k, vT)


def _dil_units(dilation):
    blocks = DIL_SPAN // (DIL_BLOCK * dilation)
    return [(res, jb) for res in range(dilation) for jb in range(blocks)]


def _dilated_fused_kernel(q_ref, kp_ref, kc_ref, vp_ref, vc_ref, out_ref,
                          qf_ref, kf_ref, vf_ref, s_ref, p_ref, num_ref, den_ref, max_ref):
    L, SPAN = DIL_BLOCK, DIL_SPAN
    first_span = pl.program_id(1) == 0
    qf_ref[...] = q_ref[...].astype(F32) * (HEAD_DIM ** -0.5)
    kf_ref[0:SPAN, :] = kp_ref[...].astype(F32)
    kf_ref[SPAN:, :] = kc_ref[...].astype(F32)
    vf_ref[0:SPAN, :] = vp_ref[...].astype(F32)
    vf_ref[SPAN:, :] = vc_ref[...].astype(F32)

    lane = lax.broadcasted_iota(I32, (1, 2 * HEAD_DIM), 1)
    low = lane < HEAD_DIM
    qi = lax.broadcasted_iota(I32, (L, 2 * L), 0)
    kj = lax.broadcasted_iota(I32, (L, 2 * L), 1)
    dist = L + qi - kj
    band = (dist >= 0) & (dist <= L)
    bias_band = jnp.where(band, 0.0, -jnp.inf)
    bias_head = jnp.where(band & ((kj >= L) | jnp.logical_not(first_span)), 0.0, -jnp.inf)

    def rows(start, count, stride):
        return pl.ds(start, count, stride=stride) if stride > 1 else pl.ds(start, count)

    for branch, (window, r) in enumerate(DIL_PATTERNS):
        units = _dil_units(r)
        for g0 in range(0, len(units), DIL_GROUP):
            group = units[g0:g0 + DIL_GROUP]
            vvs = []
            for u, (res, jb) in enumerate(group):
                q = qf_ref[rows(res + r * L * jb, L, r), :].astype(BF16)
                k0 = SPAN + res + r * L * (jb - 1)
                kk = kf_ref[rows(k0, 2 * L, r), :].astype(BF16)
                vv = vf_ref[rows(k0, 2 * L, r), :]
                for sub in range(2):
                    mine = low if sub == 0 else jnp.logical_not(low)
                    s_ref[2 * u + sub] = _dot_nt(jnp.where(mine, q, jnp.zeros_like(q)), kk)
                    vvs.append(jnp.where(mine, vv, 1.0).astype(BF16))
            ms = []
            for u, (res, jb) in enumerate(group):
                bias = bias_head if jb == 0 else bias_band
                for sub in range(2):
                    t = s_ref[2 * u + sub] + bias
                    m = jnp.max(t, axis=-1, keepdims=True)
                    p_ref[2 * u + sub] = jnp.exp(t - m).astype(BF16)
                    ms.append(m)
            for u, (res, jb) in enumerate(group):
                halves = []
                for sub in range(2):
                    nd = _dot(p_ref[2 * u + sub], vvs[2 * u + sub])
                    halves.append((nd, pltpu.roll(nd, HEAD_DIM, 1), ms[2 * u + sub]))
                (n0, d0, m0), (n1, d1, m1) = halves
                dst = rows(res + r * L * jb, L, r)
                num_ref[branch, dst, :] = jnp.where(low, n0, n1)
                den_ref[branch, dst, :] = jnp.where(low, d0, d1)
                max_ref[branch, dst, :] = jnp.where(low, m0, m1)

    for piece in range(SPAN // DIL_MERGE_ROWS):
        rs = slice(piece * DIL_MERGE_ROWS, (piece + 1) * DIL_MERGE_ROWS)
        a, b, c = max_ref[0, rs, :], max_ref[1, rs, :], max_ref[2, rs, :]
        m = jnp.maximum(jnp.maximum(a, b), c)
        ea, eb, ec = jnp.exp(a - m), jnp.exp(b - m), jnp.exp(c - m)
        num = ea * num_ref[0, rs, :] + eb * num_ref[1, rs, :] + ec * num_ref[2, rs, :]
        den = ea * den_ref[0, rs, :] + eb * den_ref[1, rs, :] + ec * den_ref[2, rs, :]
        out_ref[rs, :] = (num / den).astype(out_ref.dtype)


def _dilated_fused(q, k, v, B, S):
    T, W = q.shape
    SPAN, L = DIL_SPAN, DIL_BLOCK
    for window, r in DIL_PATTERNS:
        assert window // r == L and SPAN % (L * r) == 0 and L * r <= SPAN
    assert S % SPAN == 0
    ns = S // SPAN
    PW = 2 * HEAD_DIM
    own = lambda b, t, hp: (b * ns + t, hp)
    prev = lambda b, t, hp: (b * ns + jnp.maximum(t - 1, 0), hp)
    blk = (SPAN, PW)
    return pl.pallas_call(
        _dilated_fused_kernel,
        grid=(B, ns, W // PW),
        in_specs=[pl.BlockSpec(blk, own), pl.BlockSpec(blk, prev), pl.BlockSpec(blk, own),
                  pl.BlockSpec(blk, prev), pl.BlockSpec(blk, own)],
        out_specs=pl.BlockSpec(blk, own),
        out_shape=jax.ShapeDtypeStruct((T, W), BF16),
        scratch_shapes=[
            pltpu.VMEM((SPAN, PW), F32),
            pltpu.VMEM((2 * SPAN, PW), F32),
            pltpu.VMEM((2 * SPAN, PW), F32),
            pltpu.VMEM((2 * DIL_GROUP, L, 2 * L), F32),
            pltpu.VMEM((2 * DIL_GROUP, L, 2 * L), BF16),
            pltpu.VMEM((len(DIL_PATTERNS), SPAN, PW), F32),
            pltpu.VMEM((len(DIL_PATTERNS), SPAN, PW), F32),
            pltpu.VMEM((len(DIL_PATTERNS), SPAN, PW), F32),
        ],
        compiler_params=_params("parallel", "parallel", "parallel"),
        name="dilated",
    )(q, k, k, v, v)


def _mlp_kernel(x_ref, ret_ref, dsa_ref, dil_ref, wout_ref, nw_ref, wup_ref, wdown_ref, fnw_ref, o_ref, *,
                apply_final_norm):
    mixed = (_dot(ret_ref[...], wout_ref[0:RET_W, :])
             + _dot(dsa_ref[...], wout_ref[RET_W:RET_W + DSA_W, :])
             + _dot(dil_ref[...], wout_ref[RET_W + DSA_W:, :]))
    x = x_ref[...] + mixed
    h = _rms(x, nw_ref[...]).astype(BF16)
    ff = None
    for c in range(wup_ref.shape[1] // FF_CHUNK):
        cs = slice(c * FF_CHUNK, (c + 1) * FF_CHUNK)
        u = jnp.maximum(_dot(h, wup_ref[:, cs]), 0.0)
        part = _dot((u * u).astype(BF16), wdown_ref[cs, :])
        ff = part if ff is None else ff + part
    out = x + ff
    o_ref[...] = _rms(out, fnw_ref[...]) if apply_final_norm else out


def _mlp(x2, ret_o, dsa_o, dil_o, w_out, nw, w_up, w_down, final_nw, *, apply_final_norm):
    T, D = x2.shape
    tm = ROW_TILE
    row = lambda i: (i, 0)
    const = lambda i: (0, 0)
    return pl.pallas_call(
        functools.partial(_mlp_kernel, apply_final_norm=apply_final_norm),
        grid=(T // tm,),
        in_specs=[
            pl.BlockSpec((tm, D), row),
            pl.BlockSpec((tm, RET_W), row),
            pl.BlockSpec((tm, DSA_W), row),
            pl.BlockSpec((tm, DIL_W), row),
            _resident(w_out.shape, const),
            _resident((1, D), const),
            _resident(w_up.shape, const),
            _resident(w_down.shape, const),
            _resident((1, D), const),
        ],
        out_specs=pl.BlockSpec((tm, D), row),
        out_shape=jax.ShapeDtypeStruct((T, D), F32),
        compiler_params=_params("parallel"),
        name="outproj_mlp",
    )(x2, ret_o, dsa_o, dil_o, w_out, nw, w_up, w_down, final_nw)


def _split_w_in(w_in):
    ret_cols = 4 * RET_W
    o = ret_cols
    dsa_q = w_in[:, o:o + DSA_W]; o += DSA_W
    c_kv = w_in[:, o:o + KV_LATENT]; o += KV_LATENT
    q_idx = w_in[:, o:o + IDX_HEADS * IDX_DIM]; o += IDX_HEADS * IDX_DIM
    k_idx = w_in[:, o:o + IDX_DIM]; o += IDX_DIM
    w_idx = w_in[:, o:o + IDX_HEADS]; o += IDX_HEADS
    dil = w_in[:, o:]
    w_main = jnp.concatenate([w_in[:, :ret_cols], dil, k_idx, c_kv], axis=1).astype(BF16)
    pad = jnp.zeros((w_in.shape[0], WIDX_ROWS - IDX_HEADS), w_in.dtype)
    w_t = jnp.concatenate([q_idx, dsa_q, w_idx, pad], axis=1).T.astype(BF16)
    return w_main, w_t


def kernel(x, attn_norm_w, w_in, ret_norm_w, dsa_kv_norm_w, dsa_w_uk, dsa_w_uv, w_out, mlp_norm_w, w_up, w_down, final_norm_w):
    B, S, D = x.shape
    depth = w_in.shape[0]
    assert S % ROW_TILE == 0 and S % DSA_KC == 0 and S % RET_CHUNK == 0
    tables = _retention_tables(S)
    x2 = x.reshape(B * S, D)
    for layer in range(depth):
        w_main, w_t = _split_w_in(w_in[layer])
        ret_cols, dq, dk, dv, kidx, qidxT, dsaqT, widxT, k, vT = _inproj(
            x2, attn_norm_w[layer][None, :], w_main, w_t, dsa_kv_norm_w[layer][None, :],
            dsa_w_uk[layer].astype(BF16), dsa_w_uv[layer].T.astype(BF16), B, S)
        ret_o = _retention(ret_cols, tables, ret_norm_w[layer].reshape(1, RET_W), B, S)
        dsa_o = _dsa(qidxT, dsaqT, widxT, kidx, k, vT, B, S)
        dil_o = _dilated_fused(dq, dk, dv, B, S)
        x2 = _mlp(x2, ret_o, dsa_o, dil_o, w_out[layer].astype(BF16), mlp_norm_w[layer][None, :],
                  w_up[layer].astype(BF16), w_down[layer].astype(BF16), final_norm_w[None, :],
                  apply_final_norm=layer == depth - 1)
    return x2.reshape(B, S, D)
```

```python
import functools
import math

import numpy as np
import jax
import jax.numpy as jnp
from jax import lax
from jax.experimental import pallas as pl
from jax.experimental.pallas import tpu as pltpu

F32 = jnp.float32
BF16 = jnp.bfloat16
I32 = jnp.int32
I16 = jnp.int16

HEAD_DIM = 64
RET_HEADS = 4
DSA_HEADS = 4
DIL_HEADS = 8
RET_W = RET_HEADS * HEAD_DIM
DSA_W = DSA_HEADS * HEAD_DIM
DIL_W = DIL_HEADS * HEAD_DIM
RET_CHUNK = 128
RET_STEP_CHUNKS = 8
RET_ROPE_BASE = 10000.0
KV_LATENT = 128
IDX_HEADS = 8
IDX_DIM = 64
TOPK_MAX = 256
DIL_PATTERNS = ((128, 1), (512, 4), (2048, 16))
DIL_BLOCK = 128
DIL_SPAN = 2048
DIL_GROUP = 4
DIL_MERGE_ROWS = 256
NORM_EPS = 1e-6

V7X_VMEM_LIMIT_BYTES = 56 * 1024 * 1024

ROW_TILE = 512
DSA_QB = 256
DSA_KC = 256
DSA_SUB = 64
DSA_SCORE_ROWS = 128
DSA_IDX_GROUP = 4
FF_CHUNK = 1024

WIDX_ROWS = 16

KEY_NEG_INF = int(np.int32(np.uint32(0xFF800000) ^ np.uint32(0x7FFFFFFF)))
MASKED_SCORE = -1e30
LOG2_E = math.log2(math.e)


def _params(*semantics):
    return pltpu.CompilerParams(dimension_semantics=semantics, vmem_limit_bytes=V7X_VMEM_LIMIT_BYTES)


def _resident(shape, index_map):
    return pl.BlockSpec(shape, index_map, pipeline_mode=pl.Buffered(1))


def _dot(a, b):
    return jnp.dot(a, b, preferred_element_type=F32)


def _dot_nt(a, b):
    return lax.dot_general(a, b, (((1,), (1,)), ((), ())), preferred_element_type=F32)


def _rms(x, w):
    return x * lax.rsqrt(jnp.mean(x * x, axis=-1, keepdims=True) + NORM_EPS) * w


def _inproj_kernel(x_ref, nw_ref, w_ref, wt_ref, kvn_ref, wuk_ref, wuvT_ref,
                   ret_ref, dq_ref, dk_ref, dv_ref, kidx_ref, qidxT_ref, dsaqT_ref, widxT_ref, k_ref, vT_ref):
    h = _rms(x_ref[...], nw_ref[...]).astype(BF16)
    o = 0
    for ref in (ret_ref, dq_ref, dk_ref, dv_ref, kidx_ref):
        n = ref.shape[-1]
        ref[...] = _dot(h, w_ref[:, o:o + n]).astype(ref.dtype)
        o += n
    c = _rms(_dot(h, w_ref[:, o:o + KV_LATENT]), kvn_ref[...]).astype(BF16)
    for hd in range(DSA_HEADS):
        k_ref[hd] = _dot(c, wuk_ref[:, hd * HEAD_DIM:(hd + 1) * HEAD_DIM]).astype(k_ref.dtype)
    for j in range(vT_ref.shape[0]):
        vT_ref[j] = _dot_nt(wuvT_ref[...], c[j * DSA_KC:(j + 1) * DSA_KC, :]).astype(vT_ref.dtype)
    o = 0
    for ref in (qidxT_ref, dsaqT_ref, widxT_ref):
        n = ref.shape[-2]
        ref[0] = _dot_nt(wt_ref[o:o + n, :], h).astype(ref.dtype)
        o += n


def _inproj(x2, nw, w_main, w_t, kvn, w_uk, w_uvT, B, S):
    T, D = x2.shape
    tm = ROW_TILE
    nt = S // tm
    assert tm % DSA_KC == 0
    widx_rows = w_t.shape[0] - IDX_HEADS * IDX_DIM - DSA_W
    row = lambda i: (i, 0)
    tr = lambda i: (i // nt, 0, i % nt)
    const = lambda i: (0, 0)
    out_shape = (
        jax.ShapeDtypeStruct((T, 4 * RET_W), F32),
        jax.ShapeDtypeStruct((T, DIL_W), BF16),
        jax.ShapeDtypeStruct((T, DIL_W), BF16),
        jax.ShapeDtypeStruct((T, DIL_W), BF16),
        jax.ShapeDtypeStruct((T, IDX_DIM), BF16),
        jax.ShapeDtypeStruct((B, IDX_HEADS * IDX_DIM, S), BF16),
        jax.ShapeDtypeStruct((B, DSA_W, S), F32),
        jax.ShapeDtypeStruct((B, widx_rows, S), F32),
        jax.ShapeDtypeStruct((DSA_HEADS, T, HEAD_DIM), BF16),
        jax.ShapeDtypeStruct((T // DSA_KC, DSA_W, DSA_KC), BF16),
    )
    out_specs = (
        pl.BlockSpec((tm, 4 * RET_W), row),
        pl.BlockSpec((tm, DIL_W), row),
        pl.BlockSpec((tm, DIL_W), row),
        pl.BlockSpec((tm, DIL_W), row),
        pl.BlockSpec((tm, IDX_DIM), row),
        pl.BlockSpec((1, IDX_HEADS * IDX_DIM, tm), tr),
        pl.BlockSpec((1, DSA_W, tm), tr),
        pl.BlockSpec((1, widx_rows, tm), tr),
        pl.BlockSpec((DSA_HEADS, tm, HEAD_DIM), lambda i: (0, i, 0)),
        pl.BlockSpec((tm // DSA_KC, DSA_W, DSA_KC), lambda i: (i, 0, 0)),
    )
    return pl.pallas_call(
        _inproj_kernel,
        grid=(T // tm,),
        in_specs=[
            pl.BlockSpec((tm, D), row),
            _resident((1, D), const),
            _resident(w_main.shape, const),
            _resident(w_t.shape, const),
            _resident((1, KV_LATENT), const),
            _resident(w_uk.shape, const),
            _resident(w_uvT.shape, const),
        ],
        out_specs=out_specs,
        out_shape=out_shape,
        compiler_params=_params("parallel"),
        name="inproj",
    )(x2, nw, w_main, w_t, kvn, w_uk, w_uvT)


def _retention_tables(S):
    half = HEAD_DIM // 2
    inv = RET_ROPE_BASE ** (-jnp.arange(half, dtype=F32) / half)
    ang = jnp.arange(S, dtype=F32)[:, None] * inv[None, :]
    cos, sin = jnp.cos(ang), jnp.sin(ang)
    cos_h = jnp.concatenate([cos, cos], axis=-1)
    sin_h = jnp.concatenate([-sin, sin], axis=-1)
    cosf = jnp.tile(cos_h, (1, RET_HEADS))
    sinf = jnp.tile(sin_h, (1, RET_HEADS))
    C = RET_CHUNK
    log_g = np.log(1.0 - 2.0 ** (-5.0 - np.arange(RET_HEADS, dtype=np.float64)))
    i = np.arange(C, dtype=np.float64)
    diff = i[:, None] - i[None, :]
    decay = np.exp(np.maximum(diff, 0.0)[None] * log_g[:, None, None]) * (diff >= 0)[None]
    zeta = np.exp((C - 1.0 - i)[None, :] * log_g[:, None])
    xi = np.exp((i + 1.0)[None, :] * log_g[:, None])
    g_chunk = np.exp(C * log_g)
    rep = lambda t: np.repeat(t.T, HEAD_DIM, axis=1)
    g_rows = np.repeat(g_chunk, HEAD_DIM)[:, None] * np.ones((1, RET_W))
    return (cosf, sinf, jnp.asarray(decay, F32), jnp.asarray(rep(zeta), F32), jnp.asarray(rep(xi), F32),
            jnp.asarray(g_rows, F32))


def _retention_kernel(cols_ref, cos_ref, sin_ref, decay_ref, zeta_ref, xi_ref, grow_ref, nw_ref, o_ref, state_ref):
    W = RET_W

    @pl.when(pl.program_id(1) == 0)
    def _():
        state_ref[...] = jnp.zeros_like(state_ref)

    C = RET_CHUNK
    lane = lax.broadcasted_iota(I32, (1, W), 1)
    first_half = (lane % HEAD_DIM) < (HEAD_DIM // 2)
    head_of_lane = lane // HEAD_DIM
    r_head = lax.broadcasted_iota(I32, (W, W), 0) // HEAD_DIM
    c_head = lax.broadcasted_iota(I32, (W, W), 1) // HEAD_DIM
    same_head = r_head == c_head

    def rot(t, rows):
        partner = jnp.where(first_half, pltpu.roll(t, W - HEAD_DIM // 2, 1), pltpu.roll(t, HEAD_DIM // 2, 1))
        return t * cos_ref[rows, :] + partner * sin_ref[rows, :]

    qs, inners, kvs = [], [], []
    for j in range(RET_STEP_CHUNKS):
        rows = slice(j * C, (j + 1) * C)
        q = rot(cols_ref[rows, 0:W], rows)
        k = rot(cols_ref[rows, W:2 * W], rows) * (HEAD_DIM ** -0.5)
        v = cols_ref[rows, 2 * W:3 * W].astype(BF16)
        kb = k.astype(BF16)
        inner = jnp.zeros((C, W), F32)
        for h in range(RET_HEADS):
            sel = head_of_lane == h
            qh = jnp.where(sel, q, 0.0).astype(BF16)
            a = (_dot_nt(qh, kb) * decay_ref[h]).astype(BF16)
            inner = inner + jnp.where(sel, _dot(a, v), 0.0)
        kzT = (k * zeta_ref[...]).T.astype(BF16)
        kv = _dot(kzT, v)
        qs.append(q.astype(BF16))
        inners.append(inner)
        kvs.append(jnp.where(same_head, kv, 0.0))

    state = state_ref[...]
    for j in range(RET_STEP_CHUNKS):
        rows = slice(j * C, (j + 1) * C)
        o = inners[j] + _dot(qs[j], state.astype(BF16)) * xi_ref[...]
        state = grow_ref[...] * state + kvs[j]

        mu = jnp.zeros_like(o)
        for h in range(RET_HEADS):
            sel = head_of_lane == h
            mu = mu + jnp.where(sel, jnp.sum(jnp.where(sel, o, 0.0), axis=-1, keepdims=True), 0.0)
        mu = mu * (1.0 / HEAD_DIM)
        d = o - mu
        var = jnp.zeros_like(o)
        for h in range(RET_HEADS):
            sel = head_of_lane == h
            var = var + jnp.where(sel, jnp.sum(jnp.where(sel, d * d, 0.0), axis=-1, keepdims=True), 0.0)
        var = var * (1.0 / HEAD_DIM)
        y = d * lax.rsqrt(var + NORM_EPS) * nw_ref[...]
        g = cols_ref[rows, 3 * W:4 * W]
        o_ref[rows, :] = (jax.nn.silu(g) * y).astype(o_ref.dtype)
    state_ref[...] = state


def _retention(ret_cols, tables, norm_w, B, S):
    T = ret_cols.shape[0]
    C, W = RET_CHUNK, RET_W
    rows = RET_STEP_CHUNKS * C
    assert S % rows == 0
    n = S // rows
    cosf, sinf, decay, zeta, xi, g_rows = tables
    tok = lambda b, j: (b * n + j, 0)
    pos = lambda b, j: (j, 0)
    const2 = lambda b, j: (0, 0)
    return pl.pallas_call(
        _retention_kernel,
        grid=(B, n),
        in_specs=[
            pl.BlockSpec((rows, 4 * W), tok),
            pl.BlockSpec((rows, W), pos),
            pl.BlockSpec((rows, W), pos),
            _resident(decay.shape, lambda b, j: (0, 0, 0)),
            _resident((C, W), const2),
            _resident((C, W), const2),
            _resident((W, W), const2),
            _resident((1, W), const2),
        ],
        out_specs=pl.BlockSpec((rows, W), tok),
        out_shape=jax.ShapeDtypeStruct((T, W), BF16),
        scratch_shapes=[pltpu.VMEM((W, W), F32)],
        compiler_params=_params("parallel", "arbitrary"),
        name="retention",
    )(ret_cols, cosf, sinf, decay, zeta, xi, g_rows, norm_w)


def _sortable(x):
    b = pltpu.bitcast(x, I32)
    return b ^ ((b >> 31) & 0x7FFFFFFF)


def _dsa_kernel(qidxT_ref, dsaqT_ref, widxT_ref, kidx_ref, k_ref, vT_ref, o_ref,
                key_ref, hi_ref, lo_ref, digit_ref, tied_ref, tri_ref, qh_ref, s_ref, bias_ref, p_ref, acc_ref,
                *, n_sel, S):
    QB, KC, SUB = DSA_QB, DSA_KC, DSA_SUB
    n_sub = KC // SUB
    i = pl.program_id(1)
    n_chunks = i + 1
    n_pairs = (n_chunks + 1) // 2
    qpos = i * QB + lax.broadcasted_iota(I32, (1, QB), 1)
    krow = lax.broadcasted_iota(I32, (SUB, 1), 0)
    w_scale = IDX_HEADS ** -0.5 * IDX_DIM ** -0.5
    widxT = widxT_ref[0] * w_scale

    def score_pair(c, carry, causal_mask):
        off = pl.multiple_of(c * (2 * KC), 2 * KC)
        for sub in range(2 * KC // DSA_SCORE_ROWS):
            r0 = off + sub * DSA_SCORE_ROWS
            kc = kidx_ref[pl.ds(r0, DSA_SCORE_ROWS), :]
            acc = jnp.zeros((DSA_SCORE_ROWS, QB), F32)
            for g in range(0, IDX_HEADS, DSA_IDX_GROUP):
                logits = [_dot(kc, qidxT_ref[0, h * IDX_DIM:(h + 1) * IDX_DIM, :])
                          for h in range(g, g + DSA_IDX_GROUP)]
                for h, logit in zip(range(g, g + DSA_IDX_GROUP), logits):
                    acc = acc + jnp.maximum(logit, 0.0) * widxT[h:h + 1, :]
            for part in range(DSA_SCORE_ROWS // SUB):
                a = acc[part * SUB:(part + 1) * SUB, :]
                a = jnp.where(a == 0.0, 0.0, a)
                if causal_mask:
                    a = jnp.where(r0 + part * SUB + krow <= qpos, a, -jnp.inf)
                key = _sortable(a)
                rows = pl.ds(r0 + part * SUB, SUB)
                key_ref[rows, :] = key
                hi_ref[rows, :] = (key >> 16).astype(I16)
                lo_ref[rows, :] = ((key & 0xFFFF) - 0x8000).astype(I16)
        return carry

    lax.fori_loop(0, n_pairs - 1, functools.partial(score_pair, causal_mask=False), 0)
    score_pair(n_pairs - 1, 0, causal_mask=True)

    one_b, zero_b = jnp.ones((SUB, QB), BF16), jnp.zeros((SUB, QB), BF16)

    def count_packed(ref, pred):
        def body(c2, cnt):
            off = pl.multiple_of(c2 * (2 * KC), 2 * KC)
            for sub in range(2 * n_sub):
                cnt = cnt + jnp.where(pred(ref[pl.ds(off + sub * SUB, SUB), :]), one_b, zero_b)
            return cnt
        cnt = lax.fori_loop(0, n_pairs, body, zero_b)
        return jnp.sum(cnt.astype(F32), axis=0, keepdims=True)

    def signed_digit(u):
        return (u - 0x8000).astype(I16)

    def search_digit(ref, n_above, n_ge_zero):
        def step(t, carry):
            d, n_ge = carry
            cand = d | jnp.left_shift(jnp.int32(1), 15 - t)
            cand_s = signed_digit(cand)
            cnt = n_above + count_packed(ref, lambda x: x >= cand_s)
            keep = cnt >= n_sel
            return jnp.where(keep, cand, d), jnp.where(keep, cnt, n_ge)
        return lax.fori_loop(0, 16, step, (jnp.zeros((1, QB), I32), n_ge_zero))

    stored = jnp.full((1, QB), (n_pairs * (2 * KC)).astype(F32), F32)
    u_hi, n_ge = search_digit(hi_ref, 0.0, stored)
    thr_hi = signed_digit(u_hi)
    n_above = count_packed(hi_ref, lambda x: x > thr_hi)
    lowest = jnp.full((SUB, QB), -0x8000, I16)

    def low_digits_of_ties(c2, carry):
        off = pl.multiple_of(c2 * (2 * KC), 2 * KC)
        for sub in range(2 * n_sub):
            rows = pl.ds(off + sub * SUB, SUB)
            digit_ref[rows, :] = jnp.where(hi_ref[rows, :] == thr_hi, lo_ref[rows, :], lowest)
        return carry

    lax.fori_loop(0, n_pairs, low_digits_of_ties, 0)
    u_lo, n_ge = search_digit(digit_ref, n_above, n_ge)
    thr = ((u_hi - 0x8000) << 16) | u_lo

    needs_cut = (n_ge > n_sel) & (thr > KEY_NEG_INF)
    any_ties = jnp.max(jnp.where(needs_cut, 1.0, 0.0)) > 0.0

    qh_ref[...] = (dsaqT_ref[0] * (HEAD_DIM ** -0.5 * LOG2_E)).astype(BF16)
    acc_ref[...] = jnp.zeros_like(acc_ref)
    p_ref[...] = jnp.zeros_like(p_ref)

    thr_floor = jnp.maximum(thr, KEY_NEG_INF + 1)

    def write_bias(slot, c):
        off = pl.multiple_of(c * KC, KC)
        for sub in range(n_sub):
            key = key_ref[pl.ds(off + sub * SUB, SUB), :]
            bias_ref[slot, sub * SUB:(sub + 1) * SUB, :] = jnp.where(key >= thr_floor, 0.0, MASKED_SCORE)

    def write_bias_ties(slot, c, tie_take, taken):
        off = pl.multiple_of(c * KC, KC)
        for sub in range(n_sub):
            key = key_ref[pl.ds(off + sub * SUB, SUB), :]
            tied_ref[sub * SUB:(sub + 1) * SUB, :] = jnp.where(key == thr, 1.0, 0.0).astype(BF16)
        s_ref[slot, 0] = _dot(tri_ref[...], tied_ref[...])
        room = tie_take - taken
        for sub in range(n_sub):
            rows = slice(sub * SUB, (sub + 1) * SUB)
            key = key_ref[pl.ds(off + sub * SUB, SUB), :]
            tied_bias = jnp.where(s_ref[slot, 0, rows, :] < room, 0.0, MASKED_SCORE)
            bias_ref[slot, rows, :] = jnp.where(key > thr, 0.0, jnp.where(key == thr, tied_bias, MASKED_SCORE))
        return taken + s_ref[slot, 0, KC - 1:KC, :] + tied_ref[KC - 1:KC, :].astype(F32)

    def issue_scores(slot, c):
        rows = pl.ds(pl.multiple_of(c * KC, KC), KC)
        for h in range(DSA_HEADS):
            s_ref[slot, h] = _dot(k_ref[h, rows, :], qh_ref[h * HEAD_DIM:(h + 1) * HEAD_DIM, :])

    def pv_update(slot, c, alphas):
        vch = vT_ref[c]
        pvs = [_dot(vch[h * HEAD_DIM:(h + 1) * HEAD_DIM, :], p_ref[slot, h]) for h in range(DSA_HEADS)]
        for h in range(DSA_HEADS):
            acc_ref[h] = alphas[h] * acc_ref[h] + pvs[h]

    def softmax_chunk(slot, ms, ls):
        new_m, new_l, alphas = [], [], []
        for h in range(DSA_HEADS):
            mx = jnp.full((SUB, QB), MASKED_SCORE, F32)
            for sub in range(n_sub):
                rows = slice(sub * SUB, (sub + 1) * SUB)
                t = s_ref[slot, h, rows, :] + bias_ref[slot, rows, :]
                s_ref[slot, h, rows, :] = t
                mx = jnp.maximum(mx, t)
            m_new = jnp.maximum(ms[h], jnp.max(mx, axis=0, keepdims=True))
            alpha = jnp.exp2(ms[h] - m_new)
            psum = jnp.zeros((SUB, QB), F32)
            for sub in range(n_sub):
                rows = slice(sub * SUB, (sub + 1) * SUB)
                p = jnp.exp2(s_ref[slot, h, rows, :] - m_new)
                psum = psum + p
                p_ref[slot, h, rows, :] = p.astype(BF16)
            new_m.append(m_new)
            new_l.append(alpha * ls[h] + jnp.sum(psum, axis=0, keepdims=True))
            alphas.append(alpha)
        return tuple(new_m), tuple(new_l), tuple(alphas)

    def attend(ties):
        if ties:
            thr_lo = signed_digit(u_lo)
            n_gt = n_above + count_packed(digit_ref, lambda x: x > thr_lo)
            tie_take = jnp.where(thr > KEY_NEG_INF, n_sel - n_gt, 0.0)
            ri = lax.broadcasted_iota(I32, (KC, KC), 0)
            ci = lax.broadcasted_iota(I32, (KC, KC), 1)
            tri_ref[...] = jnp.where(ci < ri, 1.0, 0.0).astype(BF16)

        def attend_pair(c2, carry):
            ms, ls, prev_alphas, taken = carry
            ca, cb = 2 * c2, 2 * c2 + 1
            if ties:
                taken = write_bias_ties(0, ca, tie_take, taken)
                taken = write_bias_ties(1, cb, tie_take, taken)
            else:
                write_bias(0, ca)
                write_bias(1, cb)
            issue_scores(0, ca)
            pv_update(1, jnp.maximum(ca - 1, 0), prev_alphas)
            issue_scores(1, cb)
            ms, ls, alphas_a = softmax_chunk(0, ms, ls)
            pv_update(0, ca, alphas_a)
            ms, ls, alphas_b = softmax_chunk(1, ms, ls)
            return ms, ls, alphas_b, taken

        init = (tuple(jnp.full((1, QB), MASKED_SCORE, F32) for _ in range(DSA_HEADS)),
                tuple(jnp.zeros((1, QB), F32) for _ in range(DSA_HEADS)),
                tuple(jnp.ones((1, QB), F32) for _ in range(DSA_HEADS)),
                jnp.zeros((1, QB), F32))
        _, ls, last_alphas, _ = lax.fori_loop(0, n_pairs, attend_pair, init)
        pv_update(1, 2 * n_pairs - 1, last_alphas)
        for h in range(DSA_HEADS):
            acc_ref[h] = acc_ref[h] / ls[h]

    pl.when(any_ties)(lambda: attend(True))
    pl.when(jnp.logical_not(any_ties))(lambda: attend(False))
    o_ref[...] = acc_ref[...].reshape(DSA_W, QB).T.astype(o_ref.dtype)


def _dsa(qidxT, dsaqT, widxT, kidx, k, vT, B, S):
    QB, KC = DSA_QB, DSA_KC
    assert QB == KC and S % (2 * KC) == 0
    n_sel = min(TOPK_MAX, S // 4)
    assert n_sel <= KC, "the first causal chunk must be able to hold every selected key"
    assert S // DSA_SUB <= 256, "packed counts are exact only up to 256 adds per accumulator lane"
    nq = S // QB
    nkc = S // KC
    n_sel = min(TOPK_MAX, S // 4)
    T = B * S
    kern = functools.partial(_dsa_kernel, n_sel=n_sel, S=S)
    return pl.pallas_call(
        kern,
        grid=(B, nq),
        in_specs=[
            pl.BlockSpec((1, IDX_HEADS * IDX_DIM, QB), lambda b, i: (b, 0, i)),
            pl.BlockSpec((1, DSA_W, QB), lambda b, i: (b, 0, i)),
            pl.BlockSpec((1, widxT.shape[1], QB), lambda b, i: (b, 0, i)),
            _resident((S, IDX_DIM), lambda b, i: (b, 0)),
            _resident((DSA_HEADS, S, HEAD_DIM), lambda b, i: (0, b, 0)),
            _resident((nkc, DSA_W, KC), lambda b, i: (b, 0, 0)),
        ],
        out_specs=pl.BlockSpec((QB, DSA_W), lambda b, i: (b * nq + i, 0)),
        out_shape=jax.ShapeDtypeStruct((T, DSA_W), BF16),
        scratch_shapes=[
            pltpu.VMEM((S, QB), I32),
            pltpu.VMEM((S, QB), I16),
            pltpu.VMEM((S, QB), I16),
            pltpu.VMEM((S, QB), I16),
            pltpu.VMEM((KC, QB), BF16),
            pltpu.VMEM((KC, KC), BF16),
            pltpu.VMEM((DSA_W, QB), BF16),
            pltpu.VMEM((2, DSA_HEADS, KC, QB), F32),
            pltpu.VMEM((2, KC, QB), F32),
            pltpu.VMEM((2, DSA_HEADS, KC, QB), BF16),
            pltpu.VMEM((DSA_HEADS, HEAD_DIM, QB), F32),
        ],
        compiler_params=_params("parallel", "arbitrary"),
        name="dsa",
    )(qidxT, dsaqT, widxT, kidx, k, vT)


def _dil_units(dilation):
    blocks = DIL_SPAN // (DIL_BLOCK * dilation)
    return [(res, jb) for res in range(dilation) for jb in range(blocks)]


def _dilated_fused_kernel(q_ref, kp_ref, kc_ref, vp_ref, vc_ref, out_ref,
                          qf_ref, kf_ref, vf_ref, s_ref, p_ref, num_ref, den_ref, max_ref):
    L, SPAN = DIL_BLOCK, DIL_SPAN
    first_span = pl.program_id(1) == 0
    qf_ref[...] = q_ref[...].astype(F32) * (HEAD_DIM ** -0.5)
    kf_ref[0:SPAN, :] = kp_ref[...].astype(F32)
    kf_ref[SPAN:, :] = kc_ref[...].astype(F32)
    vf_ref[0:SPAN, :] = vp_ref[...].astype(F32)
    vf_ref[SPAN:, :] = vc_ref[...].astype(F32)

    lane = lax.broadcasted_iota(I32, (1, 2 * HEAD_DIM), 1)
    low = lane < HEAD_DIM
    qi = lax.broadcasted_iota(I32, (L, 2 * L), 0)
    kj = lax.broadcasted_iota(I32, (L, 2 * L), 1)
    dist = L + qi - kj
    band = (dist >= 0) & (dist <= L)
    bias_band = jnp.where(band, 0.0, -jnp.inf)
    bias_head = jnp.where(band & ((kj >= L) | jnp.logical_not(first_span)), 0.0, -jnp.inf)

    def rows(start, count, stride):
        return pl.ds(start, count, stride=stride) if stride > 1 else pl.ds(start, count)

    for branch, (window, r) in enumerate(DIL_PATTERNS):
        units = _dil_units(r)
        for g0 in range(0, len(units), DIL_GROUP):
            group = units[g0:g0 + DIL_GROUP]
            vvs = []
            for u, (res, jb) in enumerate(group):
                q = qf_ref[rows(res + r * L * jb, L, r), :].astype(BF16)
                k0 = SPAN + res + r * L * (jb - 1)
                kk = kf_ref[rows(k0, 2 * L, r), :].astype(BF16)
                vv = vf_ref[rows(k0, 2 * L, r), :]
                for sub in range(2):
                    mine = low if sub == 0 else jnp.logical_not(low)
                    s_ref[2 * u + sub] = _dot_nt(jnp.where(mine, q, jnp.zeros_like(q)), kk)
                    vvs.append(jnp.where(mine, vv, 1.0).astype(BF16))
            ms = []
            for u, (res, jb) in enumerate(group):
                bias = bias_head if jb == 0 else bias_band
                for sub in range(2):
                    t = s_ref[2 * u + sub] + bias
                    m = jnp.max(t, axis=-1, keepdims=True)
                    p_ref[2 * u + sub] = jnp.exp(t - m).astype(BF16)
                    ms.append(m)
            for u, (res, jb) in enumerate(group):
                halves = []
                for sub in range(2):
                    nd = _dot(p_ref[2 * u + sub], vvs[2 * u + sub])
                    halves.append((nd, pltpu.roll(nd, HEAD_DIM, 1), ms[2 * u + sub]))
                (n0, d0, m0), (n1, d1, m1) = halves
                dst = rows(res + r * L * jb, L, r)
                num_ref[branch, dst, :] = jnp.where(low, n0, n1)
                den_ref[branch, dst, :] = jnp.where(low, d0, d1)
                max_ref[branch, dst, :] = jnp.where(low, m0, m1)

    for piece in range(SPAN // DIL_MERGE_ROWS):
        rs = slice(piece * DIL_MERGE_ROWS, (piece + 1) * DIL_MERGE_ROWS)
        a, b, c = max_ref[0, rs, :], max_ref[1, rs, :], max_ref[2, rs, :]
        m = jnp.maximum(jnp.maximum(a, b), c)
        ea, eb, ec = jnp.exp(a - m), jnp.exp(b - m), jnp.exp(c - m)
        num = ea * num_ref[0, rs, :] + eb * num_ref[1, rs, :] + ec * num_ref[2, rs, :]
        den = ea * den_ref[0, rs, :] + eb * den_ref[1, rs, :] + ec * den_ref[2, rs, :]
        out_ref[rs, :] = (num / den).astype(out_ref.dtype)


def _dilated_fused(q, k, v, B, S):
    T, W = q.shape
    SPAN, L = DIL_SPAN, DIL_BLOCK
    for window, r in DIL_PATTERNS:
        assert window // r == L and SPAN % (L * r) == 0 and L * r <= SPAN
    assert S % SPAN == 0
    ns = S // SPAN
    PW = 2 * HEAD_DIM
    own = lambda b, t, hp: (b * ns + t, hp)
    prev = lambda b, t, hp: (b * ns + jnp.maximum(t - 1, 0), hp)
    blk = (SPAN, PW)
    return pl.pallas_call(
        _dilated_fused_kernel,
        grid=(B, ns, W // PW),
        in_specs=[pl.BlockSpec(blk, own), pl.BlockSpec(blk, prev), pl.BlockSpec(blk, own),
                  pl.BlockSpec(blk, prev), pl.BlockSpec(blk, own)],
        out_specs=pl.BlockSpec(blk, own),
        out_shape=jax.ShapeDtypeStruct((T, W), BF16),
        scratch_shapes=[
            pltpu.VMEM((SPAN, PW), F32),
            pltpu.VMEM((2 * SPAN, PW), F32),
            pltpu.VMEM((2 * SPAN, PW), F32),
            pltpu.VMEM((2 * DIL_GROUP, L, 2 * L), F32),
            pltpu.VMEM((2 * DIL_GROUP, L, 2 * L), BF16),
            pltpu.VMEM((len(DIL_PATTERNS), SPAN, PW), F32),
            pltpu.VMEM((len(DIL_PATTERNS), SPAN, PW), F32),
            pltpu.VMEM((len(DIL_PATTERNS), SPAN, PW), F32),
        ],
        compiler_params=_params("parallel", "parallel", "parallel"),
        name="dilated",
    )(q, k, k, v, v)


def _mlp_kernel(x_ref, ret_ref, dsa_ref, dil_ref, wout_ref, nw_ref, wup_ref, wdown_ref, fnw_ref, o_ref, *,
                apply_final_norm):
    mixed = (_dot(ret_ref[...], wout_ref[0:RET_W, :])
             + _dot(dsa_ref[...], wout_ref[RET_W:RET_W + DSA_W, :])
             + _dot(dil_ref[...], wout_ref[RET_W + DSA_W:, :]))
    x = x_ref[...] + mixed
    h = _rms(x, nw_ref[...]).astype(BF16)
    ff = None
    for c in range(wup_ref.shape[1] // FF_CHUNK):
        cs = slice(c * FF_CHUNK, (c + 1) * FF_CHUNK)
        u = jnp.maximum(_dot(h, wup_ref[:, cs]), 0.0)
        part = _dot((u * u).astype(BF16), wdown_ref[cs, :])
        ff = part if ff is None else ff + part
    out = x + ff
    o_ref[...] = _rms(out, fnw_ref[...]) if apply_final_norm else out


def _mlp(x2, ret_o, dsa_o, dil_o, w_out, nw, w_up, w_down, final_nw, *, apply_final_norm):
    T, D = x2.shape
    tm = ROW_TILE
    row = lambda i: (i, 0)
    const = lambda i: (0, 0)
    return pl.pallas_call(
        functools.partial(_mlp_kernel, apply_final_norm=apply_final_norm),
        grid=(T // tm,),
        in_specs=[
            pl.BlockSpec((tm, D), row),
            pl.BlockSpec((tm, RET_W), row),
            pl.BlockSpec((tm, DSA_W), row),
            pl.BlockSpec((tm, DIL_W), row),
            _resident(w_out.shape, const),
            _resident((1, D), const),
            _resident(w_up.shape, const),
            _resident(w_down.shape, const),
            _resident((1, D), const),
        ],
        out_specs=pl.BlockSpec((tm, D), row),
        out_shape=jax.ShapeDtypeStruct((T, D), F32),
        compiler_params=_params("parallel"),
        name="outproj_mlp",
    )(x2, ret_o, dsa_o, dil_o, w_out, nw, w_up, w_down, final_nw)


def _split_w_in(w_in):
    ret_cols = 4 * RET_W
    o = ret_cols
    dsa_q = w_in[:, o:o + DSA_W]; o += DSA_W
    c_kv = w_in[:, o:o + KV_LATENT]; o += KV_LATENT
    q_idx = w_in[:, o:o + IDX_HEADS * IDX_DIM]; o += IDX_HEADS * IDX_DIM
    k_idx = w_in[:, o:o + IDX_DIM]; o += IDX_DIM
    w_idx = w_in[:, o:o + IDX_HEADS]; o += IDX_HEADS
    dil = w_in[:, o:]
    w_main = jnp.concatenate([w_in[:, :ret_cols], dil, k_idx, c_kv], axis=1).astype(BF16)
    pad = jnp.zeros((w_in.shape[0], WIDX_ROWS - IDX_HEADS), w_in.dtype)
    w_t = jnp.concatenate([q_idx, dsa_q, w_idx, pad], axis=1).T.astype(BF16)
    return w_main, w_t


def kernel(x, attn_norm_w, w_in, ret_norm_w, dsa_kv_norm_w, dsa_w_uk, dsa_w_uv, w_out, mlp_norm_w, w_up, w_down, final_norm_w):
    B, S, D = x.shape
    depth = w_in.shape[0]
    assert S % ROW_TILE == 0 and S % DSA_KC == 0 and S % RET_CHUNK == 0
    tables = _retention_tables(S)
    x2 = x.reshape(B * S, D)
    for layer in range(depth):
        w_main, w_t = _split_w_in(w_in[layer])
        ret_cols, dq, dk, dv, kidx, qidxT, dsaqT, widxT, k, vT = _inproj(
            x2, attn_norm_w[layer][None, :], w_main, w_t, dsa_kv_norm_w[layer][None, :],
            dsa_w_uk[layer].astype(BF16), dsa_w_uv[layer].T.astype(BF16), B, S)
        ret_o = _retention(ret_cols, tables, ret_norm_w[layer].reshape(1, RET_W), B, S)
        dsa_o = _dsa(qidxT, dsaqT, widxT, kidx, k, vT, B, S)
        dil_o = _dilated_fused(dq, dk, dv, B, S)
        x2 = _mlp(x2, ret_o, dsa_o, dil_o, w_out[layer].astype(BF16), mlp_norm_w[layer][None, :],
                  w_up[layer].astype(BF16), w_down[layer].astype(BF16), final_norm_w[None, :],
                  apply_final_norm=layer == depth - 1)
    return x2.reshape(B, S, D)
```

```python
import functools
import math

import numpy as np
import jax
import jax.numpy as jnp
from jax import lax
from jax.experimental import pallas as pl
from jax.experimental.pallas import tpu as pltpu

F32 = jnp.float32
BF16 = jnp.bfloat16
I32 = jnp.int32
I16 = jnp.int16

HEAD_DIM = 64
RET_HEADS = 4
DSA_HEADS = 4
DIL_HEADS = 8
RET_W = RET_HEADS * HEAD_DIM
DSA_W = DSA_HEADS * HEAD_DIM
DIL_W = DIL_HEADS * HEAD_DIM
RET_CHUNK = 128
RET_STEP_CHUNKS = 8
RET_ROPE_BASE = 10000.0
KV_LATENT = 128
IDX_HEADS = 8
IDX_DIM = 64
TOPK_MAX = 256
DIL_PATTERNS = ((128, 1), (512, 4), (2048, 16))
DIL_BLOCK = 128
DIL_SPAN = 2048
DIL_GROUP = 4
DIL_MERGE_ROWS = 256
NORM_EPS = 1e-6

V7X_VMEM_LIMIT_BYTES = 56 * 1024 * 1024

ROW_TILE = 512
DSA_QB = 256
DSA_KC = 256
DSA_SUB = 64
DSA_SCORE_ROWS = 128
DSA_IDX_GROUP = 4
FF_CHUNK = 1024

WIDX_ROWS = 16

KEY_NEG_INF = int(np.int32(np.uint32(0xFF800000) ^ np.uint32(0x7FFFFFFF)))
MASKED_SCORE = -1e30
LOG2_E = math.log2(math.e)


def _params(*semantics):
    return pltpu.CompilerParams(dimension_semantics=semantics, vmem_limit_bytes=V7X_VMEM_LIMIT_BYTES)


def _resident(shape, index_map):
    return pl.BlockSpec(shape, index_map, pipeline_mode=pl.Buffered(1))


def _dot(a, b):
    return jnp.dot(a, b, preferred_element_type=F32)


def _dot_nt(a, b):
    return lax.dot_general(a, b, (((1,), (1,)), ((), ())), preferred_element_type=F32)


def _rms(x, w):
    return x * lax.rsqrt(jnp.mean(x * x, axis=-1, keepdims=True) + NORM_EPS) * w


def _inproj_kernel(x_ref, nw_ref, w_ref, wt_ref, kvn_ref, wuk_ref, wuvT_ref,
                   ret_ref, dq_ref, dk_ref, dv_ref, kidx_ref, qidxT_ref, dsaqT_ref, widxT_ref, k_ref, vT_ref):
    h = _rms(x_ref[...], nw_ref[...]).astype(BF16)
    o = 0
    for ref in (ret_ref, dq_ref, dk_ref, dv_ref, kidx_ref):
        n = ref.shape[-1]
        ref[...] = _dot(h, w_ref[:, o:o + n]).astype(ref.dtype)
        o += n
    c = _rms(_dot(h, w_ref[:, o:o + KV_LATENT]), kvn_ref[...]).astype(BF16)
    for hd in range(DSA_HEADS):
        k_ref[hd] = _dot(c, wuk_ref[:, hd * HEAD_DIM:(hd + 1) * HEAD_DIM]).astype(k_ref.dtype)
    for j in range(vT_ref.shape[0]):
        vT_ref[j] = _dot_nt(wuvT_ref[...], c[j * DSA_KC:(j + 1) * DSA_KC, :]).astype(vT_ref.dtype)
    o = 0
    for ref in (qidxT_ref, dsaqT_ref, widxT_ref):
        n = ref.shape[-2]
        ref[0] = _dot_nt(wt_ref[o:o + n, :], h).astype(ref.dtype)
        o += n


def _inproj(x2, nw, w_main, w_t, kvn, w_uk, w_uvT, B, S):
    T, D = x2.shape
    tm = ROW_TILE
    nt = S // tm
    assert tm % DSA_KC == 0
    widx_rows = w_t.shape[0] - IDX_HEADS * IDX_DIM - DSA_W
    row = lambda i: (i, 0)
    tr = lambda i: (i // nt, 0, i % nt)
    const = lambda i: (0, 0)
    out_shape = (
        jax.ShapeDtypeStruct((T, 4 * RET_W), F32),
        jax.ShapeDtypeStruct((T, DIL_W), BF16),
        jax.ShapeDtypeStruct((T, DIL_W), BF16),
        jax.ShapeDtypeStruct((T, DIL_W), BF16),
        jax.ShapeDtypeStruct((T, IDX_DIM), BF16),
        jax.ShapeDtypeStruct((B, IDX_HEADS * IDX_DIM, S), BF16),
        jax.ShapeDtypeStruct((B, DSA_W, S), F32),
        jax.ShapeDtypeStruct((B, widx_rows, S), F32),
        jax.ShapeDtypeStruct((DSA_HEADS, T, HEAD_DIM), BF16),
        jax.ShapeDtypeStruct((T // DSA_KC, DSA_W, DSA_KC), BF16),
    )
    out_specs = (
        pl.BlockSpec((tm, 4 * RET_W), row),
        pl.BlockSpec((tm, DIL_W), row),
        pl.BlockSpec((tm, DIL_W), row),
        pl.BlockSpec((tm, DIL_W), row),
        pl.BlockSpec((tm, IDX_DIM), row),
        pl.BlockSpec((1, IDX_HEADS * IDX_DIM, tm), tr),
        pl.BlockSpec((1, DSA_W, tm), tr),
        pl.BlockSpec((1, widx_rows, tm), tr),
        pl.BlockSpec((DSA_HEADS, tm, HEAD_DIM), lambda i: (0, i, 0)),
        pl.BlockSpec((tm // DSA_KC, DSA_W, DSA_KC), lambda i: (i, 0, 0)),
    )
    return pl.pallas_call(
        _inproj_kernel,
        grid=(T // tm,),
        in_specs=[
            pl.BlockSpec((tm, D), row),
            _resident((1, D), const),
            _resident(w_main.shape, const),
            _resident(w_t.shape, const),
            _resident((1, KV_LATENT), const),
            _resident(w_uk.shape, const),
            _resident(w_uvT.shape, const),
        ],
        out_specs=out_specs,
        out_shape=out_shape,
        compiler_params=_params("parallel"),
        name="inproj",
    )(x2, nw, w_main, w_t, kvn, w_uk, w_uvT)


def _retention_tables(S):
    half = HEAD_DIM // 2
    inv = RET_ROPE_BASE ** (-jnp.arange(half, dtype=F32) / half)
    ang = jnp.arange(S, dtype=F32)[:, None] * inv[None, :]
    cos, sin = jnp.cos(ang), jnp.sin(ang)
    cos_h = jnp.concatenate([cos, cos], axis=-1)
    sin_h = jnp.concatenate([-sin, sin], axis=-1)
    cosf = jnp.tile(cos_h, (1, RET_HEADS))
    sinf = jnp.tile(sin_h, (1, RET_HEADS))
    C = RET_CHUNK
    log_g = np.log(1.0 - 2.0 ** (-5.0 - np.arange(RET_HEADS, dtype=np.float64)))
    i = np.arange(C, dtype=np.float64)
    diff = i[:, None] - i[None, :]
    decay = np.exp(np.maximum(diff, 0.0)[None] * log_g[:, None, None]) * (diff >= 0)[None]
    zeta = np.exp((C - 1.0 - i)[None, :] * log_g[:, None])
    xi = np.exp((i + 1.0)[None, :] * log_g[:, None])
    g_chunk = np.exp(C * log_g)
    rep = lambda t: np.repeat(t.T, HEAD_DIM, axis=1)
    g_rows = np.repeat(g_chunk, HEAD_DIM)[:, None] * np.ones((1, RET_W))
    return (cosf, sinf, jnp.asarray(decay, F32), jnp.asarray(rep(zeta), F32), jnp.asarray(rep(xi), F32),
            jnp.asarray(g_rows, F32))


def _retention_kernel(cols_ref, cos_ref, sin_ref, decay_ref, zeta_ref, xi_ref, grow_ref, nw_ref, o_ref, state_ref):
    W = RET_W

    @pl.when(pl.program_id(1) == 0)
    def _():
        state_ref[...] = jnp.zeros_like(state_ref)

    C = RET_CHUNK
    lane = lax.broadcasted_iota(I32, (1, W), 1)
    first_half = (lane % HEAD_DIM) < (HEAD_DIM // 2)
    head_of_lane = lane // HEAD_DIM
    r_head = lax.broadcasted_iota(I32, (W, W), 0) // HEAD_DIM
    c_head = lax.broadcasted_iota(I32, (W, W), 1) // HEAD_DIM
    same_head = r_head == c_head

    def rot(t, rows):
        partner = jnp.where(first_half, pltpu.roll(t, W - HEAD_DIM // 2, 1), pltpu.roll(t, HEAD_DIM // 2, 1))
        return t * cos_ref[rows, :] + partner * sin_ref[rows, :]

    qs, inners, kvs = [], [], []
    for j in range(RET_STEP_CHUNKS):
        rows = slice(j * C, (j + 1) * C)
        q = rot(cols_ref[rows, 0:W], rows)
        k = rot(cols_ref[rows, W:2 * W], rows) * (HEAD_DIM ** -0.5)
        v = cols_ref[rows, 2 * W:3 * W].astype(BF16)
        kb = k.astype(BF16)
        inner = jnp.zeros((C, W), F32)
        for h in range(RET_HEADS):
            sel = head_of_lane == h
            qh = jnp.where(sel, q, 0.0).astype(BF16)
            a = (_dot_nt(qh, kb) * decay_ref[h]).astype(BF16)
            inner = inner + jnp.where(sel, _dot(a, v), 0.0)
        kzT = (k * zeta_ref[...]).T.astype(BF16)
        kv = _dot(kzT, v)
        qs.append(q.astype(BF16))
        inners.append(inner)
        kvs.append(jnp.where(same_head, kv, 0.0))

    state = state_ref[...]
    for j in range(RET_STEP_CHUNKS):
        rows = slice(j * C, (j + 1) * C)
        o = inners[j] + _dot(qs[j], state.astype(BF16)) * xi_ref[...]
        state = grow_ref[...] * state + kvs[j]

        mu = jnp.zeros_like(o)
        for h in range(RET_HEADS):
            sel = head_of_lane == h
            mu = mu + jnp.where(sel, jnp.sum(jnp.where(sel, o, 0.0), axis=-1, keepdims=True), 0.0)
        mu = mu * (1.0 / HEAD_DIM)
        d = o - mu
        var = jnp.zeros_like(o)
        for h in range(RET_HEADS):
            sel = head_of_lane == h
            var = var + jnp.where(sel, jnp.sum(jnp.where(sel, d * d, 0.0), axis=-1, keepdims=True), 0.0)
        var = var * (1.0 / HEAD_DIM)
        y = d * lax.rsqrt(var + NORM_EPS) * nw_ref[...]
        g = cols_ref[rows, 3 * W:4 * W]
        o_ref[rows, :] = (jax.nn.silu(g) * y).astype(o_ref.dtype)
    state_ref[...] = state


def _retention(ret_cols, tables, norm_w, B, S):
    T = ret_cols.shape[0]
    C, W = RET_CHUNK, RET_W
    rows = RET_STEP_CHUNKS * C
    assert S % rows == 0
    n = S // rows
    cosf, sinf, decay, zeta, xi, g_rows = tables
    tok = lambda b, j: (b * n + j, 0)
    pos = lambda b, j: (j, 0)
    const2 = lambda b, j: (0, 0)
    return pl.pallas_call(
        _retention_kernel,
        grid=(B, n),
        in_specs=[
            pl.BlockSpec((rows, 4 * W), tok),
            pl.BlockSpec((rows, W), pos),
            pl.BlockSpec((rows, W), pos),
            _resident(decay.shape, lambda b, j: (0, 0, 0)),
            _resident((C, W), const2),
            _resident((C, W), const2),
            _resident((W, W), const2),
            _resident((1, W), const2),
        ],
        out_specs=pl.BlockSpec((rows, W), tok),
        out_shape=jax.ShapeDtypeStruct((T, W), BF16),
        scratch_shapes=[pltpu.VMEM((W, W), F32)],
        compiler_params=_params("parallel", "arbitrary"),
        name="retention",
    )(ret_cols, cosf, sinf, decay, zeta, xi, g_rows, norm_w)


def _sortable(x):
    b = pltpu.bitcast(x, I32)
    return b ^ ((b >> 31) & 0x7FFFFFFF)


def _dsa_kernel(*refs, n_sel, S):
    odd = pl.program_id(1) % 2 == 0
    pl.when(odd)(lambda: _dsa_block(*refs, n_sel=n_sel, S=S, lead=1))
    pl.when(jnp.logical_not(odd))(lambda: _dsa_block(*refs, n_sel=n_sel, S=S, lead=0))


def _dsa_block(qidxT_ref, dsaqT_ref, widxT_ref, kidx_ref, k_ref, vT_ref, o_ref,
               key_ref, hi_ref, lo_ref, digit_ref, tied_ref, tri_ref, qh_ref, s_ref, bias_ref, p_ref, acc_ref,
               *, n_sel, S, lead):
    QB, KC, SUB = DSA_QB, DSA_KC, DSA_SUB
    n_sub = KC // SUB
    i = pl.program_id(1)
    n_chunks = i + 1
    n_pairs = (n_chunks - lead) // 2
    pair_off = lambda c2: pl.multiple_of(c2 * (2 * KC) + lead * KC, KC)
    chunk_off = lambda c: c * KC if isinstance(c, int) else pl.multiple_of(c * KC, KC)
    qpos = i * QB + lax.broadcasted_iota(I32, (1, QB), 1)
    krow = lax.broadcasted_iota(I32, (SUB, 1), 0)
    w_scale = IDX_HEADS ** -0.5 * IDX_DIM ** -0.5
    widxT = widxT_ref[0] * w_scale

    def score_rows(off, n_rows, causal_mask):
        for sub in range(n_rows // DSA_SCORE_ROWS):
            r0 = off + sub * DSA_SCORE_ROWS
            kc = kidx_ref[pl.ds(r0, DSA_SCORE_ROWS), :]
            acc = jnp.zeros((DSA_SCORE_ROWS, QB), F32)
            for g in range(0, IDX_HEADS, DSA_IDX_GROUP):
                logits = [_dot(kc, qidxT_ref[0, h * IDX_DIM:(h + 1) * IDX_DIM, :])
                          for h in range(g, g + DSA_IDX_GROUP)]
                for h, logit in zip(range(g, g + DSA_IDX_GROUP), logits):
                    acc = acc + jnp.maximum(logit, 0.0) * widxT[h:h + 1, :]
            for part in range(DSA_SCORE_ROWS // SUB):
                a = acc[part * SUB:(part + 1) * SUB, :]
                a = jnp.where(a == 0.0, 0.0, a)
                if causal_mask:
                    a = jnp.where(r0 + part * SUB + krow <= qpos, a, -jnp.inf)
                key = _sortable(a)
                rows = pl.ds(r0 + part * SUB, SUB)
                key_ref[rows, :] = key
                hi_ref[rows, :] = (key >> 16).astype(I16)
                lo_ref[rows, :] = ((key & 0xFFFF) - 0x8000).astype(I16)

    def score_pair(c2, carry, causal_mask):
        score_rows(pair_off(c2), 2 * KC, causal_mask)
        return carry

    if lead:
        score_rows(0, KC, True)
        lax.fori_loop(0, n_pairs - 1, functools.partial(score_pair, causal_mask=False), 0)
        pl.when(n_pairs > 0)(lambda: score_rows(pair_off(n_pairs - 1), 2 * KC, True))
    else:
        lax.fori_loop(0, n_pairs - 1, functools.partial(score_pair, causal_mask=False), 0)
        score_pair(n_pairs - 1, 0, causal_mask=True)

    one_b, zero_b = jnp.ones((SUB, QB), BF16), jnp.zeros((SUB, QB), BF16)

    def count_packed(ref, pred):
        def tiles(cnt, off, n_tiles):
            for sub in range(n_tiles):
                cnt = cnt + jnp.where(pred(ref[pl.ds(off + sub * SUB, SUB), :]), one_b, zero_b)
            return cnt
        cnt = tiles(zero_b, 0, n_sub) if lead else zero_b
        cnt = lax.fori_loop(0, n_pairs, lambda c2, cnt: tiles(cnt, pair_off(c2), 2 * n_sub), cnt)
        return jnp.sum(cnt.astype(F32), axis=0, keepdims=True)

    def signed_digit(u):
        return (u - 0x8000).astype(I16)

    def search_digit(ref, n_above, n_ge_zero):
        def step(t, carry):
            d, n_ge = carry
            cand = d | jnp.left_shift(jnp.int32(1), 15 - t)
            cand_s = signed_digit(cand)
            cnt = n_above + count_packed(ref, lambda x: x >= cand_s)
            keep = cnt >= n_sel
            return jnp.where(keep, cand, d), jnp.where(keep, cnt, n_ge)
        return lax.fori_loop(0, 16, step, (jnp.zeros((1, QB), I32), n_ge_zero))

    stored = jnp.full((1, QB), (n_chunks * KC).astype(F32), F32)
    u_hi, n_ge = search_digit(hi_ref, 0.0, stored)
    thr_hi = signed_digit(u_hi)
    n_above = count_packed(hi_ref, lambda x: x > thr_hi)
    lowest = jnp.full((SUB, QB), -0x8000, I16)

    def low_digits_of_ties(off, n_tiles):
        for sub in range(n_tiles):
            rows = pl.ds(off + sub * SUB, SUB)
            digit_ref[rows, :] = jnp.where(hi_ref[rows, :] == thr_hi, lo_ref[rows, :], lowest)

    def low_digits_pair(c2, carry):
        low_digits_of_ties(pair_off(c2), 2 * n_sub)
        return carry

    if lead:
        low_digits_of_ties(0, n_sub)
    lax.fori_loop(0, n_pairs, low_digits_pair, 0)
    u_lo, n_ge = search_digit(digit_ref, n_above, n_ge)
    thr = ((u_hi - 0x8000) << 16) | u_lo

    needs_cut = (n_ge > n_sel) & (thr > KEY_NEG_INF)
    any_ties = jnp.max(jnp.where(needs_cut, 1.0, 0.0)) > 0.0

    qh_ref[...] = (dsaqT_ref[0] * (HEAD_DIM ** -0.5 * LOG2_E)).astype(BF16)
    acc_ref[...] = jnp.zeros_like(acc_ref)
    p_ref[...] = jnp.zeros_like(p_ref)

    thr_floor = jnp.maximum(thr, KEY_NEG_INF + 1)

    def write_bias(slot, c):
        off = chunk_off(c)
        for sub in range(n_sub):
            key = key_ref[pl.ds(off + sub * SUB, SUB), :]
            bias_ref[slot, sub * SUB:(sub + 1) * SUB, :] = jnp.where(key >= thr_floor, 0.0, MASKED_SCORE)

    def write_bias_ties(slot, c, tie_take, taken):
        off = chunk_off(c)
        for sub in range(n_sub):
            key = key_ref[pl.ds(off + sub * SUB, SUB), :]
            tied_ref[sub * SUB:(sub + 1) * SUB, :] = jnp.where(key == thr, 1.0, 0.0).astype(BF16)
        s_ref[slot, 0] = _dot(tri_ref[...], tied_ref[...])
        room = tie_take - taken
        for sub in range(n_sub):
            rows = slice(sub * SUB, (sub + 1) * SUB)
            key = key_ref[pl.ds(off + sub * SUB, SUB), :]
            tied_bias = jnp.where(s_ref[slot, 0, rows, :] < room, 0.0, MASKED_SCORE)
            bias_ref[slot, rows, :] = jnp.where(key > thr, 0.0, jnp.where(key == thr, tied_bias, MASKED_SCORE))
        return taken + s_ref[slot, 0, KC - 1:KC, :] + tied_ref[KC - 1:KC, :].astype(F32)

    def issue_scores(slot, c):
        rows = pl.ds(chunk_off(c), KC)
        for h in range(DSA_HEADS):
            s_ref[slot, h] = _dot(k_ref[h, rows, :], qh_ref[h * HEAD_DIM:(h + 1) * HEAD_DIM, :])

    def pv_update(slot, c, alphas):
        vch = vT_ref[c]
        pvs = [_dot(vch[h * HEAD_DIM:(h + 1) * HEAD_DIM, :], p_ref[slot, h]) for h in range(DSA_HEADS)]
        for h in range(DSA_HEADS):
            acc_ref[h] = alphas[h] * acc_ref[h] + pvs[h]

    def softmax_chunk(slot, ms, ls):
        new_m, new_l, alphas = [], [], []
        for h in range(DSA_HEADS):
            mx = jnp.full((SUB, QB), MASKED_SCORE, F32)
            for sub in range(n_sub):
                rows = slice(sub * SUB, (sub + 1) * SUB)
                t = s_ref[slot, h, rows, :] + bias_ref[slot, rows, :]
                s_ref[slot, h, rows, :] = t
                mx = jnp.maximum(mx, t)
            m_new = jnp.maximum(ms[h], jnp.max(mx, axis=0, keepdims=True))
            alpha = jnp.exp2(ms[h] - m_new)
            psum = jnp.zeros((SUB, QB), F32)
            for sub in range(n_sub):
                rows = slice(sub * SUB, (sub + 1) * SUB)
                p = jnp.exp2(s_ref[slot, h, rows, :] - m_new)
                psum = psum + p
                p_ref[slot, h, rows, :] = p.astype(BF16)
            new_m.append(m_new)
            new_l.append(alpha * ls[h] + jnp.sum(psum, axis=0, keepdims=True))
            alphas.append(alpha)
        return tuple(new_m), tuple(new_l), tuple(alphas)

    def attend(ties):
        if ties:
            thr_lo = signed_digit(u_lo)
            n_gt = n_above + count_packed(digit_ref, lambda x: x > thr_lo)
            tie_take = jnp.where(thr > KEY_NEG_INF, n_sel - n_gt, 0.0)
            ri = lax.broadcasted_iota(I32, (KC, KC), 0)
            ci = lax.broadcasted_iota(I32, (KC, KC), 1)
            tri_ref[...] = jnp.where(ci < ri, 1.0, 0.0).astype(BF16)

        def attend_pair(c2, carry):
            ms, ls, prev_alphas, taken = carry
            ca, cb = lead + 2 * c2, lead + 2 * c2 + 1
            if ties:
                taken = write_bias_ties(0, ca, tie_take, taken)
                taken = write_bias_ties(1, cb, tie_take, taken)
            else:
                write_bias(0, ca)
                write_bias(1, cb)
            issue_scores(0, ca)
            pv_update(1, jnp.maximum(ca - 1, 0), prev_alphas)
            issue_scores(1, cb)
            ms, ls, alphas_a = softmax_chunk(0, ms, ls)
            pv_update(0, ca, alphas_a)
            ms, ls, alphas_b = softmax_chunk(1, ms, ls)
            return ms, ls, alphas_b, taken

        init = (tuple(jnp.full((1, QB), MASKED_SCORE, F32) for _ in range(DSA_HEADS)),
                tuple(jnp.zeros((1, QB), F32) for _ in range(DSA_HEADS)),
                tuple(jnp.ones((1, QB), F32) for _ in range(DSA_HEADS)),
                jnp.zeros((1, QB), F32))
        if lead:
            ms, ls, _, taken = init
            if ties:
                taken = write_bias_ties(1, 0, tie_take, taken)
            else:
                write_bias(1, 0)
            issue_scores(1, 0)
            ms, ls, lead_alphas = softmax_chunk(1, ms, ls)
            init = (ms, ls, lead_alphas, taken)
        _, ls, last_alphas, _ = lax.fori_loop(0, n_pairs, attend_pair, init)
        pv_update(1, n_chunks - 1, last_alphas)
        for h in range(DSA_HEADS):
            acc_ref[h] = acc_ref[h] / ls[h]

    pl.when(any_ties)(lambda: attend(True))
    pl.when(jnp.logical_not(any_ties))(lambda: attend(False))
    o_ref[...] = acc_ref[...].reshape(DSA_W, QB).T.astype(o_ref.dtype)


def _dsa(qidxT, dsaqT, widxT, kidx, k, vT, B, S):
    QB, KC = DSA_QB, DSA_KC
    assert QB == KC and S % (2 * KC) == 0
    n_sel = min(TOPK_MAX, S // 4)
    assert n_sel <= KC, "the first causal chunk must be able to hold every selected key"
    assert S // DSA_SUB <= 256, "packed counts are exact only up to 256 adds per accumulator lane"
    nq = S // QB
    nkc = S // KC
    n_sel = min(TOPK_MAX, S // 4)
    T = B * S
    kern = functools.partial(_dsa_kernel, n_sel=n_sel, S=S)
    return pl.pallas_call(
        kern,
        grid=(B, nq),
        in_specs=[
            pl.BlockSpec((1, IDX_HEADS * IDX_DIM, QB), lambda b, i: (b, 0, i)),
            pl.BlockSpec((1, DSA_W, QB), lambda b, i: (b, 0, i)),
            pl.BlockSpec((1, widxT.shape[1], QB), lambda b, i: (b, 0, i)),
            _resident((S, IDX_DIM), lambda b, i: (b, 0)),
            _resident((DSA_HEADS, S, HEAD_DIM), lambda b, i: (0, b, 0)),
            _resident((nkc, DSA_W, KC), lambda b, i: (b, 0, 0)),
        ],
        out_specs=pl.BlockSpec((QB, DSA_W), lambda b, i: (b * nq + i, 0)),
        out_shape=jax.ShapeDtypeStruct((T, DSA_W), BF16),
        scratch_shapes=[
            pltpu.VMEM((S, QB), I32),
            pltpu.VMEM((S, QB), I16),
            pltpu.VMEM((S, QB), I16),
            pltpu.VMEM((S, QB), I16),
            pltpu.VMEM((KC, QB), BF16),
            pltpu.VMEM((KC, KC), BF16),
            pltpu.VMEM((DSA_W, QB), BF16),
            pltpu.VMEM((2, DSA_HEADS, KC, QB), F32),
            pltpu.VMEM((2, KC, QB), F32),
            pltpu.VMEM((2, DSA_HEADS, KC, QB), BF16),
            pltpu.VMEM((DSA_HEADS, HEAD_DIM, QB), F32),
        ],
        compiler_params=_params("parallel", "arbitrary"),
        name="dsa",
    )(qidxT, dsaqT, widxT, kidx, k, vT)


def _dil_units(dilation):
    blocks = DIL_SPAN // (DIL_BLOCK * dilation)
    return [(res, jb) for res in range(dilation) for jb in range(blocks)]


def _dilated_fused_kernel(q_ref, kp_ref, kc_ref, vp_ref, vc_ref, out_ref,
                          qf_ref, kf_ref, vf_ref, s_ref, p_ref, num_ref, den_ref, max_ref):
    L, SPAN = DIL_BLOCK, DIL_SPAN
    first_span = pl.program_id(1) == 0
    qf_ref[...] = q_ref[...].astype(F32) * (HEAD_DIM ** -0.5)
    kf_ref[0:SPAN, :] = kp_ref[...].astype(F32)
    kf_ref[SPAN:, :] = kc_ref[...].astype(F32)
    vf_ref[0:SPAN, :] = vp_ref[...].astype(F32)
    vf_ref[SPAN:, :] = vc_ref[...].astype(F32)

    lane = lax.broadcasted_iota(I32, (1, 2 * HEAD_DIM), 1)
    low = lane < HEAD_DIM
    qi = lax.broadcasted_iota(I32, (L, 2 * L), 0)
    kj = lax.broadcasted_iota(I32, (L, 2 * L), 1)
    dist = L + qi - kj
    band = (dist >= 0) & (dist <= L)
    bias_band = jnp.where(band, 0.0, -jnp.inf)
    bias_head = jnp.where(band & ((kj >= L) | jnp.logical_not(first_span)), 0.0, -jnp.inf)

    def rows(start, count, stride):
        return pl.ds(start, count, stride=stride) if stride > 1 else pl.ds(start, count)

    for branch, (window, r) in enumerate(DIL_PATTERNS):
        units = _dil_units(r)
        for g0 in range(0, len(units), DIL_GROUP):
            group = units[g0:g0 + DIL_GROUP]
            vvs = []
            for u, (res, jb) in enumerate(group):
                q = qf_ref[rows(res + r * L * jb, L, r), :].astype(BF16)
                k0 = SPAN + res + r * L * (jb - 1)
                kk = kf_ref[rows(k0, 2 * L, r), :].astype(BF16)
                vv = vf_ref[rows(k0, 2 * L, r), :]
                for sub in range(2):
                    mine = low if sub == 0 else jnp.logical_not(low)
                    s_ref[2 * u + sub] = _dot_nt(jnp.where(mine, q, jnp.zeros_like(q)), kk)
                    vvs.append(jnp.where(mine, vv, 1.0).astype(BF16))
            ms = []
            for u, (res, jb) in enumerate(group):
                bias = bias_head if jb == 0 else bias_band
                for sub in range(2):
                    t = s_ref[2 * u + sub] + bias
                    m = jnp.max(t, axis=-1, keepdims=True)
                    p_ref[2 * u + sub] = jnp.exp(t - m).astype(BF16)
                    ms.append(m)
            for u, (res, jb) in enumerate(group):
                halves = []
                for sub in range(2):
                    nd = _dot(p_ref[2 * u + sub], vvs[2 * u + sub])
                    halves.append((nd, pltpu.roll(nd, HEAD_DIM, 1), ms[2 * u + sub]))
                (n0, d0, m0), (n1, d1, m1) = halves
                dst = rows(res + r * L * jb, L, r)
                num_ref[branch, dst, :] = jnp.where(low, n0, n1)
                den_ref[branch, dst, :] = jnp.where(low, d0, d1)
                max_ref[branch, dst, :] = jnp.where(low, m0, m1)

    for piece in range(SPAN // DIL_MERGE_ROWS):
        rs = slice(piece * DIL_MERGE_ROWS, (piece + 1) * DIL_MERGE_ROWS)
        a, b, c = max_ref[0, rs, :], max_ref[1, rs, :], max_ref[2, rs, :]
        m = jnp.maximum(jnp.maximum(a, b), c)
        ea, eb, ec = jnp.exp(a - m), jnp.exp(b - m), jnp.exp(c - m)
        num = ea * num_ref[0, rs, :] + eb * num_ref[1, rs, :] + ec * num_ref[2, rs, :]
        den = ea * den_ref[0, rs, :] + eb * den_ref[1, rs, :] + ec * den_ref[2, rs, :]
        out_ref[rs, :] = (num / den).astype(out_ref.dtype)


def _dilated_fused(q, k, v, B, S):
    T, W = q.shape
    SPAN, L = DIL_SPAN, DIL_BLOCK
    for window, r in DIL_PATTERNS:
        assert window // r == L and SPAN % (L * r) == 0 and L * r <= SPAN
    assert S % SPAN == 0
    ns = S // SPAN
    PW = 2 * HEAD_DIM
    own = lambda b, t, hp: (b * ns + t, hp)
    prev = lambda b, t, hp: (b * ns + jnp.maximum(t - 1, 0), hp)
    blk = (SPAN, PW)
    return pl.pallas_call(
        _dilated_fused_kernel,
        grid=(B, ns, W // PW),
        in_specs=[pl.BlockSpec(blk, own), pl.BlockSpec(blk, prev), pl.BlockSpec(blk, own),
                  pl.BlockSpec(blk, prev), pl.BlockSpec(blk, own)],
        out_specs=pl.BlockSpec(blk, own),
        out_shape=jax.ShapeDtypeStruct((T, W), BF16),
        scratch_shapes=[
            pltpu.VMEM((SPAN, PW), F32),
            pltpu.VMEM((2 * SPAN, PW), F32),
            pltpu.VMEM((2 * SPAN, PW), F32),
            pltpu.VMEM((2 * DIL_GROUP, L, 2 * L), F32),
            pltpu.VMEM((2 * DIL_GROUP, L, 2 * L), BF16),
            pltpu.VMEM((len(DIL_PATTERNS), SPAN, PW), F32),
            pltpu.VMEM((len(DIL_PATTERNS), SPAN, PW), F32),
            pltpu.VMEM((len(DIL_PATTERNS), SPAN, PW), F32),
        ],
        compiler_params=_params("parallel", "parallel", "parallel"),
        name="dilated",
    )(q, k, k, v, v)


def _mlp_kernel(x_ref, ret_ref, dsa_ref, dil_ref, wout_ref, nw_ref, wup_ref, wdown_ref, fnw_ref, o_ref, *,
                apply_final_norm):
    mixed = (_dot(ret_ref[...], wout_ref[0:RET_W, :])
             + _dot(dsa_ref[...], wout_ref[RET_W:RET_W + DSA_W, :])
             + _dot(dil_ref[...], wout_ref[RET_W + DSA_W:, :]))
    x = x_ref[...] + mixed
    h = _rms(x, nw_ref[...]).astype(BF16)
    ff = None
    for c in range(wup_ref.shape[1] // FF_CHUNK):
        cs = slice(c * FF_CHUNK, (c + 1) * FF_CHUNK)
        u = jnp.maximum(_dot(h, wup_ref[:, cs]), 0.0)
        part = _dot((u * u).astype(BF16), wdown_ref[cs, :])
        ff = part if ff is None else ff + part
    out = x + ff
    o_ref[...] = _rms(out, fnw_ref[...]) if apply_final_norm else out


def _mlp(x2, ret_o, dsa_o, dil_o, w_out, nw, w_up, w_down, final_nw, *, apply_final_norm):
    T, D = x2.shape
    tm = ROW_TILE
    row = lambda i: (i, 0)
    const = lambda i: (0, 0)
    return pl.pallas_call(
        functools.partial(_mlp_kernel, apply_final_norm=apply_final_norm),
        grid=(T // tm,),
        in_specs=[
            pl.BlockSpec((tm, D), row),
            pl.BlockSpec((tm, RET_W), row),
            pl.BlockSpec((tm, DSA_W), row),
            pl.BlockSpec((tm, DIL_W), row),
            _resident(w_out.shape, const),
            _resident((1, D), const),
            _resident(w_up.shape, const),
            _resident(w_down.shape, const),
            _resident((1, D), const),
        ],
        out_specs=pl.BlockSpec((tm, D), row),
        out_shape=jax.ShapeDtypeStruct((T, D), F32),
        compiler_params=_params("parallel"),
        name="outproj_mlp",
    )(x2, ret_o, dsa_o, dil_o, w_out, nw, w_up, w_down, final_nw)


def _split_w_in(w_in):
    ret_cols = 4 * RET_W
    o = ret_cols
    dsa_q = w_in[:, o:o + DSA_W]; o += DSA_W
    c_kv = w_in[:, o:o + KV_LATENT]; o += KV_LATENT
    q_idx = w_in[:, o:o + IDX_HEADS * IDX_DIM]; o += IDX_HEADS * IDX_DIM
    k_idx = w_in[:, o:o + IDX_DIM]; o += IDX_DIM
    w_idx = w_in[:, o:o + IDX_HEADS]; o += IDX_HEADS
    dil = w_in[:, o:]
    w_main = jnp.concatenate([w_in[:, :ret_cols], dil, k_idx, c_kv], axis=1).astype(BF16)
    pad = jnp.zeros((w_in.shape[0], WIDX_ROWS - IDX_HEADS), w_in.dtype)
    w_t = jnp.concatenate([q_idx, dsa_q, w_idx, pad], axis=1).T.astype(BF16)
    return w_main, w_t


def kernel(x, attn_norm_w, w_in, ret_norm_w, dsa_kv_norm_w, dsa_w_uk, dsa_w_uv, w_out, mlp_norm_w, w_up, w_down, final_norm_w):
    B, S, D = x.shape
    depth = w_in.shape[0]
    assert S % ROW_TILE == 0 and S % DSA_KC == 0 and S % RET_CHUNK == 0
    tables = _retention_tables(S)
    x2 = x.reshape(B * S, D)
    for layer in range(depth):
        w_main, w_t = _split_w_in(w_in[layer])
        ret_cols, dq, dk, dv, kidx, qidxT, dsaqT, widxT, k, vT = _inproj(
            x2, attn_norm_w[layer][None, :], w_main, w_t, dsa_kv_norm_w[layer][None, :],
            dsa_w_uk[layer].astype(BF16), dsa_w_uv[layer].T.astype(BF16), B, S)
        ret_o = _retention(ret_cols, tables, ret_norm_w[layer].reshape(1, RET_W), B, S)
        dsa_o = _dsa(qidxT, dsaqT, widxT, kidx, k, vT, B, S)
        dil_o = _dilated_fused(dq, dk, dv, B, S)
        x2 = _mlp(x2, ret_o, dsa_o, dil_o, w_out[layer].astype(BF16), mlp_norm_w[layer][None, :],
                  w_up[layer].astype(BF16), w_down[layer].astype(BF16), final_norm_w[None, :],
                  apply_final_norm=layer == depth - 1)
    return x2.reshape(B, S, D)
```

```python
import functools
import math

import numpy as np
import jax
import jax.numpy as jnp
from jax import lax
from jax.experimental import pallas as pl
from jax.experimental.pallas import tpu as pltpu

F32 = jnp.float32
BF16 = jnp.bfloat16
I32 = jnp.int32
I16 = jnp.int16

HEAD_DIM = 64
RET_HEADS = 4
DSA_HEADS = 4
DIL_HEADS = 8
RET_W = RET_HEADS * HEAD_DIM
DSA_W = DSA_HEADS * HEAD_DIM
DIL_W = DIL_HEADS * HEAD_DIM
RET_CHUNK = 128
RET_STEP_CHUNKS = 8
RET_ROPE_BASE = 10000.0
KV_LATENT = 128
IDX_HEADS = 8
IDX_DIM = 64
TOPK_MAX = 256
DIL_PATTERNS = ((128, 1), (512, 4), (2048, 16))
DIL_BLOCK = 128
DIL_SPAN = 2048
DIL_GROUP = 4
DIL_MERGE_ROWS = 256
NORM_EPS = 1e-6

V7X_VMEM_LIMIT_BYTES = 56 * 1024 * 1024

ROW_TILE = 512
DSA_QB = 256
DSA_KC = 256
DSA_SUB = 64
DSA_SCORE_ROWS = 128
DSA_IDX_GROUP = 4
FF_CHUNK = 1024

WIDX_ROWS = 16

KEY_NEG_INF = int(np.int32(np.uint32(0xFF800000) ^ np.uint32(0x7FFFFFFF)))
MASKED_SCORE = -1e30
LOG2_E = math.log2(math.e)


def _params(*semantics):
    return pltpu.CompilerParams(dimension_semantics=semantics, vmem_limit_bytes=V7X_VMEM_LIMIT_BYTES)


def _resident(shape, index_map):
    return pl.BlockSpec(shape, index_map, pipeline_mode=pl.Buffered(1))


def _dot(a, b):
    return jnp.dot(a, b, preferred_element_type=F32)


def _dot_nt(a, b):
    return lax.dot_general(a, b, (((1,), (1,)), ((), ())), preferred_element_type=F32)


def _rms(x, w):
    return x * lax.rsqrt(jnp.mean(x * x, axis=-1, keepdims=True) + NORM_EPS) * w


def _inproj_kernel(x_ref, nw_ref, w_ref, wt_ref, kvn_ref, wuk_ref, wuvT_ref,
                   ret_ref, dq_ref, dk_ref, dv_ref, kidx_ref, qidxT_ref, dsaqT_ref, widxT_ref, k_ref, vT_ref):
    h = _rms(x_ref[...], nw_ref[...]).astype(BF16)
    o = 0
    for ref in (ret_ref, dq_ref, dk_ref, dv_ref, kidx_ref):
        n = ref.shape[-1]
        ref[...] = _dot(h, w_ref[:, o:o + n]).astype(ref.dtype)
        o += n
    c = _rms(_dot(h, w_ref[:, o:o + KV_LATENT]), kvn_ref[...]).astype(BF16)
    for hd in range(DSA_HEADS):
        k_ref[hd] = _dot(c, wuk_ref[:, hd * HEAD_DIM:(hd + 1) * HEAD_DIM]).astype(k_ref.dtype)
    for j in range(vT_ref.shape[0]):
        vT_ref[j] = _dot_nt(wuvT_ref[...], c[j * DSA_KC:(j + 1) * DSA_KC, :]).astype(vT_ref.dtype)
    o = 0
    for ref in (qidxT_ref, dsaqT_ref, widxT_ref):
        n = ref.shape[-2]
        ref[0] = _dot_nt(wt_ref[o:o + n, :], h).astype(ref.dtype)
        o += n


def _inproj(x2, nw, w_main, w_t, kvn, w_uk, w_uvT, B, S):
    T, D = x2.shape
    tm = ROW_TILE
    nt = S // tm
    assert tm % DSA_KC == 0
    widx_rows = w_t.shape[0] - IDX_HEADS * IDX_DIM - DSA_W
    row = lambda i: (i, 0)
    tr = lambda i: (i // nt, 0, i % nt)
    const = lambda i: (0, 0)
    out_shape = (
        jax.ShapeDtypeStruct((T, 4 * RET_W), F32),
        jax.ShapeDtypeStruct((T, DIL_W), BF16),
        jax.ShapeDtypeStruct((T, DIL_W), BF16),
        jax.ShapeDtypeStruct((T, DIL_W), BF16),
        jax.ShapeDtypeStruct((T, IDX_DIM), BF16),
        jax.ShapeDtypeStruct((B, IDX_HEADS * IDX_DIM, S), BF16),
        jax.ShapeDtypeStruct((B, DSA_W, S), F32),
        jax.ShapeDtypeStruct((B, widx_rows, S), F32),
        jax.ShapeDtypeStruct((DSA_HEADS, T, HEAD_DIM), BF16),
        jax.ShapeDtypeStruct((T // DSA_KC, DSA_W, DSA_KC), BF16),
    )
    out_specs = (
        pl.BlockSpec((tm, 4 * RET_W), row),
        pl.BlockSpec((tm, DIL_W), row),
        pl.BlockSpec((tm, DIL_W), row),
        pl.BlockSpec((tm, DIL_W), row),
        pl.BlockSpec((tm, IDX_DIM), row),
        pl.BlockSpec((1, IDX_HEADS * IDX_DIM, tm), tr),
        pl.BlockSpec((1, DSA_W, tm), tr),
        pl.BlockSpec((1, widx_rows, tm), tr),
        pl.BlockSpec((DSA_HEADS, tm, HEAD_DIM), lambda i: (0, i, 0)),
        pl.BlockSpec((tm // DSA_KC, DSA_W, DSA_KC), lambda i: (i, 0, 0)),
    )
    return pl.pallas_call(
        _inproj_kernel,
        grid=(T // tm,),
        in_specs=[
            pl.BlockSpec((tm, D), row),
            _resident((1, D), const),
            _resident(w_main.shape, const),
            _resident(w_t.shape, const),
            _resident((1, KV_LATENT), const),
            _resident(w_uk.shape, const),
            _resident(w_uvT.shape, const),
        ],
        out_specs=out_specs,
        out_shape=out_shape,
        compiler_params=_params("parallel"),
        name="inproj",
    )(x2, nw, w_main, w_t, kvn, w_uk, w_uvT)


def _retention_tables(S):
    half = HEAD_DIM // 2
    inv = RET_ROPE_BASE ** (-jnp.arange(half, dtype=F32) / half)
    ang = jnp.arange(S, dtype=F32)[:, None] * inv[None, :]
    cos, sin = jnp.cos(ang), jnp.sin(ang)
    cos_h = jnp.concatenate([cos, cos], axis=-1)
    sin_h = jnp.concatenate([-sin, sin], axis=-1)
    cosf = jnp.tile(cos_h, (1, RET_HEADS))
    sinf = jnp.tile(sin_h, (1, RET_HEADS))
    C = RET_CHUNK
    log_g = np.log(1.0 - 2.0 ** (-5.0 - np.arange(RET_HEADS, dtype=np.float64)))
    i = np.arange(C, dtype=np.float64)
    diff = i[:, None] - i[None, :]
    decay = np.exp(np.maximum(diff, 0.0)[None] * log_g[:, None, None]) * (diff >= 0)[None]
    zeta = np.exp((C - 1.0 - i)[None, :] * log_g[:, None])
    xi = np.exp((i + 1.0)[None, :] * log_g[:, None])
    g_chunk = np.exp(C * log_g)
    rep = lambda t: np.repeat(t.T, HEAD_DIM, axis=1)
    g_rows = np.repeat(g_chunk, HEAD_DIM)[:, None] * np.ones((1, RET_W))
    return (cosf, sinf, jnp.asarray(decay, F32), jnp.asarray(rep(zeta), F32), jnp.asarray(rep(xi), F32),
            jnp.asarray(g_rows, F32))


def _retention_kernel(cols_ref, cos_ref, sin_ref, decay_ref, zeta_ref, xi_ref, grow_ref, nw_ref, o_ref, state_ref):
    W = RET_W

    @pl.when(pl.program_id(1) == 0)
    def _():
        state_ref[...] = jnp.zeros_like(state_ref)

    C = RET_CHUNK
    lane = lax.broadcasted_iota(I32, (1, W), 1)
    first_half = (lane % HEAD_DIM) < (HEAD_DIM // 2)
    head_of_lane = lane // HEAD_DIM
    r_head = lax.broadcasted_iota(I32, (W, W), 0) // HEAD_DIM
    c_head = lax.broadcasted_iota(I32, (W, W), 1) // HEAD_DIM
    same_head = r_head == c_head

    def rot(t, rows):
        partner = jnp.where(first_half, pltpu.roll(t, W - HEAD_DIM // 2, 1), pltpu.roll(t, HEAD_DIM // 2, 1))
        return t * cos_ref[rows, :] + partner * sin_ref[rows, :]

    qs, inners, kvs = [], [], []
    for j in range(RET_STEP_CHUNKS):
        rows = slice(j * C, (j + 1) * C)
        q = rot(cols_ref[rows, 0:W], rows)
        k = rot(cols_ref[rows, W:2 * W], rows) * (HEAD_DIM ** -0.5)
        v = cols_ref[rows, 2 * W:3 * W].astype(BF16)
        kb = k.astype(BF16)
        inner = jnp.zeros((C, W), F32)
        for h in range(RET_HEADS):
            sel = head_of_lane == h
            qh = jnp.where(sel, q, 0.0).astype(BF16)
            a = (_dot_nt(qh, kb) * decay_ref[h]).astype(BF16)
            inner = inner + jnp.where(sel, _dot(a, v), 0.0)
        kzT = (k * zeta_ref[...]).T.astype(BF16)
        kv = _dot(kzT, v)
        qs.append(q.astype(BF16))
        inners.append(inner)
        kvs.append(jnp.where(same_head, kv, 0.0))

    state = state_ref[...]
    for j in range(RET_STEP_CHUNKS):
        rows = slice(j * C, (j + 1) * C)
        o = inners[j] + _dot(qs[j], state.astype(BF16)) * xi_ref[...]
        state = grow_ref[...] * state + kvs[j]

        mu = jnp.zeros_like(o)
        for h in range(RET_HEADS):
            sel = head_of_lane == h
            mu = mu + jnp.where(sel, jnp.sum(jnp.where(sel, o, 0.0), axis=-1, keepdims=True), 0.0)
        mu = mu * (1.0 / HEAD_DIM)
        d = o - mu
        var = jnp.zeros_like(o)
        for h in range(RET_HEADS):
            sel = head_of_lane == h
            var = var + jnp.where(sel, jnp.sum(jnp.where(sel, d * d, 0.0), axis=-1, keepdims=True), 0.0)
        var = var * (1.0 / HEAD_DIM)
        y = d * lax.rsqrt(var + NORM_EPS) * nw_ref[...]
        g = cols_ref[rows, 3 * W:4 * W]
        o_ref[rows, :] = (jax.nn.silu(g) * y).astype(o_ref.dtype)
    state_ref[...] = state


def _retention(ret_cols, tables, norm_w, B, S):
    T = ret_cols.shape[0]
    C, W = RET_CHUNK, RET_W
    rows = RET_STEP_CHUNKS * C
    assert S % rows == 0
    n = S // rows
    cosf, sinf, decay, zeta, xi, g_rows = tables
    tok = lambda b, j: (b * n + j, 0)
    pos = lambda b, j: (j, 0)
    const2 = lambda b, j: (0, 0)
    return pl.pallas_call(
        _retention_kernel,
        grid=(B, n),
        in_specs=[
            pl.BlockSpec((rows, 4 * W), tok),
            pl.BlockSpec((rows, W), pos),
            pl.BlockSpec((rows, W), pos),
            _resident(decay.shape, lambda b, j: (0, 0, 0)),
            _resident((C, W), const2),
            _resident((C, W), const2),
            _resident((W, W), const2),
            _resident((1, W), const2),
        ],
        out_specs=pl.BlockSpec((rows, W), tok),
        out_shape=jax.ShapeDtypeStruct((T, W), BF16),
        scratch_shapes=[pltpu.VMEM((W, W), F32)],
        compiler_params=_params("parallel", "arbitrary"),
        name="retention",
    )(ret_cols, cosf, sinf, decay, zeta, xi, g_rows, norm_w)


def _sortable(x):
    b = pltpu.bitcast(x, I32)
    return b ^ ((b >> 31) & 0x7FFFFFFF)


def _dsa_kernel(*refs, n_sel, S):
    odd = pl.program_id(1) % 2 == 0
    pl.when(odd)(lambda: _dsa_block(*refs, n_sel=n_sel, S=S, lead=1))
    pl.when(jnp.logical_not(odd))(lambda: _dsa_block(*refs, n_sel=n_sel, S=S, lead=0))


def _dsa_block(qidxT_ref, dsaqT_ref, widxT_ref, kidx_ref, k_ref, vT_ref, o_ref,
               key_ref, hi_ref, lo_ref, digit_ref, tied_ref, tri_ref, qh_ref, s_ref, bias_ref, p_ref, acc_ref,
               *, n_sel, S, lead):
    QB, KC, SUB = DSA_QB, DSA_KC, DSA_SUB
    n_sub = KC // SUB
    i = pl.program_id(1)
    n_chunks = i + 1
    n_pairs = (n_chunks - lead) // 2
    pair_off = lambda c2: pl.multiple_of(c2 * (2 * KC) + lead * KC, KC)
    chunk_off = lambda c: c * KC if isinstance(c, int) else pl.multiple_of(c * KC, KC)
    qpos = i * QB + lax.broadcasted_iota(I32, (1, QB), 1)
    krow = lax.broadcasted_iota(I32, (SUB, 1), 0)
    w_scale = IDX_HEADS ** -0.5 * IDX_DIM ** -0.5
    widxT = widxT_ref[0] * w_scale

    def score_rows(off, n_rows, causal_mask):
        for sub in range(n_rows // DSA_SCORE_ROWS):
            r0 = off + sub * DSA_SCORE_ROWS
            kc = kidx_ref[pl.ds(r0, DSA_SCORE_ROWS), :]
            acc = jnp.zeros((DSA_SCORE_ROWS, QB), F32)
            for g in range(0, IDX_HEADS, DSA_IDX_GROUP):
                logits = [_dot(kc, qidxT_ref[0, h * IDX_DIM:(h + 1) * IDX_DIM, :])
                          for h in range(g, g + DSA_IDX_GROUP)]
                for h, logit in zip(range(g, g + DSA_IDX_GROUP), logits):
                    acc = acc + jnp.maximum(logit, 0.0) * widxT[h:h + 1, :]
            for part in range(DSA_SCORE_ROWS // SUB):
                a = acc[part * SUB:(part + 1) * SUB, :]
                a = jnp.where(a == 0.0, 0.0, a)
                if causal_mask:
                    a = jnp.where(r0 + part * SUB + krow <= qpos, a, -jnp.inf)
                key = _sortable(a)
                rows = pl.ds(r0 + part * SUB, SUB)
                key_ref[rows, :] = key
                hi_ref[rows, :] = (key >> 16).astype(I16)
                lo_ref[rows, :] = ((key & 0xFFFF) - 0x8000).astype(I16)

    def score_pair(c2, carry, causal_mask):
        score_rows(pair_off(c2), 2 * KC, causal_mask)
        return carry

    if lead:
        score_rows(0, KC, True)
        lax.fori_loop(0, n_pairs - 1, functools.partial(score_pair, causal_mask=False), 0)
        pl.when(n_pairs > 0)(lambda: score_rows(pair_off(n_pairs - 1), 2 * KC, True))
    else:
        lax.fori_loop(0, n_pairs - 1, functools.partial(score_pair, causal_mask=False), 0)
        score_pair(n_pairs - 1, 0, causal_mask=True)

    one_b, zero_b = jnp.ones((SUB, QB), BF16), jnp.zeros((SUB, QB), BF16)

    def count_packed(ref, pred):
        def tiles(cnt, off, n_tiles):
            for sub in range(n_tiles):
                cnt = cnt + jnp.where(pred(ref[pl.ds(off + sub * SUB, SUB), :]), one_b, zero_b)
            return cnt
        cnt = tiles(zero_b, 0, n_sub) if lead else zero_b
        cnt = lax.fori_loop(0, n_pairs, lambda c2, cnt: tiles(cnt, pair_off(c2), 2 * n_sub), cnt)
        return jnp.sum(cnt.astype(F32), axis=0, keepdims=True)

    def signed_digit(u):
        return (u - 0x8000).astype(I16)

    def search_digit(ref, n_above, n_ge_zero):
        def step(t, carry):
            d, n_ge, n_gt = carry
            cand = d | jnp.left_shift(jnp.int32(1), 15 - t)
            cand_s = signed_digit(cand)
            cnt = n_above + count_packed(ref, lambda x: x >= cand_s)
            keep = cnt >= n_sel
            return jnp.where(keep, cand, d), jnp.where(keep, cnt, n_ge), jnp.where(keep, n_gt, cnt)
        n_gt_init = jnp.zeros((1, QB), F32) + n_above
        return lax.fori_loop(0, 16, step, (jnp.zeros((1, QB), I32), n_ge_zero, n_gt_init))

    stored = jnp.full((1, QB), (n_chunks * KC).astype(F32), F32)
    u_hi, n_ge, n_above = search_digit(hi_ref, 0.0, stored)
    thr_hi = signed_digit(u_hi)
    lowest = jnp.full((SUB, QB), -0x8000, I16)

    def low_digits_of_ties(off, n_tiles):
        for sub in range(n_tiles):
            rows = pl.ds(off + sub * SUB, SUB)
            digit_ref[rows, :] = jnp.where(hi_ref[rows, :] == thr_hi, lo_ref[rows, :], lowest)

    def low_digits_pair(c2, carry):
        low_digits_of_ties(pair_off(c2), 2 * n_sub)
        return carry

    if lead:
        low_digits_of_ties(0, n_sub)
    lax.fori_loop(0, n_pairs, low_digits_pair, 0)
    u_lo, n_ge, n_gt = search_digit(digit_ref, n_above, n_ge)
    thr = ((u_hi - 0x8000) << 16) | u_lo

    needs_cut = (n_ge > n_sel) & (thr > KEY_NEG_INF)
    any_ties = jnp.max(jnp.where(needs_cut, 1.0, 0.0)) > 0.0

    qh_ref[...] = (dsaqT_ref[0] * (HEAD_DIM ** -0.5 * LOG2_E)).astype(BF16)
    acc_ref[...] = jnp.zeros_like(acc_ref)
    p_ref[...] = jnp.zeros_like(p_ref)

    thr_floor = jnp.maximum(thr, KEY_NEG_INF + 1)

    def write_bias(slot, c):
        off = chunk_off(c)
        for sub in range(n_sub):
            key = key_ref[pl.ds(off + sub * SUB, SUB), :]
            bias_ref[slot, sub * SUB:(sub + 1) * SUB, :] = jnp.where(key >= thr_floor, 0.0, MASKED_SCORE)

    def write_bias_ties(slot, c, tie_take, taken):
        off = chunk_off(c)
        for sub in range(n_sub):
            key = key_ref[pl.ds(off + sub * SUB, SUB), :]
            tied_ref[sub * SUB:(sub + 1) * SUB, :] = jnp.where(key == thr, 1.0, 0.0).astype(BF16)
        s_ref[slot, 0] = _dot(tri_ref[...], tied_ref[...])
        room = tie_take - taken
        for sub in range(n_sub):
            rows = slice(sub * SUB, (sub + 1) * SUB)
            key = key_ref[pl.ds(off + sub * SUB, SUB), :]
            tied_bias = jnp.where(s_ref[slot, 0, rows, :] < room, 0.0, MASKED_SCORE)
            bias_ref[slot, rows, :] = jnp.where(key > thr, 0.0, jnp.where(key == thr, tied_bias, MASKED_SCORE))
        return taken + s_ref[slot, 0, KC - 1:KC, :] + tied_ref[KC - 1:KC, :].astype(F32)

    def issue_scores(slot, c):
        rows = pl.ds(chunk_off(c), KC)
        for h in range(DSA_HEADS):
            s_ref[slot, h] = _dot(k_ref[h, rows, :], qh_ref[h * HEAD_DIM:(h + 1) * HEAD_DIM, :])

    def pv_update(slot, c, alphas):
        vch = vT_ref[c]
        pvs = [_dot(vch[h * HEAD_DIM:(h + 1) * HEAD_DIM, :], p_ref[slot, h]) for h in range(DSA_HEADS)]
        for h in range(DSA_HEADS):
            acc_ref[h] = alphas[h] * acc_ref[h] + pvs[h]

    def softmax_chunk(slot, ms, ls):
        new_m, new_l, alphas = [], [], []
        for h in range(DSA_HEADS):
            mx = jnp.full((SUB, QB), MASKED_SCORE, F32)
            for sub in range(n_sub):
                rows = slice(sub * SUB, (sub + 1) * SUB)
                t = s_ref[slot, h, rows, :] + bias_ref[slot, rows, :]
                s_ref[slot, h, rows, :] = t
                mx = jnp.maximum(mx, t)
            m_new = jnp.maximum(ms[h], jnp.max(mx, axis=0, keepdims=True))
            alpha = jnp.exp2(ms[h] - m_new)
            psum = jnp.zeros((SUB, QB), F32)
            for sub in range(n_sub):
                rows = slice(sub * SUB, (sub + 1) * SUB)
                p = jnp.exp2(s_ref[slot, h, rows, :] - m_new)
                psum = psum + p
                p_ref[slot, h, rows, :] = p.astype(BF16)
            new_m.append(m_new)
            new_l.append(alpha * ls[h] + jnp.sum(psum, axis=0, keepdims=True))
            alphas.append(alpha)
        return tuple(new_m), tuple(new_l), tuple(alphas)

    def attend(ties):
        if ties:
            tie_take = jnp.where(thr > KEY_NEG_INF, n_sel - n_gt, 0.0)
            ri = lax.broadcasted_iota(I32, (KC, KC), 0)
            ci = lax.broadcasted_iota(I32, (KC, KC), 1)
            tri_ref[...] = jnp.where(ci < ri, 1.0, 0.0).astype(BF16)

        def attend_pair(c2, carry):
            ms, ls, prev_alphas, taken = carry
            ca, cb = lead + 2 * c2, lead + 2 * c2 + 1
            if ties:
                taken = write_bias_ties(0, ca, tie_take, taken)
                taken = write_bias_ties(1, cb, tie_take, taken)
            else:
                write_bias(0, ca)
                write_bias(1, cb)
            issue_scores(0, ca)
            pv_update(1, jnp.maximum(ca - 1, 0), prev_alphas)
            issue_scores(1, cb)
            ms, ls, alphas_a = softmax_chunk(0, ms, ls)
            pv_update(0, ca, alphas_a)
            ms, ls, alphas_b = softmax_chunk(1, ms, ls)
            return ms, ls, alphas_b, taken

        init = (tuple(jnp.full((1, QB), MASKED_SCORE, F32) for _ in range(DSA_HEADS)),
                tuple(jnp.zeros((1, QB), F32) for _ in range(DSA_HEADS)),
                tuple(jnp.ones((1, QB), F32) for _ in range(DSA_HEADS)),
                jnp.zeros((1, QB), F32))
        if lead:
            ms, ls, _, taken = init
            if ties:
                taken = write_bias_ties(1, 0, tie_take, taken)
            else:
                write_bias(1, 0)
            issue_scores(1, 0)
            ms, ls, lead_alphas = softmax_chunk(1, ms, ls)
            init = (ms, ls, lead_alphas, taken)
        _, ls, last_alphas, _ = lax.fori_loop(0, n_pairs, attend_pair, init)
        pv_update(1, n_chunks - 1, last_alphas)
        for h in range(DSA_HEADS):
            acc_ref[h] = acc_ref[h] / ls[h]

    pl.when(any_ties)(lambda: attend(True))
    pl.when(jnp.logical_not(any_ties))(lambda: attend(False))
    o_ref[...] = acc_ref[...].reshape(DSA_W, QB).T.astype(o_ref.dtype)


def _dsa(qidxT, dsaqT, widxT, kidx, k, vT, B, S):
    QB, KC = DSA_QB, DSA_KC
    assert QB == KC and S % (2 * KC) == 0
    n_sel = min(TOPK_MAX, S // 4)
    assert n_sel <= KC, "the first causal chunk must be able to hold every selected key"
    assert S // DSA_SUB <= 256, "packed counts are exact only up to 256 adds per accumulator lane"
    nq = S // QB
    nkc = S // KC
    n_sel = min(TOPK_MAX, S // 4)
    T = B * S
    kern = functools.partial(_dsa_kernel, n_sel=n_sel, S=S)
    return pl.pallas_call(
        kern,
        grid=(B, nq),
        in_specs=[
            pl.BlockSpec((1, IDX_HEADS * IDX_DIM, QB), lambda b, i: (b, 0, i)),
            pl.BlockSpec((1, DSA_W, QB), lambda b, i: (b, 0, i)),
            pl.BlockSpec((1, widxT.shape[1], QB), lambda b, i: (b, 0, i)),
            _resident((S, IDX_DIM), lambda b, i: (b, 0)),
            _resident((DSA_HEADS, S, HEAD_DIM), lambda b, i: (0, b, 0)),
            _resident((nkc, DSA_W, KC), lambda b, i: (b, 0, 0)),
        ],
        out_specs=pl.BlockSpec((QB, DSA_W), lambda b, i: (b * nq + i, 0)),
        out_shape=jax.ShapeDtypeStruct((T, DSA_W), BF16),
        scratch_shapes=[
            pltpu.VMEM((S, QB), I32),
            pltpu.VMEM((S, QB), I16),
            pltpu.VMEM((S, QB), I16),
            pltpu.VMEM((S, QB), I16),
            pltpu.VMEM((KC, QB), BF16),
            pltpu.VMEM((KC, KC), BF16),
            pltpu.VMEM((DSA_W, QB), BF16),
            pltpu.VMEM((2, DSA_HEADS, KC, QB), F32),
            pltpu.VMEM((2, KC, QB), F32),
            pltpu.VMEM((2, DSA_HEADS, KC, QB), BF16),
            pltpu.VMEM((DSA_HEADS, HEAD_DIM, QB), F32),
        ],
        compiler_params=_params("parallel", "arbitrary"),
        name="dsa",
    )(qidxT, dsaqT, widxT, kidx, k, vT)


def _dil_units(dilation):
    blocks = DIL_SPAN // (DIL_BLOCK * dilation)
    return [(res, jb) for res in range(dilation) for jb in range(blocks)]


def _dilated_fused_kernel(q_ref, kp_ref, kc_ref, vp_ref, vc_ref, out_ref,
                          qf_ref, kf_ref, vf_ref, s_ref, p_ref, num_ref, den_ref, max_ref):
    L, SPAN = DIL_BLOCK, DIL_SPAN
    first_span = pl.program_id(1) == 0
    qf_ref[...] = q_ref[...].astype(F32) * (HEAD_DIM ** -0.5)
    kf_ref[0:SPAN, :] = kp_ref[...].astype(F32)
    kf_ref[SPAN:, :] = kc_ref[...].astype(F32)
    vf_ref[0:SPAN, :] = vp_ref[...].astype(F32)
    vf_ref[SPAN:, :] = vc_ref[...].astype(F32)

    lane = lax.broadcasted_iota(I32, (1, 2 * HEAD_DIM), 1)
    low = lane < HEAD_DIM
    qi = lax.broadcasted_iota(I32, (L, 2 * L), 0)
    kj = lax.broadcasted_iota(I32, (L, 2 * L), 1)
    dist = L + qi - kj
    band = (dist >= 0) & (dist <= L)
    bias_band = jnp.where(band, 0.0, -jnp.inf)
    bias_head = jnp.where(band & ((kj >= L) | jnp.logical_not(first_span)), 0.0, -jnp.inf)

    def rows(start, count, stride):
        return pl.ds(start, count, stride=stride) if stride > 1 else pl.ds(start, count)

    for branch, (window, r) in enumerate(DIL_PATTERNS):
        units = _dil_units(r)
        for g0 in range(0, len(units), DIL_GROUP):
            group = units[g0:g0 + DIL_GROUP]
            vvs = []
            for u, (res, jb) in enumerate(group):
                q = qf_ref[rows(res + r * L * jb, L, r), :].astype(BF16)
                k0 = SPAN + res + r * L * (jb - 1)
                kk = kf_ref[rows(k0, 2 * L, r), :].astype(BF16)
                vv = vf_ref[rows(k0, 2 * L, r), :]
                for sub in range(2):
                    mine = low if sub == 0 else jnp.logical_not(low)
                    s_ref[2 * u + sub] = _dot_nt(jnp.where(mine, q, jnp.zeros_like(q)), kk)
                    vvs.append(jnp.where(mine, vv, 1.0).astype(BF16))
            ms = []
            for u, (res, jb) in enumerate(group):
                bias = bias_head if jb == 0 else bias_band
                for sub in range(2):
                    t = s_ref[2 * u + sub] + bias
                    m = jnp.max(t, axis=-1, keepdims=True)
                    p_ref[2 * u + sub] = jnp.exp(t - m).astype(BF16)
                    ms.append(m)
            for u, (res, jb) in enumerate(group):
                halves = []
                for sub in range(2):
                    nd = _dot(p_ref[2 * u + sub], vvs[2 * u + sub])
                    halves.append((nd, pltpu.roll(nd, HEAD_DIM, 1), ms[2 * u + sub]))
                (n0, d0, m0), (n1, d1, m1) = halves
                dst = rows(res + r * L * jb, L, r)
                num_ref[branch, dst, :] = jnp.where(low, n0, n1)
                den_ref[branch, dst, :] = jnp.where(low, d0, d1)
                max_ref[branch, dst, :] = jnp.where(low, m0, m1)

    for piece in range(SPAN // DIL_MERGE_ROWS):
        rs = slice(piece * DIL_MERGE_ROWS, (piece + 1) * DIL_MERGE_ROWS)
        a, b, c = max_ref[0, rs, :], max_ref[1, rs, :], max_ref[2, rs, :]
        m = jnp.maximum(jnp.maximum(a, b), c)
        ea, eb, ec = jnp.exp(a - m), jnp.exp(b - m), jnp.exp(c - m)
        num = ea * num_ref[0, rs, :] + eb * num_ref[1, rs, :] + ec * num_ref[2, rs, :]
        den = ea * den_ref[0, rs, :] + eb * den_ref[1, rs, :] + ec * den_ref[2, rs, :]
        out_ref[rs, :] = (num / den).astype(out_ref.dtype)


def _dilated_fused(q, k, v, B, S):
    T, W = q.shape
    SPAN, L = DIL_SPAN, DIL_BLOCK
    for window, r in DIL_PATTERNS:
        assert window // r == L and SPAN % (L * r) == 0 and L * r <= SPAN
    assert S % SPAN == 0
    ns = S // SPAN
    PW = 2 * HEAD_DIM
    own = lambda b, t, hp: (b * ns + t, hp)
    prev = lambda b, t, hp: (b * ns + jnp.maximum(t - 1, 0), hp)
    blk = (SPAN, PW)
    return pl.pallas_call(
        _dilated_fused_kernel,
        grid=(B, ns, W // PW),
        in_specs=[pl.BlockSpec(blk, own), pl.BlockSpec(blk, prev), pl.BlockSpec(blk, own),
                  pl.BlockSpec(blk, prev), pl.BlockSpec(blk, own)],
        out_specs=pl.BlockSpec(blk, own),
        out_shape=jax.ShapeDtypeStruct((T, W), BF16),
        scratch_shapes=[
            pltpu.VMEM((SPAN, PW), F32),
            pltpu.VMEM((2 * SPAN, PW), F32),
            pltpu.VMEM((2 * SPAN, PW), F32),
            pltpu.VMEM((2 * DIL_GROUP, L, 2 * L), F32),
            pltpu.VMEM((2 * DIL_GROUP, L, 2 * L), BF16),
            pltpu.VMEM((len(DIL_PATTERNS), SPAN, PW), F32),
            pltpu.VMEM((len(DIL_PATTERNS), SPAN, PW), F32),
            pltpu.VMEM((len(DIL_PATTERNS), SPAN, PW), F32),
        ],
        compiler_params=_params("parallel", "parallel", "parallel"),
        name="dilated",
    )(q, k, k, v, v)


def _mlp_kernel(x_ref, ret_ref, dsa_ref, dil_ref, wout_ref, nw_ref, wup_ref, wdown_ref, fnw_ref, o_ref, *,
                apply_final_norm):
    mixed = (_dot(ret_ref[...], wout_ref[0:RET_W, :])
             + _dot(dsa_ref[...], wout_ref[RET_W:RET_W + DSA_W, :])
             + _dot(dil_ref[...], wout_ref[RET_W + DSA_W:, :]))
    x = x_ref[...] + mixed
    h = _rms(x, nw_ref[...]).astype(BF16)
    ff = None
    for c in range(wup_ref.shape[1] // FF_CHUNK):
        cs = slice(c * FF_CHUNK, (c + 1) * FF_CHUNK)
        u = jnp.maximum(_dot(h, wup_ref[:, cs]), 0.0)
        part = _dot((u * u).astype(BF16), wdown_ref[cs, :])
        ff = part if ff is None else ff + part
    out = x + ff
    o_ref[...] = _rms(out, fnw_ref[...]) if apply_final_norm else out


def _mlp(x2, ret_o, dsa_o, dil_o, w_out, nw, w_up, w_down, final_nw, *, apply_final_norm):
    T, D = x2.shape
    tm = ROW_TILE
    row = lambda i: (i, 0)
    const = lambda i: (0, 0)
    return pl.pallas_call(
        functools.partial(_mlp_kernel, apply_final_norm=apply_final_norm),
        grid=(T // tm,),
        in_specs=[
            pl.BlockSpec((tm, D), row),
            pl.BlockSpec((tm, RET_W), row),
            pl.BlockSpec((tm, DSA_W), row),
            pl.BlockSpec((tm, DIL_W), row),
            _resident(w_out.shape, const),
            _resident((1, D), const),
            _resident(w_up.shape, const),
            _resident(w_down.shape, const),
            _resident((1, D), const),
        ],
        out_specs=pl.BlockSpec((tm, D), row),
        out_shape=jax.ShapeDtypeStruct((T, D), F32),
        compiler_params=_params("parallel"),
        name="outproj_mlp",
    )(x2, ret_o, dsa_o, dil_o, w_out, nw, w_up, w_down, final_nw)


def _split_w_in(w_in):
    ret_cols = 4 * RET_W
    o = ret_cols
    dsa_q = w_in[:, o:o + DSA_W]; o += DSA_W
    c_kv = w_in[:, o:o + KV_LATENT]; o += KV_LATENT
    q_idx = w_in[:, o:o + IDX_HEADS * IDX_DIM]; o += IDX_HEADS * IDX_DIM
    k_idx = w_in[:, o:o + IDX_DIM]; o += IDX_DIM
    w_idx = w_in[:, o:o + IDX_HEADS]; o += IDX_HEADS
    dil = w_in[:, o:]
    w_main = jnp.concatenate([w_in[:, :ret_cols], dil, k_idx, c_kv], axis=1).astype(BF16)
    pad = jnp.zeros((w_in.shape[0], WIDX_ROWS - IDX_HEADS), w_in.dtype)
    w_t = jnp.concatenate([q_idx, dsa_q, w_idx, pad], axis=1).T.astype(BF16)
    return w_main, w_t


def kernel(x, attn_norm_w, w_in, ret_norm_w, dsa_kv_norm_w, dsa_w_uk, dsa_w_uv, w_out, mlp_norm_w, w_up, w_down, final_norm_w):
    B, S, D = x.shape
    depth = w_in.shape[0]
    assert S % ROW_TILE == 0 and S % DSA_KC == 0 and S % RET_CHUNK == 0
    tables = _retention_tables(S)
    x2 = x.reshape(B * S, D)
    for layer in range(depth):
        w_main, w_t = _split_w_in(w_in[layer])
        ret_cols, dq, dk, dv, kidx, qidxT, dsaqT, widxT, k, vT = _inproj(
            x2, attn_norm_w[layer][None, :], w_main, w_t, dsa_kv_norm_w[layer][None, :],
            dsa_w_uk[layer].astype(BF16), dsa_w_uv[layer].T.astype(BF16), B, S)
        ret_o = _retention(ret_cols, tables, ret_norm_w[layer].reshape(1, RET_W), B, S)
        dsa_o = _dsa(qidxT, dsaqT, widxT, kidx, k, vT, B, S)
        dil_o = _dilated_fused(dq, dk, dv, B, S)
        x2 = _mlp(x2, ret_o, dsa_o, dil_o, w_out[layer].astype(BF16), mlp_norm_w[layer][None, :],
                  w_up[layer].astype(BF16), w_down[layer].astype(BF16), final_norm_w[None, :],
                  apply_final_norm=layer == depth - 1)
    return x2.reshape(B, S, D)
```
